```python
import math
import jax, jax.numpy as jnp
from jax import lax
import numpy as np

D_MODEL = 2048
BATCH = 8
SEQ = 4096
DEPTH = 4

N_BRANCH = 4
BRANCH_W = D_MODEL // 4
RET_HEADS = 4
RET_DK = BRANCH_W // (2 * RET_HEADS)
RET_DV = BRANCH_W // RET_HEADS
RET_CHUNK = 128
ROPE_BASE = 10000.0
S5_GROUP = 16
S5_GROUPS = BRANCH_W // S5_GROUP
S5_STATE = 64
GLA_HEADS = 4
GLA_DK = BRANCH_W // (2 * GLA_HEADS)
GLA_DV = BRANCH_W // GLA_HEADS
GLA_RANK = 16
GLA_GATE_TEMP = 16.0
GLA_CHUNK = 16
POOL_WINDOWS = (2, 4, 8, 16)
POOL_GROUP = BRANCH_W // len(POOL_WINDOWS)
POOL_PAD = max(POOL_WINDOWS)
D_FF = 5504
CONV_W = 3
EPS = 1e-6

IN_SIZES = (
    RET_HEADS * RET_DK, RET_HEADS * RET_DK, RET_HEADS * RET_DV, BRANCH_W,
    BRANCH_W,
    GLA_HEADS * GLA_DK, GLA_HEADS * GLA_DK, GLA_HEADS * GLA_DV, BRANCH_W, GLA_RANK,
    BRANCH_W,
    N_BRANCH * D_MODEL,
)
N_IN = sum(IN_SIZES)

kernel_name = 'hybrid_gated_parallel_mixer'


def _split_points():
    pts, acc = [], 0
    for s in IN_SIZES[:-1]:
        acc += s
        pts.append(acc)
    return pts


def rmsnorm(x, g):
    xf = x.astype(jnp.float32)
    r = lax.rsqrt(jnp.mean(xf * xf, axis=-1, keepdims=True) + EPS)
    return (xf * r * g.astype(jnp.float32)).astype(x.dtype)


def head_norm(o):
    mu = jnp.mean(o, axis=-1, keepdims=True)
    var = jnp.mean(jnp.square(o - mu), axis=-1, keepdims=True)
    return (o - mu) * lax.rsqrt(var + EPS)


def rotary(x, positions):
    d = x.shape[-1]
    inv = ROPE_BASE ** (-jnp.arange(0, d, 2, dtype=jnp.float32) / d)
    ang = positions.astype(jnp.float32)[..., None] * inv
    cos = jnp.cos(ang)[:, :, None, :]
    sin = jnp.sin(ang)[:, :, None, :]
    x1, x2 = x[..., : d // 2], x[..., d // 2:]
    return jnp.concatenate([x1 * cos - x2 * sin, x1 * sin + x2 * cos], axis=-1)


def retention(q, k, v):
    B_, S_, H, dk = q.shape
    dv = v.shape[-1]
    C = RET_CHUNK
    n = S_ // C
    log_g = jnp.log(1.0 - 2.0 ** (-5.0 - jnp.arange(H, dtype=jnp.float32)))
    q = q.reshape(B_, n, C, H, dk)
    k = k.reshape(B_, n, C, H, dk)
    v = v.reshape(B_, n, C, H, dv)
    idx = jnp.arange(C, dtype=jnp.float32)
    rel = idx[:, None] - idx[None, :]
    causal = rel >= 0
    decay = jnp.where(causal[None], jnp.exp(jnp.where(causal, rel, 0.0)[None] * log_g[:, None, None]), 0.0)
    scores = jnp.einsum('bnthd,bnshd->bnhts', q, k) * decay
    intra = jnp.einsum('bnhts,bnshe->bnthe', scores, v)
    k_dec = k * jnp.exp((C - 1.0 - idx)[:, None] * log_g[None, :])[None, None, :, :, None]
    chunk_state = jnp.einsum('bnshd,bnshe->nbhde', k_dec, v)
    chunk_decay = jnp.exp(C * log_g)[None, :, None, None]

    def step(R, s):
        return chunk_decay * R + s, R

    _, R_prev = lax.scan(step, jnp.zeros((B_, H, dk, dv), jnp.float32), chunk_state)
    q_dec = q * jnp.exp((idx + 1.0)[:, None] * log_g[None, :])[None, None, :, :, None]
    cross = jnp.einsum('bnthd,nbhde->bnthe', q_dec, R_prev)
    return (intra + cross).reshape(B_, S_, H, dv)


def s5_branch(u, a_re, a_im, log_dt, b_re, b_im, c_re, c_im, d_skip, w_glu):
    B_, S_, _ = u.shape
    ug = u.reshape(B_, S_, S5_GROUPS, S5_GROUP)
    a_re = a_re.astype(jnp.float32)
    a_im = a_im.astype(jnp.float32)
    dt = jnp.exp(log_dt.astype(jnp.float32))[:, None]
    mag = jnp.exp(a_re * dt)
    abar_re = mag * jnp.cos(a_im * dt)
    abar_im = mag * jnp.sin(a_im * dt)
    den = a_re * a_re + a_im * a_im
    nr, ni = abar_re - 1.0, abar_im
    f_re = (nr * a_re + ni * a_im) / den
    f_im = (ni * a_re - nr * a_im) / den
    b_re = b_re.astype(jnp.float32)
    b_im = b_im.astype(jnp.float32)
    bb_re = f_re[..., None] * b_re - f_im[..., None] * b_im
    bb_im = f_re[..., None] * b_im + f_im[..., None] * b_re
    bu_re = jnp.einsum('bsgc,gpc->bsgp', ug, bb_re)
    bu_im = jnp.einsum('bsgc,gpc->bsgp', ug, bb_im)
    ar = jnp.broadcast_to(abar_re[None, None], bu_re.shape)
    ai = jnp.broadcast_to(abar_im[None, None], bu_re.shape)

    def combine(l, r):
        ar1, ai1, br1, bi1 = l
        ar2, ai2, br2, bi2 = r
        return (ar2 * ar1 - ai2 * ai1, ar2 * ai1 + ai2 * ar1,
                ar2 * br1 - ai2 * bi1 + br2, ar2 * bi1 + ai2 * br1 + bi2)

    _, _, xr, xi = lax.associative_scan(combine, (ar, ai, bu_re, bu_im), axis=1)
    y = jnp.einsum('bsgp,gcp->bsgc', xr, c_re) - jnp.einsum('bsgp,gcp->bsgc', xi, c_im)
    y = y + d_skip.reshape(S5_GROUPS, S5_GROUP) * ug
    y = jax.nn.gelu(y.reshape(B_, S_, BRANCH_W))
    return y * jax.nn.sigmoid(y @ w_glu)


def gla(q, k, v, log_a):
    B_, S_, H, dk = q.shape
    dv = v.shape[-1]
    C = GLA_CHUNK
    n = S_ // C
    q = q.reshape(B_, n, C, H, dk)
    k = k.reshape(B_, n, C, H, dk)
    v = v.reshape(B_, n, C, H, dv)
    b = jnp.cumsum(log_a.reshape(B_, n, C, H, dk), axis=2)
    idx = jnp.arange(C)
    mask = (idx[:, None] >= idx[None, :])[None, None, :, :, None, None]
    diff = b[:, :, :, None] - b[:, :, None, :]
    w = jnp.where(mask, jnp.exp(jnp.where(mask, diff, 0.0)), 0.0)
    scores = jnp.einsum('bnthd,bnshd,bntshd->bnhts', q, k, w)
    intra = jnp.einsum('bnhts,bnshe->bnthe', scores, v)
    b_last = b[:, :, -1]
    k_dec = k * jnp.exp(b_last[:, :, None] - b)
    chunk_state = jnp.einsum('bnshd,bnshe->nbhde', k_dec, v)
    chunk_decay = jnp.moveaxis(jnp.exp(b_last), 1, 0)

    def step(Sst, inp):
        s_in, dec = inp
        return dec[..., None] * Sst + s_in, Sst

    _, S_prev = lax.scan(step, jnp.zeros((B_, H, dk, dv), jnp.float32), (chunk_state, chunk_decay))
    cross = jnp.einsum('bnthd,nbhde->bnthe', q * jnp.exp(b), S_prev)
    return (intra + cross).reshape(B_, S_, H, dv)


def pool_branch(p, pool_w, pool_scale):
    B_, S_, _ = p.shape
    G = len(POOL_WINDOWS)
    pg = p.reshape(B_, S_, G, POOL_GROUP)
    cs = jnp.cumsum(pg, axis=1)
    cs_pad = jnp.concatenate([jnp.zeros((B_, POOL_PAD, G, POOL_GROUP), cs.dtype), cs], axis=1)
    pos1 = jnp.arange(S_, dtype=jnp.float32) + 1.0
    means = []
    for g, win in enumerate(POOL_WINDOWS):
        prev = cs_pad[:, POOL_PAD - win: POOL_PAD - win + S_, g]
        cnt = jnp.minimum(pos1, float(win))[None, :, None]
        means.append((cs[:, :, g] - prev) / cnt)
    mixed = jnp.stack(means, axis=2) - pg
    out = jnp.einsum('bsgc,gcd->bsgd', mixed, pool_w).reshape(B_, S_, BRANCH_W)
    return out * pool_scale


def hybrid_mixer(h, positions, w_in, s5_a_re, s5_a_im, s5_log_dt, s5_b_re, s5_b_im, s5_c_re, s5_c_im,
                 s5_d, s5_w_glu, gla_w_gate, gla_b_gate, pool_w, pool_scale, w_branch, w_out):
    B_, S_, _ = h.shape
    proj = (h @ w_in).astype(jnp.float32)
    rq, rk, rv, rg, su, gq, gk, gv, gg, gr, pu, gate_logits = jnp.split(proj, _split_points(), axis=-1)
    rq = rotary(rq.reshape(B_, S_, RET_HEADS, RET_DK), positions)
    rk = rotary(rk.reshape(B_, S_, RET_HEADS, RET_DK), positions) * (RET_DK ** -0.5)
    ra = retention(rq, rk, rv.reshape(B_, S_, RET_HEADS, RET_DV))
    ya = head_norm(ra).reshape(B_, S_, BRANCH_W) * jax.nn.silu(rg)
    yb = s5_branch(su, s5_a_re, s5_a_im, s5_log_dt, s5_b_re, s5_b_im, s5_c_re, s5_c_im, s5_d, s5_w_glu)
    log_a = jax.nn.log_sigmoid(gr @ gla_w_gate + gla_b_gate) / GLA_GATE_TEMP
    gc = gla(gq.reshape(B_, S_, GLA_HEADS, GLA_DK) * (GLA_DK ** -0.5),
             gk.reshape(B_, S_, GLA_HEADS, GLA_DK),
             gv.reshape(B_, S_, GLA_HEADS, GLA_DV),
             log_a.reshape(B_, S_, GLA_HEADS, GLA_DK))
    yc = head_norm(gc).reshape(B_, S_, BRANCH_W) * jax.nn.silu(gg)
    yd = pool_branch(pu, pool_w, pool_scale)
    branches = jnp.stack([ya, yb, yc, yd], axis=2)
    up = jnp.einsum('bsnc,ncd->bsnd', branches, w_branch)
    gates = jax.nn.sigmoid(gate_logits.reshape(B_, S_, N_BRANCH, D_MODEL))
    merged = jnp.sum(gates * up, axis=2)
    return merged @ w_out


def conv_ffn(h, w_up, conv_w, conv_b, w_down):
    S_ = h.shape[1]
    u = h @ w_up
    up = jnp.pad(u, ((0, 0), (CONV_W - 1, 0), (0, 0)))
    uc = conv_b + up[:, 0:S_] * conv_w[0] + up[:, 1:S_ + 1] * conv_w[1] + up[:, 2:S_ + 2] * conv_w[2]
    a, v = jnp.split(uc, 2, axis=-1)
    return (jax.nn.silu(a) * v) @ w_down


def setup_inputs(seed: int = 0) -> dict:
    key = jax.random.key(seed)
    ks = jax.random.split(key, 32)
    f32 = jnp.float32
    L = DEPTH

    def nrm(k, shape, scale):
        return jax.random.normal(k, shape, f32) * scale

    x = nrm(ks[0], (BATCH, SEQ, D_MODEL), 1.0)
    offsets = jax.random.randint(ks[1], (BATCH, 1), 0, 1024, dtype=jnp.int32)
    positions = offsets + jnp.arange(SEQ, dtype=jnp.int32)[None, :]
    norm_mix_g = 1.0 + nrm(ks[2], (L, D_MODEL), 0.02)
    w_in = nrm(ks[3], (L, D_MODEL, N_IN), D_MODEL ** -0.5)
    n_idx = jnp.arange(S5_STATE, dtype=f32)
    s5_a_re = -0.5 + nrm(ks[4], (L, S5_GROUPS, S5_STATE), 0.01)
    s5_a_im = math.pi * n_idx + nrm(ks[5], (L, S5_GROUPS, S5_STATE), 0.01)
    s5_log_dt = jax.random.uniform(ks[6], (L, S5_GROUPS), f32, math.log(1e-3), math.log(1e-1))
    s5_b_re = nrm(ks[7], (L, S5_GROUPS, S5_STATE, S5_GROUP), (2.0 * S5_GROUP) ** -0.5)
    s5_b_im = nrm(ks[8], (L, S5_GROUPS, S5_STATE, S5_GROUP), (2.0 * S5_GROUP) ** -0.5)
    s5_c_re = nrm(ks[9], (L, S5_GROUPS, S5_GROUP, S5_STATE), 2.0 ** -0.5)
    s5_c_im = nrm(ks[10], (L, S5_GROUPS, S5_GROUP, S5_STATE), 2.0 ** -0.5)
    s5_d = nrm(ks[11], (L, BRANCH_W), 1.0)
    s5_w_glu = nrm(ks[12], (L, BRANCH_W, BRANCH_W), BRANCH_W ** -0.5)
    gla_w_gate = nrm(ks[13], (L, GLA_RANK, GLA_HEADS * GLA_DK), GLA_RANK ** -0.5)
    gla_b_gate = nrm(ks[14], (L, GLA_HEADS * GLA_DK), 0.01)
    pool_w = nrm(ks[15], (L, len(POOL_WINDOWS), POOL_GROUP, POOL_GROUP), POOL_GROUP ** -0.5)
    pool_scale = 1.0 + nrm(ks[16], (L, BRANCH_W), 0.1)
    w_branch = nrm(ks[17], (L, N_BRANCH, BRANCH_W, D_MODEL), BRANCH_W ** -0.5)
    w_out = nrm(ks[18], (L, D_MODEL, D_MODEL), D_MODEL ** -0.5)
    norm_ffn_g = 1.0 + nrm(ks[19], (L, D_MODEL), 0.02)
    w_up = nrm(ks[20], (L, D_MODEL, 2 * D_FF), D_MODEL ** -0.5)
    conv_w = nrm(ks[21], (L, CONV_W, 2 * D_FF), CONV_W ** -0.5)
    conv_b = nrm(ks[22], (L, 2 * D_FF), 0.01)
    w_down = nrm(ks[23], (L, D_FF, D_MODEL), D_FF ** -0.5)
    final_g = 1.0 + nrm(ks[24], (D_MODEL,), 0.02)
    return {'x': x, 'positions': positions, 'norm_mix_g': norm_mix_g, 'w_in': w_in,
            's5_a_re': s5_a_re, 's5_a_im': s5_a_im, 's5_log_dt': s5_log_dt,
            's5_b_re': s5_b_re, 's5_b_im': s5_b_im, 's5_c_re': s5_c_re, 's5_c_im': s5_c_im,
            's5_d': s5_d, 's5_w_glu': s5_w_glu, 'gla_w_gate': gla_w_gate, 'gla_b_gate': gla_b_gate,
            'pool_w': pool_w, 'pool_scale': pool_scale, 'w_branch': w_branch, 'w_out': w_out,
            'norm_ffn_g': norm_ffn_g, 'w_up': w_up, 'conv_w': conv_w, 'conv_b': conv_b,
            'w_down': w_down, 'final_g': final_g}


def reference(x, positions, norm_mix_g, w_in, s5_a_re, s5_a_im, s5_log_dt, s5_b_re, s5_b_im,
              s5_c_re, s5_c_im, s5_d, s5_w_glu, gla_w_gate, gla_b_gate, pool_w, pool_scale,
              w_branch, w_out, norm_ffn_g, w_up, conv_w, conv_b, w_down, final_g):
    h = x
    for l in range(DEPTH):
        hn = rmsnorm(h, norm_mix_g[l])
        mix = hybrid_mixer(hn, positions, w_in[l], s5_a_re[l], s5_a_im[l], s5_log_dt[l],
                           s5_b_re[l], s5_b_im[l], s5_c_re[l], s5_c_im[l], s5_d[l], s5_w_glu[l],
                           gla_w_gate[l], gla_b_gate[l], pool_w[l], pool_scale[l], w_branch[l], w_out[l])
        h = h + mix.astype(h.dtype)
        hn = rmsnorm(h, norm_ffn_g[l])
        h = h + conv_ffn(hn, w_up[l], conv_w[l], conv_b[l], w_down[l]).astype(h.dtype)
    return rmsnorm(h, final_g)
```

```python
import functools
import math

import numpy as np
import jax
import jax.numpy as jnp
from jax import lax
from jax.experimental import pallas as pl
from jax.experimental.pallas import tpu as pltpu

F32 = jnp.float32
BF16 = jnp.bfloat16
EPS = 1e-6

N_BRANCH = 4
RET_HEADS = 4
GLA_HEADS = 4
GLA_RANK = 16
GLA_GATE_TEMP = 16.0
S5_GROUP = 16
S5_STATE = 64
POOL_WINDOWS = (2, 4, 8, 16)
POOL_PAD = 16
ROPE_BASE = 10000.0
CONV_W = 3

LANE = 128
CHUNK = 128
VMEM_LIMIT = 56 * 1024 * 1024
ROW_TILE = 512
FF_TILE = 512
SEQ_TILE = 512
S5_TILE = 64


def _cparams(sem):
    return pltpu.CompilerParams(dimension_semantics=sem, vmem_limit_bytes=VMEM_LIMIT)


def _const_spec(shape):
    nd = len(shape)
    return pl.BlockSpec(shape, lambda *_: (0,) * nd, pipeline_mode=pl.Buffered(1))


def _dot(a, b):
    return jnp.dot(a, b, preferred_element_type=F32)


def _dot_nt(a, b):
    return lax.dot_general(a, b, (((1,), (1,)), ((), ())), preferred_element_type=F32)


def _rms(x, g):
    r = lax.rsqrt(jnp.mean(x * x, axis=-1, keepdims=True) + EPS)
    return x * r * g


def _head_norm(o):
    mu = jnp.mean(o, axis=-1, keepdims=True)
    d = o - mu
    var = jnp.mean(d * d, axis=-1, keepdims=True)
    return d * lax.rsqrt(var + EPS)


def _silu(x):
    return x * jax.nn.sigmoid(x)


def _copy_kernel(x_ref, o_ref):
    o_ref[...] = x_ref[...]


def _to_time_major(x):
    B, S, D = x.shape
    tt = min(S, SEQ_TILE)
    return pl.pallas_call(
        _copy_kernel,
        grid=(B, S // tt),
        in_specs=[pl.BlockSpec((None, tt, D), lambda b, t: (b, t, 0))],
        out_specs=pl.BlockSpec((tt, D), lambda b, t: (t, b)),
        out_shape=jax.ShapeDtypeStruct((S, B * D), x.dtype),
        compiler_params=_cparams(("parallel", "parallel")),
        name="to_time_major",
    )(x)


def _final_norm_kernel(h_ref, g_ref, o_ref):
    o_ref[...] = _rms(h_ref[...], g_ref[...])


def _final_norm(h2, g, B, S, D):
    tt = min(S, SEQ_TILE)
    return pl.pallas_call(
        _final_norm_kernel,
        grid=(B, S // tt),
        in_specs=[pl.BlockSpec((tt, D), lambda b, t: (t, b)),
                  pl.BlockSpec((1, D), lambda b, t: (0, 0))],
        out_specs=pl.BlockSpec((None, tt, D), lambda b, t: (b, t, 0)),
        out_shape=jax.ShapeDtypeStruct((B, S, D), F32),
        compiler_params=_cparams(("parallel", "parallel")),
        name="final_norm",
    )(h2, g.reshape(1, D))


def _proj_kernel(h_ref, g_ref, w_ret, w_s5, w_gla, w_pool, o_ret, o_s5, o_gla, o_pool):
    hn = _rms(h_ref[...], g_ref[...]).astype(BF16)
    o_ret[...] = _dot(hn, w_ret[...])
    o_s5[...] = _dot(hn, w_s5[...])
    o_gla[...] = _dot(hn, w_gla[...])
    o_pool[...] = _dot(hn, w_pool[...])


def _proj(h, g, ws):
    R, D = h.shape
    tm = min(R, ROW_TILE)
    widths = [w.shape[1] for w in ws]
    return pl.pallas_call(
        _proj_kernel,
        grid=(R // tm,),
        in_specs=[pl.BlockSpec((tm, D), lambda i: (i, 0)), _const_spec((1, D))]
        + [_const_spec(w.shape) for w in ws],
        out_specs=[pl.BlockSpec((tm, n), lambda i: (i, 0)) for n in widths],
        out_shape=[jax.ShapeDtypeStruct((R, n), F32) for n in widths],
        compiler_params=_cparams(("parallel",)),
        name="mixer_proj",
    )(h, g.reshape(1, D), *ws)


def _ret_consts(heads, dk, dv):
    log_g = np.log(1.0 - 2.0 ** (-5.0 - np.arange(heads, dtype=np.float64)))
    idx = np.arange(CHUNK, dtype=np.float64)
    rel = idx[:, None] - idx[None, :]
    dmask = np.where(rel >= 0, np.exp(np.maximum(rel, 0.0)[None] * log_g[:, None, None]), 0.0)
    lane_g = np.repeat(log_g, dk)[None, :]
    qdec = np.exp((idx[:, None] + 1.0) * lane_g)
    kdec = np.exp((CHUNK - 1.0 - idx)[:, None] * lane_g)
    sdec = np.exp(CHUNK * np.repeat(log_g, dk))[:, None] * np.ones((1, heads * dv))
    bmask = np.kron(np.eye(heads), np.ones((dk, dv)))
    f = lambda a: jnp.asarray(a, F32)
    return f(dmask), f(qdec), f(kdec), f(sdec), f(bmask)


def _ret_kernel(x_ref, ang_ref, dmask_ref, qdec_ref, kdec_ref, sdec_ref, bmask_ref,
                o_ref, state_ref, *, heads, dk, dv, n_chunks):
    hk = heads * dk
    hv = heads * dv

    @pl.when(pl.program_id(1) == 0)
    def _():
        state_ref[...] = jnp.zeros_like(state_ref)

    lane = lax.broadcasted_iota(jnp.int32, (CHUNK, hk), 1)
    first_half = (lane % dk) < (dk // 2)
    head_of_lane = lane // dk

    def rotary(x, cos2, ssin):
        swapped = jnp.where(first_half, pltpu.roll(x, hk - dk // 2, 1), pltpu.roll(x, dk // 2, 1))
        return x * cos2 + swapped * ssin

    def chunk(ci, carry):
        r0 = pl.multiple_of(ci * CHUNK, CHUNK)
        rows = pl.ds(r0, CHUNK)
        ang = ang_ref[rows, :]
        reps = hk // LANE
        cos2 = jnp.concatenate([jnp.cos(ang)] * reps, axis=1)
        sin2 = jnp.concatenate([jnp.sin(ang)] * reps, axis=1)
        ssin = jnp.where(first_half, -sin2, sin2)
        q = rotary(x_ref[rows, 0:hk], cos2, ssin)
        k = rotary(x_ref[rows, hk:2 * hk], cos2, ssin) * (dk ** -0.5)
        v = x_ref[rows, 2 * hk:2 * hk + hv]
        gate = x_ref[rows, 2 * hk + hv:2 * hk + 2 * hv]
        vb = v.astype(BF16)
        kb = k.astype(BF16)
        state = state_ref[...]
        cross = _dot((q * qdec_ref[...]).astype(BF16), state.astype(BF16))
        outs = []
        for h in range(heads):
            qh = jnp.where(head_of_lane == h, q, 0.0).astype(BF16)
            s = _dot_nt(qh, kb) * dmask_ref[h]
            outs.append(_dot(s.astype(BF16), vb[:, h * dv:(h + 1) * dv]))
        o = jnp.concatenate(outs, axis=1) + cross
        kd = (k * kdec_ref[...]).astype(BF16)
        upd = _dot(kd.T, vb)
        state_ref[...] = sdec_ref[...] * state + bmask_ref[...] * upd
        y = jnp.concatenate(
            [_head_norm(o[:, h * dv:(h + 1) * dv]) for h in range(heads)], axis=1)
        o_ref[rows, :] = (y * _silu(gate)).astype(o_ref.dtype)
        return carry

    lax.fori_loop(0, n_chunks, chunk, 0)


def _retention(x2, ang2, B, S, heads, dk, dv):
    tt = min(S, SEQ_TILE)
    win = 2 * heads * dk + 2 * heads * dv
    consts = _ret_consts(heads, dk, dv)
    kern = functools.partial(_ret_kernel, heads=heads, dk=dk, dv=dv, n_chunks=tt // CHUNK)
    return pl.pallas_call(
        kern,
        grid=(B, S // tt),
        in_specs=[pl.BlockSpec((tt, win), lambda b, t: (t, b)),
                  pl.BlockSpec((tt, LANE), lambda b, t: (t, b))]
        + [_const_spec(c.shape) for c in consts],
        out_specs=pl.BlockSpec((tt, heads * dv), lambda b, t: (t, b)),
        out_shape=jax.ShapeDtypeStruct((S, B * heads * dv), BF16),
        scratch_shapes=[pltpu.VMEM((heads * dk, heads * dv), F32)],
        compiler_params=_cparams(("parallel", "arbitrary")),
        name="retention",
    )(x2, ang2, *consts)


def _gla_consts(heads, dk, dv):
    C = CHUNK
    n_lvl = int(math.log2(C))
    t = np.arange(C)
    sums = []
    masks = []
    for l in range(n_lvl):
        bit = (t >> l) & 1
        start = (t >> l) << l
        r = t[None, :]
        upper = (bit[:, None] == 1) & (r >= start[:, None]) & (r <= t[:, None])
        lower = (bit[:, None] == 0) & (r > t[:, None]) & (r < (start + (1 << l))[:, None])
        sums.append((upper | lower).astype(np.float64))
        m = ((bit[:, None] == 1) & (bit[None, :] == 0)
             & ((t[:, None] >> (l + 1)) == (t[None, :] >> (l + 1))))
        masks.append(np.tile(m.astype(np.float64), (heads, 1)))
    masks.append(np.tile(np.eye(C), (heads, 1)))
    sums.append((t[None, :] <= t[:, None]).astype(np.float64))
    sums.append((t[None, :] > t[:, None]).astype(np.float64))
    g = np.concatenate(sums, axis=0)
    gcat = np.concatenate([g, g], axis=1)
    bmask = np.kron(np.eye(heads), np.ones((dk, dv)))
    return jnp.asarray(gcat, BF16), jnp.asarray(np.stack(masks), F32), jnp.asarray(bmask, F32)


def _log_sigmoid(x):
    return jnp.minimum(x, 0.0) - jnp.log(1.0 + jnp.exp(-jnp.abs(x)))


def _gla_kernel(x_ref, wg_ref, bg_ref, gcat_ref, masks_ref, bmask_ref, o_ref, state_ref,
                *, heads, dk, dv, n_chunks):
    hk = heads * dk
    hv = heads * dv
    C = CHUNK
    n_lvl = int(math.log2(C))

    @pl.when(pl.program_id(1) == 0)
    def _():
        state_ref[...] = jnp.zeros_like(state_ref)

    head_of_lane = lax.broadcasted_iota(jnp.int32, (C, hk), 1) // dk

    def stack_heads(x):
        return jnp.concatenate(
            [jnp.where(head_of_lane == h, x, 0.0) for h in range(heads)], axis=0).astype(BF16)

    def chunk(ci, carry):
        r0 = pl.multiple_of(ci * C, C)
        rows = pl.ds(r0, C)
        q = x_ref[rows, 0:hk] * (dk ** -0.5)
        k = x_ref[rows, hk:2 * hk]
        v = x_ref[rows, 2 * hk:2 * hk + hv]
        gate = x_ref[rows, 2 * hk + hv:2 * hk + 2 * hv]
        code = x_ref[rows, 2 * hk + 2 * hv:2 * hk + 2 * hv + LANE]
        vb = v.astype(BF16)
        log_a = _log_sigmoid(_dot(code.astype(BF16), wg_ref[...]) + bg_ref[...]) * (1.0 / GLA_GATE_TEMP)
        hi = log_a.astype(BF16)
        lo = (log_a - hi.astype(F32)).astype(BF16)
        sums = _dot(gcat_ref[...], jnp.concatenate([hi, lo], axis=0))

        scores = masks_ref[n_lvl] * _dot_nt(stack_heads(q), k.astype(BF16))
        for l in range(n_lvl):
            z = jnp.exp(sums[l * C:(l + 1) * C])
            scores = scores + masks_ref[l] * _dot_nt(stack_heads(q * z), (k * z).astype(BF16))
        sb = scores.astype(BF16)
        intra = jnp.concatenate(
            [_dot(sb[h * C:(h + 1) * C], vb[:, h * dv:(h + 1) * dv]) for h in range(heads)], axis=1)

        state = state_ref[...]
        e_pre = sums[n_lvl * C:(n_lvl + 1) * C]
        e_suf = sums[(n_lvl + 1) * C:(n_lvl + 2) * C]
        cross = _dot((q * jnp.exp(e_pre)).astype(BF16), state.astype(BF16))
        kd = (k * jnp.exp(e_suf)).astype(BF16)
        upd = _dot(kd.T, vb)
        total = jnp.broadcast_to(jnp.exp(e_pre[C - 1:C, :]), (LANE, hk))
        dec = jnp.concatenate([total.T] * (hv // LANE), axis=1)
        state_ref[...] = dec * state + bmask_ref[...] * upd

        o = intra + cross
        y = jnp.concatenate(
            [_head_norm(o[:, h * dv:(h + 1) * dv]) for h in range(heads)], axis=1)
        o_ref[rows, :] = (y * _silu(gate)).astype(o_ref.dtype)
        return carry

    lax.fori_loop(0, n_chunks, chunk, 0)


def _gla(x2, wg, bg, B, S, heads, dk, dv):
    tt = min(S, SEQ_TILE)
    win = 2 * heads * dk + 2 * heads * dv + LANE
    consts = _gla_consts(heads, dk, dv)
    kern = functools.partial(_gla_kernel, heads=heads, dk=dk, dv=dv, n_chunks=tt // CHUNK)
    return pl.pallas_call(
        kern,
        grid=(B, S // tt),
        in_specs=[pl.BlockSpec((tt, win), lambda b, t: (t, b)),
                  _const_spec(wg.shape), _const_spec(bg.shape)]
        + [_const_spec(c.shape) for c in consts],
        out_specs=pl.BlockSpec((tt, heads * dv), lambda b, t: (t, b)),
        out_shape=jax.ShapeDtypeStruct((S, B * heads * dv), BF16),
        scratch_shapes=[pltpu.VMEM((heads * dk, heads * dv), F32)],
        compiler_params=_cparams(("parallel", "arbitrary")),
        name="gla",
    )(x2, wg, bg, *consts)


def _s5_kernel(u_ref, a_ref, bmat_ref, cmat_ref, d_ref, wglu_ref, o_ref, xs_ref, st_ref, *, tt):
    nb, w = u_ref.shape[1], u_ref.shape[2]
    ns = a_ref.shape[1] // 2

    @pl.when(pl.program_id(0) == 0)
    def _():
        st_ref[...] = jnp.zeros_like(st_ref)

    u = u_ref[...].reshape(tt * nb, w)
    xs_ref[...] = _dot(u.astype(BF16), bmat_ref[...])

    def step(t, carry):
        xr, xi = carry
        rows = pl.ds(pl.multiple_of(t * nb, nb), nb)
        ar = a_ref[:, 0:ns]
        ai = a_ref[:, ns:2 * ns]
        nr = ar * xr - ai * xi + xs_ref[rows, 0:ns]
        ni = ar * xi + ai * xr + xs_ref[rows, ns:2 * ns]
        xs_ref[rows, 0:ns] = nr
        xs_ref[rows, ns:2 * ns] = ni
        return nr, ni

    xr, xi = lax.fori_loop(0, tt, step, (st_ref[:, 0:ns], st_ref[:, ns:2 * ns]), unroll=2)
    st_ref[:, 0:ns] = xr
    st_ref[:, ns:2 * ns] = xi

    y = _dot(xs_ref[...].astype(BF16), cmat_ref[...]) + d_ref[...] * u
    y = jax.nn.gelu(y)
    y = y * jax.nn.sigmoid(_dot(y.astype(BF16), wglu_ref[...]))
    o_ref[...] = y.reshape(tt, nb, w).astype(o_ref.dtype)


def _s5(u3, a_row, bmat, cmat, d_row, wglu):
    S, B, W = u3.shape
    tt = min(S, S5_TILE)
    ns2 = a_row.shape[1]
    a8 = jnp.broadcast_to(a_row, (B, ns2))
    return pl.pallas_call(
        functools.partial(_s5_kernel, tt=tt),
        grid=(S // tt,),
        in_specs=[pl.BlockSpec((tt, B, W), lambda t: (t, 0, 0)),
                  _const_spec(a8.shape), _const_spec(bmat.shape), _const_spec(cmat.shape),
                  _const_spec(d_row.shape), _const_spec(wglu.shape)],
        out_specs=pl.BlockSpec((tt, B, W), lambda t: (t, 0, 0)),
        out_shape=jax.ShapeDtypeStruct((S, B, W), BF16),
        scratch_shapes=[pltpu.VMEM((tt * B, ns2), F32), pltpu.VMEM((B, ns2), F32)],
        compiler_params=_cparams(("arbitrary",)),
        name="s5",
    )(u3, a8, bmat, cmat, d_row, wglu)


def _s5_params(a_re, a_im, log_dt, b_re, b_im, c_re, c_im):
    G, P = a_re.shape
    dt = jnp.exp(log_dt)[:, None]
    mag = jnp.exp(a_re * dt)
    abar_re = mag * jnp.cos(a_im * dt)
    abar_im = mag * jnp.sin(a_im * dt)
    den = a_re * a_re + a_im * a_im
    nr, ni = abar_re - 1.0, abar_im
    f_re = (nr * a_re + ni * a_im) / den
    f_im = (ni * a_re - nr * a_im) / den
    bb_re = f_re[..., None] * b_re - f_im[..., None] * b_im
    bb_im = f_re[..., None] * b_im + f_im[..., None] * b_re
    eye = jnp.eye(G, dtype=F32)
    n_c = b_re.shape[-1]
    bd_in = lambda m: jnp.einsum('gpc,gh->gchp', m, eye).reshape(G * n_c, G * P)
    bd_out = lambda m: jnp.einsum('gcp,gh->gphc', m, eye).reshape(G * P, G * n_c)
    a_row = jnp.concatenate([abar_re.reshape(1, -1), abar_im.reshape(1, -1)], axis=1)
    bmat = jnp.concatenate([bd_in(bb_re), bd_in(bb_im)], axis=1).astype(BF16)
    cmat = jnp.concatenate([bd_out(c_re), -bd_out(c_im)], axis=0).astype(BF16)
    return a_row, bmat, cmat


def _pool_kernel(p_ref, w_ref, sc_ref, o_ref, ext_ref, *, tt):
    pad = POOL_PAD
    t_idx = pl.program_id(1)

    @pl.when(t_idx == 0)
    def _():
        ext_ref[0:pad, :] = jnp.zeros((pad, ext_ref.shape[1]), F32)

    ext_ref[pad:pad + tt, :] = p_ref[...]
    pos1 = (lax.broadcasted_iota(jnp.int32, (tt, LANE), 0) + t_idx * tt + 1).astype(F32)
    outs = []
    for g, win in enumerate(POOL_WINDOWS):
        cols = slice(g * LANE, (g + 1) * LANE)
        acc = ext_ref[pad:pad + tt, cols]
        for j in range(1, win):
            acc = acc + ext_ref[pad - j:pad - j + tt, cols]
        cur = ext_ref[pad:pad + tt, cols]
        mixed = acc / jnp.minimum(pos1, float(win)) - cur
        outs.append(_dot(mixed.astype(BF16), w_ref[g]))
    o_ref[...] = (jnp.concatenate(outs, axis=1) * sc_ref[...]).astype(o_ref.dtype)
    ext_ref[0:pad, :] = ext_ref[tt:tt + pad, :]


def _pool(p2, w, scale, B, S):
    tt = min(S, SEQ_TILE)
    W = w.shape[0] * LANE
    return pl.pallas_call(
        functools.partial(_pool_kernel, tt=tt),
        grid=(B, S // tt),
        in_specs=[pl.BlockSpec((tt, W), lambda b, t: (t, b)),
                  _const_spec(w.shape), _const_spec(scale.shape)],
        out_specs=pl.BlockSpec((tt, W), lambda b, t: (t, b)),
        out_shape=jax.ShapeDtypeStruct((S, B * W), BF16),
        scratch_shapes=[pltpu.VMEM((tt + POOL_PAD, W), F32)],
        compiler_params=_cparams(("parallel", "arbitrary")),
        name="pool",
    )(p2, w, scale)


def _merge_kernel(h_ref, g_ref, ya, yb, yc, yd, wgate_ref, wbr_ref, o_ref, hn_ref, acc_ref):
    n = pl.program_id(1)

    @pl.when(n == 0)
    def _():
        hn_ref[...] = _rms(h_ref[...], g_ref[...]).astype(BF16)

    logits = _dot(hn_ref[...], wgate_ref[...])

    def contrib(y_ref):
        return jax.nn.sigmoid(logits) * _dot(y_ref[...], wbr_ref[...])

    @pl.when(n == 0)
    def _():
        acc_ref[...] = contrib(ya)

    for idx, y_ref in ((1, yb), (2, yc), (3, yd)):
        @pl.when(n == idx)
        def _(y_ref=y_ref):
            acc_ref[...] += contrib(y_ref)

    @pl.when(n == N_BRANCH - 1)
    def _():
        o_ref[...] = acc_ref[...].astype(o_ref.dtype)


def _merge(h, g, ys, wgate, wbr):
    R, D = h.shape
    tm = min(R, ROW_TILE)
    bw = ys[0].shape[1]
    return pl.pallas_call(
        _merge_kernel,
        grid=(R // tm, N_BRANCH),
        in_specs=[pl.BlockSpec((tm, D), lambda i, n: (i, 0)),
                  pl.BlockSpec((1, D), lambda i, n: (0, 0))]
        + [pl.BlockSpec((tm, bw), lambda i, n: (i, 0)) for _ in ys]
        + [pl.BlockSpec((None, D, D), lambda i, n: (n, 0, 0)),
           pl.BlockSpec((None, bw, D), lambda i, n: (n, 0, 0))],
        out_specs=pl.BlockSpec((tm, D), lambda i, n: (i, 0)),
        out_shape=jax.ShapeDtypeStruct((R, D), BF16),
        scratch_shapes=[pltpu.VMEM((tm, D), BF16), pltpu.VMEM((tm, D), F32)],
        compiler_params=_cparams(("parallel", "arbitrary")),
        name="branch_merge",
    )(h, g.reshape(1, D), *ys, wgate, wbr)


def _outproj_kernel(h_ref, m_ref, w_ref, o_ref):
    o_ref[...] = h_ref[...] + _dot(m_ref[...], w_ref[...])


def _outproj(h, merged, w):
    R, D = h.shape
    tm = min(R, ROW_TILE)
    return pl.pallas_call(
        _outproj_kernel,
        grid=(R // tm,),
        in_specs=[pl.BlockSpec((tm, D), lambda i: (i, 0)),
                  pl.BlockSpec((tm, D), lambda i: (i, 0)),
                  _const_spec(w.shape)],
        out_specs=pl.BlockSpec((tm, D), lambda i: (i, 0)),
        out_shape=jax.ShapeDtypeStruct((R, D), F32),
        compiler_params=_cparams(("parallel",)),
        name="out_proj",
    )(h, merged, w)


def _ffn_kernel(h_ref, g_ref, wa_ref, wv_ref, cw_ref, cb_ref, wd_ref, o_ref,
                hn_ref, carry_ref, *, nb, tn):
    i = pl.program_id(0)
    j = pl.program_id(1)
    halo = (CONV_W - 1) * nb

    @pl.when(j == 0)
    def _():
        hn_ref[...] = _rms(h_ref[...], g_ref[...]).astype(BF16)

    @pl.when(i == 0)
    def _():
        carry_ref[j] = jnp.zeros((halo, 2 * tn), F32)

    hn = hn_ref[...]
    prev = carry_ref[j]

    def conv(u, prev_u, col0):
        ext = jnp.concatenate([prev_u, u], axis=0)
        rows = u.shape[0]
        out = cb_ref[:, col0:col0 + tn]
        for tap in range(CONV_W):
            out = out + ext[tap * nb:tap * nb + rows] * cw_ref[tap:tap + 1, col0:col0 + tn]
        return out

    ua = _dot(hn, wa_ref[...])
    uv = _dot(hn, wv_ref[...])
    rows = ua.shape[0]
    a = conv(ua, prev[:, 0:tn], 0)
    v = conv(uv, prev[:, tn:2 * tn], tn)
    carry_ref[j] = jnp.concatenate([ua[rows - halo:], uv[rows - halo:]], axis=1)
    act = (_silu(a) * v).astype(BF16)
    part = _dot(act, wd_ref[...])

    @pl.when(j == 0)
    def _():
        o_ref[...] = h_ref[...] + part

    @pl.when(j > 0)
    def _():
        o_ref[...] += part


def _ffn(h, g, wa, wv, cw, cb, wd, nb):
    R, D = h.shape
    tm = min(R, ROW_TILE)
    ffp = wa.shape[1]
    tn = FF_TILE
    nj = ffp // tn
    return pl.pallas_call(
        functools.partial(_ffn_kernel, nb=nb, tn=tn),
        grid=(R // tm, nj),
        in_specs=[pl.BlockSpec((tm, D), lambda i, j: (i, 0)),
                  pl.BlockSpec((1, D), lambda i, j: (0, 0)),
                  pl.BlockSpec((D, tn), lambda i, j: (0, j)),
                  pl.BlockSpec((D, tn), lambda i, j: (0, j)),
                  pl.BlockSpec((None, CONV_W, 2 * tn), lambda i, j: (j, 0, 0)),
                  pl.BlockSpec((None, 1, 2 * tn), lambda i, j: (j, 0, 0)),
                  pl.BlockSpec((tn, D), lambda i, j: (j, 0))],
        out_specs=pl.BlockSpec((tm, D), lambda i, j: (i, 0)),
        out_shape=jax.ShapeDtypeStruct((R, D), F32),
        scratch_shapes=[pltpu.VMEM((tm, D), BF16),
                        pltpu.VMEM((nj, (CONV_W - 1) * nb, 2 * tn), F32)],
        compiler_params=_cparams(("arbitrary", "arbitrary")),
        name="conv_ffn",
    )(h, g.reshape(1, D), wa, wv, cw, cb, wd)


def _pad_cols(w, n):
    return jnp.pad(w, ((0, 0), (0, n - w.shape[1])))


def _split_w_in(w_in, D):
    bw = D // 4
    hk = bw // 2
    sizes = [hk, hk, bw, bw, bw, hk, hk, bw, bw, GLA_RANK, bw, N_BRANCH * D]
    offs = np.concatenate([[0], np.cumsum(sizes)])
    seg = lambda a, b: w_in[:, offs[a]:offs[b]]
    w_ret = seg(0, 4)
    w_s5 = seg(4, 5)
    w_gla = jnp.concatenate([seg(5, 9), _pad_cols(seg(9, 10), LANE)], axis=1)
    w_pool = seg(10, 11)
    w_gate = seg(11, 12).reshape(D, N_BRANCH, D).transpose(1, 0, 2)
    return [w.astype(BF16) for w in (w_ret, w_s5, w_gla, w_pool)], w_gate.astype(BF16)


def _ffn_weights(w_up, conv_w, conv_b, w_down, tn):
    D, ff2 = w_up.shape
    ff = ff2 // 2
    ffp = -(-ff // tn) * tn
    nj = ffp // tn
    wa = _pad_cols(w_up[:, :ff], ffp).astype(BF16)
    wv = _pad_cols(w_up[:, ff:], ffp).astype(BF16)
    tiles = lambda m: _pad_cols(m, ffp).reshape(m.shape[0], nj, tn)
    cw = jnp.concatenate([tiles(conv_w[:, :ff]), tiles(conv_w[:, ff:])], axis=2).transpose(1, 0, 2)
    cbr = conv_b.reshape(1, ff2)
    cb = jnp.concatenate([tiles(cbr[:, :ff]), tiles(cbr[:, ff:])], axis=2).transpose(1, 0, 2)
    wd = jnp.pad(w_down, ((0, ffp - ff), (0, 0))).astype(BF16)
    return wa, wv, cw, cb, wd


def kernel(x, positions, norm_mix_g, w_in, s5_a_re, s5_a_im, s5_log_dt, s5_b_re, s5_b_im, s5_c_re, s5_c_im, s5_d, s5_w_glu, gla_w_gate, gla_b_gate, pool_w, pool_scale, w_branch, w_out, norm_ffn_g, w_up, conv_w, conv_b, w_down, final_g):
    B, S, D = x.shape
    depth = w_in.shape[0]
    bw = D // 4
    ret_dk, ret_dv = bw // (2 * RET_HEADS), bw // RET_HEADS
    gla_dk, gla_dv = bw // (2 * GLA_HEADS), bw // GLA_HEADS
    assert S % CHUNK == 0 and (S * B) % min(S * B, ROW_TILE) == 0
    assert RET_HEADS * ret_dk * 2 == bw and ret_dk // 2 * 4 == LANE

    inv = ROPE_BASE ** (-jnp.arange(0, ret_dk, 2, dtype=F32) / ret_dk)
    ang = positions.astype(F32)[..., None] * inv
    ang2 = jnp.tile(ang, (1, 1, LANE // (ret_dk // 2))).transpose(1, 0, 2).reshape(S, B * LANE)

    h = _to_time_major(x).reshape(S * B, D)
    for l in range(depth):
        ws, w_gate = _split_w_in(w_in[l], D)
        p_ret, p_s5, p_gla, p_pool = _proj(h, norm_mix_g[l], ws)
        ya = _retention(p_ret.reshape(S, -1), ang2, B, S, RET_HEADS, ret_dk, ret_dv)
        a_row, bmat, cmat = _s5_params(s5_a_re[l], s5_a_im[l], s5_log_dt[l], s5_b_re[l],
                                       s5_b_im[l], s5_c_re[l], s5_c_im[l])
        yb = _s5(p_s5.reshape(S, B, bw), a_row, bmat, cmat, s5_d[l].reshape(1, bw),
                 s5_w_glu[l].astype(BF16))
        wg = jnp.pad(gla_w_gate[l], ((0, LANE - GLA_RANK), (0, 0))).astype(BF16)
        yc = _gla(p_gla.reshape(S, -1), wg, gla_b_gate[l].reshape(1, -1), B, S,
                  GLA_HEADS, gla_dk, gla_dv)
        yd = _pool(p_pool.reshape(S, -1), pool_w[l].astype(BF16), pool_scale[l].reshape(1, bw), B, S)
        ys = [y.reshape(S * B, bw) for y in (ya, yb, yc, yd)]
        merged = _merge(h, norm_mix_g[l], ys, w_gate, w_branch[l].astype(BF16))
        h = _outproj(h, merged, w_out[l].astype(BF16))
        ffw = _ffn_weights(w_up[l], conv_w[l], conv_b[l], w_down[l], FF_TILE)
        h = _ffn(h, norm_ffn_g[l], *ffw, nb=B)
    return _final_norm(h.reshape(S, B * D), final_g, B, S, D)
```

```python
import functools
import math

import numpy as np
import jax
import jax.numpy as jnp
from jax import lax
from jax.experimental import pallas as pl
from jax.experimental.pallas import tpu as pltpu

F32 = jnp.float32
BF16 = jnp.bfloat16
EPS = 1e-6

N_BRANCH = 4
RET_HEADS = 4
GLA_HEADS = 4
GLA_RANK = 16
GLA_GATE_TEMP = 16.0
S5_GROUP = 16
S5_STATE = 64
POOL_WINDOWS = (2, 4, 8, 16)
POOL_PAD = 16
ROPE_BASE = 10000.0
CONV_W = 3

LANE = 128
SUBLANE = 8
CHUNK = 128
VMEM_LIMIT = 56 * 1024 * 1024
ROW_TILE = 512
FF_TILE = 512
FF_SUB = 256
SEQ_TILE = 512
S5_TILE = 64
S5_PITCH = S5_TILE + 4
S5_BLOCK = 4


def _cparams(sem):
    return pltpu.CompilerParams(dimension_semantics=sem, vmem_limit_bytes=VMEM_LIMIT)


def _const_spec(shape):
    nd = len(shape)
    return pl.BlockSpec(shape, lambda *_: (0,) * nd, pipeline_mode=pl.Buffered(1))


def _dot(a, b):
    return jnp.dot(a, b, preferred_element_type=F32)


def _dot_nt(a, b):
    return lax.dot_general(a, b, (((1,), (1,)), ((), ())), preferred_element_type=F32)


def _rms(x, g):
    r = lax.rsqrt(jnp.mean(x * x, axis=-1, keepdims=True) + EPS)
    return x * r * g


def _head_norm(o):
    mu = jnp.mean(o, axis=-1, keepdims=True)
    d = o - mu
    var = jnp.mean(d * d, axis=-1, keepdims=True)
    return d * lax.rsqrt(var + EPS)


def _silu(x):
    return x * jax.nn.sigmoid(x)


def _seq_map(nt):
    return lambda b, t: (b * nt + t, 0)


def _final_norm_kernel(h_ref, g_ref, o_ref):
    o_ref[...] = _rms(h_ref[...], g_ref[...])


def _final_norm(h, g):
    R, D = h.shape
    tm = min(R, ROW_TILE)
    return pl.pallas_call(
        _final_norm_kernel,
        grid=(R // tm,),
        in_specs=[pl.BlockSpec((tm, D), lambda i: (i, 0)), _const_spec((1, D))],
        out_specs=pl.BlockSpec((tm, D), lambda i: (i, 0)),
        out_shape=jax.ShapeDtypeStruct((R, D), F32),
        compiler_params=_cparams(("parallel",)),
        name="final_norm",
    )(h, g.reshape(1, D))


def _proj_kernel(h_ref, g_ref, w_ret, w_s5, w_gla, w_pool, o_ret, o_s5, o_gla, o_pool):
    hn = _rms(h_ref[...], g_ref[...]).astype(BF16)
    o_ret[...] = _dot(hn, w_ret[...])
    o_s5[...] = _dot(hn, w_s5[...])
    o_gla[...] = _dot(hn, w_gla[...])
    o_pool[...] = _dot(hn, w_pool[...])


def _proj(h, g, ws):
    R, D = h.shape
    tm = min(R, ROW_TILE)
    widths = [w.shape[1] for w in ws]
    return pl.pallas_call(
        _proj_kernel,
        grid=(R // tm,),
        in_specs=[pl.BlockSpec((tm, D), lambda i: (i, 0)), _const_spec((1, D))]
        + [_const_spec(w.shape) for w in ws],
        out_specs=[pl.BlockSpec((tm, n), lambda i: (i, 0)) for n in widths],
        out_shape=[jax.ShapeDtypeStruct((R, n), F32) for n in widths],
        compiler_params=_cparams(("parallel",)),
        name="mixer_proj",
    )(h, g.reshape(1, D), *ws)


def _ret_consts(heads, dk, dv):
    log_g = np.log(1.0 - 2.0 ** (-5.0 - np.arange(heads, dtype=np.float64)))
    idx = np.arange(CHUNK, dtype=np.float64)
    rel = idx[:, None] - idx[None, :]
    dmask = np.where(rel >= 0, np.exp(np.maximum(rel, 0.0)[None] * log_g[:, None, None]), 0.0)
    lane_g = np.repeat(log_g, dk)[None, :]
    qdec = np.exp((idx[:, None] + 1.0) * lane_g)
    kdec = np.exp((CHUNK - 1.0 - idx)[:, None] * lane_g)
    sdec = np.exp(CHUNK * np.repeat(log_g, dk))[:, None] * np.ones((1, heads * dv))
    bmask = np.kron(np.eye(heads), np.ones((dk, dv)))
    f = lambda a: jnp.asarray(a, F32)
    return f(dmask), f(qdec), f(kdec), f(sdec), f(bmask)


def _ret_kernel(x_ref, ang_ref, dmask_ref, qdec_ref, kdec_ref, sdec_ref, bmask_ref,
                o_ref, state_ref, *, heads, dk, dv, n_chunks):
    hk = heads * dk
    hv = heads * dv

    @pl.when(pl.program_id(1) == 0)
    def _():
        state_ref[...] = jnp.zeros_like(state_ref)

    lane = lax.broadcasted_iota(jnp.int32, (CHUNK, hk), 1)
    first_half = (lane % dk) < (dk // 2)
    head_of_lane = lane // dk

    def rotary(x, cos2, ssin):
        swapped = jnp.where(first_half, pltpu.roll(x, hk - dk // 2, 1), pltpu.roll(x, dk // 2, 1))
        return x * cos2 + swapped * ssin

    def chunk(ci, carry):
        r0 = pl.multiple_of(ci * CHUNK, CHUNK)
        rows = pl.ds(r0, CHUNK)
        ang = ang_ref[rows, :]
        reps = hk // LANE
        cos2 = jnp.concatenate([jnp.cos(ang)] * reps, axis=1)
        sin2 = jnp.concatenate([jnp.sin(ang)] * reps, axis=1)
        ssin = jnp.where(first_half, -sin2, sin2)
        q = rotary(x_ref[rows, 0:hk], cos2, ssin)
        k = rotary(x_ref[rows, hk:2 * hk], cos2, ssin) * (dk ** -0.5)
        v = x_ref[rows, 2 * hk:2 * hk + hv]
        gate = x_ref[rows, 2 * hk + hv:2 * hk + 2 * hv]
        vb = v.astype(BF16)
        kb = k.astype(BF16)
        state = state_ref[...]
        cross = _dot((q * qdec_ref[...]).astype(BF16), state.astype(BF16))
        outs = []
        for h in range(heads):
            qh = jnp.where(head_of_lane == h, q, 0.0).astype(BF16)
            s = _dot_nt(qh, kb) * dmask_ref[h]
            outs.append(_dot(s.astype(BF16), vb[:, h * dv:(h + 1) * dv]))
        o = jnp.concatenate(outs, axis=1) + cross
        kd = (k * kdec_ref[...]).astype(BF16)
        upd = _dot(kd.T, vb)
        state_ref[...] = sdec_ref[...] * state + bmask_ref[...] * upd
        y = jnp.concatenate(
            [_head_norm(o[:, h * dv:(h + 1) * dv]) for h in range(heads)], axis=1)
        o_ref[rows, :] = (y * _silu(gate)).astype(o_ref.dtype)
        return carry

    lax.fori_loop(0, n_chunks, chunk, 0)


def _retention(x, ang, B, S, heads, dk, dv):
    tt = min(S, SEQ_TILE)
    nt = S // tt
    win = 2 * heads * dk + 2 * heads * dv
    consts = _ret_consts(heads, dk, dv)
    kern = functools.partial(_ret_kernel, heads=heads, dk=dk, dv=dv, n_chunks=tt // CHUNK)
    return pl.pallas_call(
        kern,
        grid=(B, nt),
        in_specs=[pl.BlockSpec((tt, win), _seq_map(nt)), pl.BlockSpec((tt, LANE), _seq_map(nt))]
        + [_const_spec(c.shape) for c in consts],
        out_specs=pl.BlockSpec((tt, heads * dv), _seq_map(nt)),
        out_shape=jax.ShapeDtypeStruct((B * S, heads * dv), BF16),
        scratch_shapes=[pltpu.VMEM((heads * dk, heads * dv), F32)],
        compiler_params=_cparams(("parallel", "arbitrary")),
        name="retention",
    )(x, ang, *consts)


def _gla_consts(heads, dk, dv):
    C = CHUNK
    n_lvl = int(math.log2(C))
    t = np.arange(C)
    sums = []
    masks = []
    for l in range(n_lvl):
        bit = (t >> l) & 1
        start = (t >> l) << l
        r = t[None, :]
        upper = (bit[:, None] == 1) & (r >= start[:, None]) & (r <= t[:, None])
        lower = (bit[:, None] == 0) & (r > t[:, None]) & (r < (start + (1 << l))[:, None])
        sums.append((upper | lower).astype(np.float64))
        m = ((bit[:, None] == 1) & (bit[None, :] == 0)
             & ((t[:, None] >> (l + 1)) == (t[None, :] >> (l + 1))))
        masks.append(np.tile(m.astype(np.float64), (heads, 1)))
    masks.append(np.tile(np.eye(C), (heads, 1)))
    sums.append((t[None, :] <= t[:, None]).astype(np.float64))
    sums.append((t[None, :] > t[:, None]).astype(np.float64))
    g = np.concatenate(sums, axis=0)
    gcat = np.concatenate([g, g], axis=1)
    bmask = np.kron(np.eye(heads), np.ones((dk, dv)))
    return jnp.asarray(gcat, BF16), jnp.asarray(np.stack(masks), F32), jnp.asarray(bmask, F32)


def _log_sigmoid(x):
    return jnp.minimum(x, 0.0) - jnp.log(1.0 + jnp.exp(-jnp.abs(x)))


def _gla_kernel(x_ref, wg_ref, bg_ref, gcat_ref, masks_ref, bmask_ref, o_ref, state_ref,
                *, heads, dk, dv, n_chunks):
    hk = heads * dk
    hv = heads * dv
    C = CHUNK
    n_lvl = int(math.log2(C))

    @pl.when(pl.program_id(1) == 0)
    def _():
        state_ref[...] = jnp.zeros_like(state_ref)

    head_of_lane = lax.broadcasted_iota(jnp.int32, (C, hk), 1) // dk

    def stack_heads(x):
        return jnp.concatenate(
            [jnp.where(head_of_lane == h, x, 0.0) for h in range(heads)], axis=0).astype(BF16)

    def chunk(ci, carry):
        r0 = pl.multiple_of(ci * C, C)
        rows = pl.ds(r0, C)
        q = x_ref[rows, 0:hk] * (dk ** -0.5)
        k = x_ref[rows, hk:2 * hk]
        v = x_ref[rows, 2 * hk:2 * hk + hv]
        gate = x_ref[rows, 2 * hk + hv:2 * hk + 2 * hv]
        code = x_ref[rows, 2 * hk + 2 * hv:2 * hk + 2 * hv + LANE]
        vb = v.astype(BF16)
        log_a = _log_sigmoid(_dot(code.astype(BF16), wg_ref[...]) + bg_ref[...]) * (1.0 / GLA_GATE_TEMP)
        hi = log_a.astype(BF16)
        lo = (log_a - hi.astype(F32)).astype(BF16)
        sums = _dot(gcat_ref[...], jnp.concatenate([hi, lo], axis=0))

        scores = masks_ref[n_lvl] * _dot_nt(stack_heads(q), k.astype(BF16))
        for l in range(n_lvl):
            z = jnp.exp(sums[l * C:(l + 1) * C])
            scores = scores + masks_ref[l] * _dot_nt(stack_heads(q * z), (k * z).astype(BF16))
        sb = scores.astype(BF16)
        intra = jnp.concatenate(
            [_dot(sb[h * C:(h + 1) * C], vb[:, h * dv:(h + 1) * dv]) for h in range(heads)], axis=1)

        state = state_ref[...]
        e_pre = sums[n_lvl * C:(n_lvl + 1) * C]
        e_suf = sums[(n_lvl + 1) * C:(n_lvl + 2) * C]
        cross = _dot((q * jnp.exp(e_pre)).astype(BF16), state.astype(BF16))
        kd = (k * jnp.exp(e_suf)).astype(BF16)
        upd = _dot(kd.T, vb)
        total = jnp.broadcast_to(jnp.exp(e_pre[C - 1:C, :]), (LANE, hk))
        dec = jnp.concatenate([total.T] * (hv // LANE), axis=1)
        state_ref[...] = dec * state + bmask_ref[...] * upd

        o = intra + cross
        y = jnp.concatenate(
            [_head_norm(o[:, h * dv:(h + 1) * dv]) for h in range(heads)], axis=1)
        o_ref[rows, :] = (y * _silu(gate)).astype(o_ref.dtype)
        return carry

    lax.fori_loop(0, n_chunks, chunk, 0)


def _gla(x, wg, bg, B, S, heads, dk, dv):
    tt = min(S, SEQ_TILE)
    nt = S // tt
    win = 2 * heads * dk + 2 * heads * dv + LANE
    consts = _gla_consts(heads, dk, dv)
    kern = functools.partial(_gla_kernel, heads=heads, dk=dk, dv=dv, n_chunks=tt // CHUNK)
    return pl.pallas_call(
        kern,
        grid=(B, nt),
        in_specs=[pl.BlockSpec((tt, win), _seq_map(nt)),
                  _const_spec(wg.shape), _const_spec(bg.shape)]
        + [_const_spec(c.shape) for c in consts],
        out_specs=pl.BlockSpec((tt, heads * dv), _seq_map(nt)),
        out_shape=jax.ShapeDtypeStruct((B * S, heads * dv), BF16),
        scratch_shapes=[pltpu.VMEM((heads * dk, heads * dv), F32)],
        compiler_params=_cparams(("parallel", "arbitrary")),
        name="gla",
    )(x, wg, bg, *consts)


def _s5_kernel(u_ref, a_ref, bblk_ref, cblk_ref, d_ref, wglu_ref, o_ref,
               stage_ref, utb_ref, xs_ref, ytb_ref, st_ref, *, tt):
    nb, w = u_ref.shape[0], u_ref.shape[2]
    nblk = w // LANE
    sw = a_ref.shape[1] // (2 * nblk)

    @pl.when(pl.program_id(0) == 0)
    def _():
        st_ref[...] = jnp.zeros_like(st_ref)

    for b in range(nb):
        for m in range(nblk):
            stage_ref[m, b * S5_PITCH:b * S5_PITCH + tt, :] = u_ref[b, :, m * LANE:(m + 1) * LANE]

    def gather(t, carry):
        rows = pl.ds(pl.multiple_of(t * nb, nb), nb)
        for m in range(nblk):
            utb_ref[rows, m * LANE:(m + 1) * LANE] = stage_ref[m, pl.ds(t, nb, stride=S5_PITCH), :]
        return carry

    lax.fori_loop(0, tt, gather, 0)

    u = utb_ref[...]
    ub = u.astype(BF16)
    for m in range(nblk):
        xs_ref[:, 2 * sw * m:2 * sw * (m + 1)] = _dot(ub[:, m * LANE:(m + 1) * LANE], bblk_ref[m])

    def step(t, carry):
        rows = pl.ds(pl.multiple_of(t * nb, nb), nb)
        new = []
        for m in range(nblk):
            xr, xi = carry[2 * m], carry[2 * m + 1]
            re = slice(2 * sw * m, 2 * sw * m + sw)
            im = slice(2 * sw * m + sw, 2 * sw * (m + 1))
            ar, ai = a_ref[:, re], a_ref[:, im]
            nr = ar * xr - ai * xi + xs_ref[rows, re]
            ni = ar * xi + ai * xr + xs_ref[rows, im]
            xs_ref[rows, re] = nr
            xs_ref[rows, im] = ni
            new += [nr, ni]
        return tuple(new)

    init = tuple(st_ref[:, sw * c:sw * (c + 1)] for c in range(2 * nblk))
    last = lax.fori_loop(0, tt, step, init, unroll=2)
    for c in range(2 * nblk):
        st_ref[:, sw * c:sw * (c + 1)] = last[c]

    ys = [_dot(xs_ref[:, 2 * sw * m:2 * sw * (m + 1)].astype(BF16), cblk_ref[m]) for m in range(nblk)]
    y = jnp.concatenate(ys, axis=1) + d_ref[...] * u
    y = jax.nn.gelu(y)
    y = y * jax.nn.sigmoid(_dot(y.astype(BF16), wglu_ref[...]))

    for m in range(nblk):
        ytb_ref[m] = y[:, m * LANE:(m + 1) * LANE]
    for b in range(nb):
        for m in range(nblk):
            o_ref[b, :, m * LANE:(m + 1) * LANE] = (
                ytb_ref[m, pl.ds(b, tt, stride=nb), :].astype(o_ref.dtype))


def _s5(u3, a_row, bblk, cblk, d_row, wglu):
    B, S, W = u3.shape
    tt = min(S, S5_TILE)
    ns2 = a_row.shape[1]
    nblk = W // LANE
    a8 = jnp.broadcast_to(a_row, (B, ns2))
    return pl.pallas_call(
        functools.partial(_s5_kernel, tt=tt),
        grid=(S // tt,),
        in_specs=[pl.BlockSpec((B, tt, W), lambda t: (0, t, 0)),
                  _const_spec(a8.shape), _const_spec(bblk.shape), _const_spec(cblk.shape),
                  _const_spec(d_row.shape), _const_spec(wglu.shape)],
        out_specs=pl.BlockSpec((B, tt, W), lambda t: (0, t, 0)),
        out_shape=jax.ShapeDtypeStruct((B, S, W), BF16),
        scratch_shapes=[pltpu.VMEM((nblk, B * S5_PITCH, LANE), F32),
                        pltpu.VMEM((tt * B, W), F32),
                        pltpu.VMEM((tt * B, ns2), F32),
                        pltpu.VMEM((nblk, tt * B, LANE), F32),
                        pltpu.VMEM((B, ns2), F32)],
        compiler_params=_cparams(("arbitrary",)),
        name="s5",
    )(u3, a8, bblk, cblk, d_row, wglu)


def _s5_params(a_re, a_im, log_dt, b_re, b_im, c_re, c_im):
    G, P = a_re.shape
    n_c = b_re.shape[-1]
    gpb = LANE // n_c
    nblk = G // gpb
    dt = jnp.exp(log_dt)[:, None]
    mag = jnp.exp(a_re * dt)
    abar_re = mag * jnp.cos(a_im * dt)
    abar_im = mag * jnp.sin(a_im * dt)
    den = a_re * a_re + a_im * a_im
    nr, ni = abar_re - 1.0, abar_im
    f_re = (nr * a_re + ni * a_im) / den
    f_im = (ni * a_re - nr * a_im) / den
    bb_re = f_re[..., None] * b_re - f_im[..., None] * b_im
    bb_im = f_re[..., None] * b_im + f_im[..., None] * b_re
    eye = jnp.eye(gpb, dtype=F32)

    def bd_in(m):
        m = m.reshape(nblk, gpb, P, n_c)
        return jnp.einsum('ngpc,gh->ngchp', m, eye).reshape(nblk, gpb * n_c, gpb * P)

    def bd_out(m):
        m = m.reshape(nblk, gpb, n_c, P)
        return jnp.einsum('ngcp,gh->ngphc', m, eye).reshape(nblk, gpb * P, gpb * n_c)

    a_row = jnp.concatenate([abar_re.reshape(nblk, 1, gpb * P), abar_im.reshape(nblk, 1, gpb * P)],
                            axis=2).reshape(1, 2 * G * P)
    bblk = jnp.concatenate([bd_in(bb_re), bd_in(bb_im)], axis=2).astype(BF16)
    cblk = jnp.concatenate([bd_out(c_re), -bd_out(c_im)], axis=1).astype(BF16)
    return a_row, bblk, cblk


def _pool_kernel(p_ref, w_ref, sc_ref, o_ref, ext_ref, *, tt):
    pad = POOL_PAD
    t_idx = pl.program_id(1)

    @pl.when(t_idx == 0)
    def _():
        ext_ref[0:pad, :] = jnp.zeros((pad, ext_ref.shape[1]), F32)

    ext_ref[pad:pad + tt, :] = p_ref[...]
    pos1 = (lax.broadcasted_iota(jnp.int32, (tt, LANE), 0) + t_idx * tt + 1).astype(F32)
    outs = []
    for g, win in enumerate(POOL_WINDOWS):
        cols = slice(g * LANE, (g + 1) * LANE)
        acc = ext_ref[pad:pad + tt, cols]
        for j in range(1, win):
            acc = acc + ext_ref[pad - j:pad - j + tt, cols]
        cur = ext_ref[pad:pad + tt, cols]
        mixed = acc / jnp.minimum(pos1, float(win)) - cur
        outs.append(_dot(mixed.astype(BF16), w_ref[g]))
    o_ref[...] = (jnp.concatenate(outs, axis=1) * sc_ref[...]).astype(o_ref.dtype)
    ext_ref[0:pad, :] = ext_ref[tt:tt + pad, :]


def _pool(p, w, scale, B, S):
    tt = min(S, SEQ_TILE)
    nt = S // tt
    W = w.shape[0] * LANE
    return pl.pallas_call(
        functools.partial(_pool_kernel, tt=tt),
        grid=(B, nt),
        in_specs=[pl.BlockSpec((tt, W), _seq_map(nt)),
                  _const_spec(w.shape), _const_spec(scale.shape)],
        out_specs=pl.BlockSpec((tt, W), _seq_map(nt)),
        out_shape=jax.ShapeDtypeStruct((B * S, W), BF16),
        scratch_shapes=[pltpu.VMEM((tt + POOL_PAD, W), F32)],
        compiler_params=_cparams(("parallel", "arbitrary")),
        name="pool",
    )(p, w, scale)


def _merge_kernel(h_ref, g_ref, ya, yb, yc, yd, wgate_ref, wbr_ref, o_ref, hn_ref, acc_ref):
    n = pl.program_id(1)

    @pl.when(n == 0)
    def _():
        hn_ref[...] = _rms(h_ref[...], g_ref[...]).astype(BF16)

    logits = _dot(hn_ref[...], wgate_ref[...])

    def contrib(y_ref):
        return jax.nn.sigmoid(logits) * _dot(y_ref[...], wbr_ref[...])

    @pl.when(n == 0)
    def _():
        acc_ref[...] = contrib(ya)

    for idx, y_ref in ((1, yb), (2, yc), (3, yd)):
        @pl.when(n == idx)
        def _(y_ref=y_ref):
            acc_ref[...] += contrib(y_ref)

    @pl.when(n == N_BRANCH - 1)
    def _():
        o_ref[...] = acc_ref[...].astype(o_ref.dtype)


def _merge(h, g, ys, wgate, wbr):
    R, D = h.shape
    tm = min(R, ROW_TILE)
    bw = ys[0].shape[1]
    return pl.pallas_call(
        _merge_kernel,
        grid=(R // tm, N_BRANCH),
        in_specs=[pl.BlockSpec((tm, D), lambda i, n: (i, 0)),
                  pl.BlockSpec((1, D), lambda i, n: (0, 0))]
        + [pl.BlockSpec((tm, bw), lambda i, n: (i, 0)) for _ in ys]
        + [pl.BlockSpec((D, D), lambda i, n: (0, n)),
           pl.BlockSpec((None, bw, D), lambda i, n: (n, 0, 0))],
        out_specs=pl.BlockSpec((tm, D), lambda i, n: (i, 0)),
        out_shape=jax.ShapeDtypeStruct((R, D), BF16),
        scratch_shapes=[pltpu.VMEM((tm, D), BF16), pltpu.VMEM((tm, D), F32)],
        compiler_params=_cparams(("parallel", "arbitrary")),
        name="branch_merge",
    )(h, g.reshape(1, D), *ys, wgate, wbr)


def _outproj_kernel(h_ref, m_ref, w_ref, o_ref):
    o_ref[...] = h_ref[...] + _dot(m_ref[...], w_ref[...])


def _outproj(h, merged, w):
    R, D = h.shape
    tm = min(R, ROW_TILE)
    return pl.pallas_call(
        _outproj_kernel,
        grid=(R // tm,),
        in_specs=[pl.BlockSpec((tm, D), lambda i: (i, 0)),
                  pl.BlockSpec((tm, D), lambda i: (i, 0)),
                  _const_spec(w.shape)],
        out_specs=pl.BlockSpec((tm, D), lambda i: (i, 0)),
        out_shape=jax.ShapeDtypeStruct((R, D), F32),
        compiler_params=_cparams(("parallel",)),
        name="out_proj",
    )(h, merged, w)


def _ffn_kernel(h_ref, g_ref, wa_ref, wv_ref, cw_ref, cb_ref, wd_ref, o_ref,
                hn_ref, carry_ref, *, tn, tiles_per_seq):
    i = pl.program_id(0)
    j = pl.program_id(1)
    tm = hn_ref.shape[0]

    @pl.when(j == 0)
    def _():
        hn_ref[...] = _rms(h_ref[...], g_ref[...]).astype(BF16)
        o_ref[...] = h_ref[...]

    @pl.when(i % tiles_per_seq == 0)
    def _():
        carry_ref[j] = jnp.zeros((SUBLANE, 2 * tn), F32)

    row8 = lax.broadcasted_iota(jnp.int32, (SUBLANE, tn), 0)

    def delayed(u, prev8, k):
        r = pltpu.roll(u, k, 0)
        top = jnp.where(row8 < k, pltpu.roll(prev8, k, 0), r[0:SUBLANE])
        return jnp.concatenate([top, r[SUBLANE:]], axis=0)

    def conv(u, prev8, col0):
        out = cb_ref[:, col0:col0 + tn] + u * cw_ref[CONV_W - 1:CONV_W, col0:col0 + tn]
        for tap in range(CONV_W - 1):
            out = out + delayed(u, prev8, CONV_W - 1 - tap) * cw_ref[tap:tap + 1, col0:col0 + tn]
        return out

    prev = carry_ref[j]
    prev_a, prev_v = prev[:, 0:tn], prev[:, tn:2 * tn]
    def up(rb):
        hn = hn_ref[rb * FF_SUB:(rb + 1) * FF_SUB, :]
        return _dot(hn, wa_ref[...]), _dot(hn, wv_ref[...])

    n_sub = tm // FF_SUB
    nxt = up(0)
    for rb in range(n_sub):
        ua, uv = nxt
        if rb + 1 < n_sub:
            nxt = up(rb + 1)
        a = conv(ua, prev_a, 0)
        v = conv(uv, prev_v, tn)
        prev_a, prev_v = ua[FF_SUB - SUBLANE:], uv[FF_SUB - SUBLANE:]
        act = (_silu(a) * v).astype(BF16)
        o_ref[rb * FF_SUB:(rb + 1) * FF_SUB, :] += _dot(act, wd_ref[...])
    carry_ref[j] = jnp.concatenate([prev_a, prev_v], axis=1)


def _ffn(h, g, wa, wv, cw, cb, wd, S):
    R, D = h.shape
    tm = min(S, ROW_TILE)
    ffp = wa.shape[1]
    tn = FF_TILE
    nj = ffp // tn
    return pl.pallas_call(
        functools.partial(_ffn_kernel, tn=tn, tiles_per_seq=S // tm),
        grid=(R // tm, nj),
        in_specs=[pl.BlockSpec((tm, D), lambda i, j: (i, 0)),
                  pl.BlockSpec((1, D), lambda i, j: (0, 0)),
                  pl.BlockSpec((D, tn), lambda i, j: (0, j)),
                  pl.BlockSpec((D, tn), lambda i, j: (0, j)),
                  pl.BlockSpec((None, CONV_W, 2 * tn), lambda i, j: (j, 0, 0)),
                  pl.BlockSpec((None, 1, 2 * tn), lambda i, j: (j, 0, 0)),
                  pl.BlockSpec((tn, D), lambda i, j: (j, 0))],
        out_specs=pl.BlockSpec((tm, D), lambda i, j: (i, 0)),
        out_shape=jax.ShapeDtypeStruct((R, D), F32),
        scratch_shapes=[pltpu.VMEM((tm, D), BF16),
                        pltpu.VMEM((nj, SUBLANE, 2 * tn), F32)],
        compiler_params=_cparams(("arbitrary", "arbitrary")),
        name="conv_ffn",
    )(h, g.reshape(1, D), wa, wv, cw, cb, wd)


def _pad_cols(w, n):
    return jnp.pad(w, ((0, 0), (0, n - w.shape[1])))


def _split_w_in(w_in, D):
    bw = D // 4
    hk = bw // 2
    sizes = [hk, hk, bw, bw, bw, hk, hk, bw, bw, GLA_RANK, bw, N_BRANCH * D]
    offs = np.concatenate([[0], np.cumsum(sizes)])
    seg = lambda a, b: w_in[:, offs[a]:offs[b]]
    w_ret = seg(0, 4)
    w_s5 = seg(4, 5)
    w_gla = jnp.concatenate([seg(5, 9), _pad_cols(seg(9, 10), LANE)], axis=1)
    w_pool = seg(10, 11)
    return [w.astype(BF16) for w in (w_ret, w_s5, w_gla, w_pool)], seg(11, 12).astype(BF16)


def _ffn_weights(w_up, conv_w, conv_b, w_down, tn):
    D, ff2 = w_up.shape
    ff = ff2 // 2
    ffp = -(-ff // tn) * tn
    nj = ffp // tn
    wa = _pad_cols(w_up[:, :ff], ffp).astype(BF16)
    wv = _pad_cols(w_up[:, ff:], ffp).astype(BF16)
    tiles = lambda m: _pad_cols(m, ffp).reshape(m.shape[0], nj, tn)
    cw = jnp.concatenate([tiles(conv_w[:, :ff]), tiles(conv_w[:, ff:])], axis=2).transpose(1, 0, 2)
    cbr = conv_b.reshape(1, ff2)
    cb = jnp.concatenate([tiles(cbr[:, :ff]), tiles(cbr[:, ff:])], axis=2).transpose(1, 0, 2)
    wd = jnp.pad(w_down, ((0, ffp - ff), (0, 0))).astype(BF16)
    return wa, wv, cw, cb, wd


def kernel(x, positions, norm_mix_g, w_in, s5_a_re, s5_a_im, s5_log_dt, s5_b_re, s5_b_im, s5_c_re, s5_c_im, s5_d, s5_w_glu, gla_w_gate, gla_b_gate, pool_w, pool_scale, w_branch, w_out, norm_ffn_g, w_up, conv_w, conv_b, w_down, final_g):
    B, S, D = x.shape
    depth = w_in.shape[0]
    bw = D // 4
    ret_dk, ret_dv = bw // (2 * RET_HEADS), bw // RET_HEADS
    gla_dk, gla_dv = bw // (2 * GLA_HEADS), bw // GLA_HEADS
    assert S % CHUNK == 0 and S % min(S, ROW_TILE) == 0 and B == SUBLANE
    assert ret_dk // 2 * 4 == LANE and bw == S5_BLOCK * LANE

    inv = ROPE_BASE ** (-jnp.arange(0, ret_dk, 2, dtype=F32) / ret_dk)
    ang = positions.astype(F32)[..., None] * inv
    ang = jnp.tile(ang, (1, 1, LANE // (ret_dk // 2))).reshape(B * S, LANE)

    h = x.reshape(B * S, D)
    for l in range(depth):
        ws, w_gate = _split_w_in(w_in[l], D)
        p_ret, p_s5, p_gla, p_pool = _proj(h, norm_mix_g[l], ws)
        ya = _retention(p_ret, ang, B, S, RET_HEADS, ret_dk, ret_dv)
        a_row, bblk, cblk = _s5_params(s5_a_re[l], s5_a_im[l], s5_log_dt[l], s5_b_re[l],
                                       s5_b_im[l], s5_c_re[l], s5_c_im[l])
        yb = _s5(p_s5.reshape(B, S, bw), a_row, bblk, cblk, s5_d[l].reshape(1, bw),
                 s5_w_glu[l].astype(BF16)).reshape(B * S, bw)
        wg = jnp.pad(gla_w_gate[l], ((0, LANE - GLA_RANK), (0, 0))).astype(BF16)
        yc = _gla(p_gla, wg, gla_b_gate[l].reshape(1, -1), B, S, GLA_HEADS, gla_dk, gla_dv)
        yd = _pool(p_pool, pool_w[l].astype(BF16), pool_scale[l].reshape(1, bw), B, S)
        merged = _merge(h, norm_mix_g[l], [ya, yb, yc, yd], w_gate, w_branch[l].astype(BF16))
        h = _outproj(h, merged, w_out[l].astype(BF16))
        ffw = _ffn_weights(w_up[l], conv_w[l], conv_b[l], w_down[l], FF_TILE)
        h = _ffn(h, norm_ffn_g[l], *ffw, S=S)
    return _final_norm(h, final_g).reshape(B, S, D)
```

```python
import functools
import math

import numpy as np
import jax
import jax.numpy as jnp
from jax import lax
from jax.experimental import pallas as pl
from jax.experimental.pallas import tpu as pltpu

F32 = jnp.float32
BF16 = jnp.bfloat16
EPS = 1e-6

N_BRANCH = 4
RET_HEADS = 4
GLA_HEADS = 4
GLA_RANK = 16
GLA_GATE_TEMP = 16.0
S5_GROUP = 16
S5_STATE = 64
POOL_WINDOWS = (2, 4, 8, 16)
POOL_PAD = 16
ROPE_BASE = 10000.0
CONV_W = 3

LANE = 128
SUBLANE = 8
CHUNK = 128
VMEM_LIMIT = 56 * 1024 * 1024
ROW_TILE = 512
MERGE_ROW_TILE = 1024
MERGE_TILE = 512
MERGE_SUB = 256
FF_ROW_TILE = 1024
FF_TILE = 512
FF_SUB = 256
SEQ_TILE = 512
S5_TILE = 64
S5_PITCH = S5_TILE + 4
S5_BLOCK = 4


def _cparams(sem):
    return pltpu.CompilerParams(dimension_semantics=sem, vmem_limit_bytes=VMEM_LIMIT)


def _const_spec(shape):
    nd = len(shape)
    return pl.BlockSpec(shape, lambda *_: (0,) * nd, pipeline_mode=pl.Buffered(1))


def _dot(a, b):
    return jnp.dot(a, b, preferred_element_type=F32)


def _dot_nt(a, b):
    return lax.dot_general(a, b, (((1,), (1,)), ((), ())), preferred_element_type=F32)


def _rms(x, g):
    r = lax.rsqrt(jnp.mean(x * x, axis=-1, keepdims=True) + EPS)
    return x * r * g


def _head_norm(o):
    mu = jnp.mean(o, axis=-1, keepdims=True)
    d = o - mu
    var = jnp.mean(d * d, axis=-1, keepdims=True)
    return d * lax.rsqrt(var + EPS)


def _silu(x):
    return x * jax.nn.sigmoid(x)


def _seq_map(nt):
    return lambda b, t: (b * nt + t, 0)


def _final_norm_kernel(h_ref, g_ref, o_ref):
    o_ref[...] = _rms(h_ref[...], g_ref[...])


def _final_norm(h, g):
    R, D = h.shape
    tm = min(R, ROW_TILE)
    return pl.pallas_call(
        _final_norm_kernel,
        grid=(R // tm,),
        in_specs=[pl.BlockSpec((tm, D), lambda i: (i, 0)), _const_spec((1, D))],
        out_specs=pl.BlockSpec((tm, D), lambda i: (i, 0)),
        out_shape=jax.ShapeDtypeStruct((R, D), F32),
        compiler_params=_cparams(("parallel",)),
        name="final_norm",
    )(h, g.reshape(1, D))


def _proj_kernel(h_ref, g_ref, w_ret, w_s5, w_gla, w_pool, o_hn, o_ret, o_s5, o_gla, o_pool):
    hn = _rms(h_ref[...], g_ref[...]).astype(BF16)
    o_hn[...] = hn
    o_ret[...] = _dot(hn, w_ret[...])
    o_s5[...] = _dot(hn, w_s5[...])
    o_gla[...] = _dot(hn, w_gla[...])
    o_pool[...] = _dot(hn, w_pool[...])


def _proj(h, g, ws):
    R, D = h.shape
    tm = min(R, ROW_TILE)
    widths = [w.shape[1] for w in ws]
    return pl.pallas_call(
        _proj_kernel,
        grid=(R // tm,),
        in_specs=[pl.BlockSpec((tm, D), lambda i: (i, 0)), _const_spec((1, D))]
        + [_const_spec(w.shape) for w in ws],
        out_specs=[pl.BlockSpec((tm, n), lambda i: (i, 0)) for n in [D] + widths],
        out_shape=[jax.ShapeDtypeStruct((R, D), BF16)]
        + [jax.ShapeDtypeStruct((R, n), F32) for n in widths],
        compiler_params=_cparams(("parallel",)),
        name="mixer_proj",
    )(h, g.reshape(1, D), *ws)


def _ret_consts(heads, dk, dv):
    log_g = np.log(1.0 - 2.0 ** (-5.0 - np.arange(heads, dtype=np.float64)))
    idx = np.arange(CHUNK, dtype=np.float64)
    rel = idx[:, None] - idx[None, :]
    dmask = np.where(rel >= 0, np.exp(np.maximum(rel, 0.0)[None] * log_g[:, None, None]), 0.0)
    lane_g = np.repeat(log_g, dk)[None, :]
    qdec = np.exp((idx[:, None] + 1.0) * lane_g)
    kdec = np.exp((CHUNK - 1.0 - idx)[:, None] * lane_g)
    sdec = np.exp(CHUNK * np.repeat(log_g, dk))[:, None] * np.ones((1, heads * dv))
    bmask = np.kron(np.eye(heads), np.ones((dk, dv)))
    f = lambda a: jnp.asarray(a, F32)
    return f(dmask), f(qdec), f(kdec), f(sdec), f(bmask)


def _ret_kernel(x_ref, ang_ref, dmask_ref, qdec_ref, kdec_ref, sdec_ref, bmask_ref,
                o_ref, state_ref, *, heads, dk, dv, n_chunks):
    hk = heads * dk
    hv = heads * dv

    @pl.when(pl.program_id(1) == 0)
    def _():
        state_ref[...] = jnp.zeros_like(state_ref)

    lane = lax.broadcasted_iota(jnp.int32, (CHUNK, hk), 1)
    first_half = (lane % dk) < (dk // 2)
    head_of_lane = lane // dk

    def rotary(x, cos2, ssin):
        swapped = jnp.where(first_half, pltpu.roll(x, hk - dk // 2, 1), pltpu.roll(x, dk // 2, 1))
        return x * cos2 + swapped * ssin

    def chunk(ci, carry):
        r0 = pl.multiple_of(ci * CHUNK, CHUNK)
        rows = pl.ds(r0, CHUNK)
        ang = ang_ref[rows, :]
        reps = hk // LANE
        cos2 = jnp.concatenate([jnp.cos(ang)] * reps, axis=1)
        sin2 = jnp.concatenate([jnp.sin(ang)] * reps, axis=1)
        ssin = jnp.where(first_half, -sin2, sin2)
        q = rotary(x_ref[rows, 0:hk], cos2, ssin)
        k = rotary(x_ref[rows, hk:2 * hk], cos2, ssin) * (dk ** -0.5)
        v = x_ref[rows, 2 * hk:2 * hk + hv]
        gate = x_ref[rows, 2 * hk + hv:2 * hk + 2 * hv]
        vb = v.astype(BF16)
        kb = k.astype(BF16)
        state = state_ref[...]
        cross = _dot((q * qdec_ref[...]).astype(BF16), state.astype(BF16))
        outs = []
        for h in range(heads):
            qh = jnp.where(head_of_lane == h, q, 0.0).astype(BF16)
            s = _dot_nt(qh, kb) * dmask_ref[h]
            outs.append(_dot(s.astype(BF16), vb[:, h * dv:(h + 1) * dv]))
        o = jnp.concatenate(outs, axis=1) + cross
        kd = (k * kdec_ref[...]).astype(BF16)
        upd = _dot(kd.T, vb)
        state_ref[...] = sdec_ref[...] * state + bmask_ref[...] * upd
        y = jnp.concatenate(
            [_head_norm(o[:, h * dv:(h + 1) * dv]) for h in range(heads)], axis=1)
        o_ref[rows, :] = (y * _silu(gate)).astype(o_ref.dtype)
        return carry

    lax.fori_loop(0, n_chunks, chunk, 0)


def _retention(x, ang, B, S, heads, dk, dv):
    tt = min(S, SEQ_TILE)
    nt = S // tt
    win = 2 * heads * dk + 2 * heads * dv
    consts = _ret_consts(heads, dk, dv)
    kern = functools.partial(_ret_kernel, heads=heads, dk=dk, dv=dv, n_chunks=tt // CHUNK)
    return pl.pallas_call(
        kern,
        grid=(B, nt),
        in_specs=[pl.BlockSpec((tt, win), _seq_map(nt)), pl.BlockSpec((tt, LANE), _seq_map(nt))]
        + [_const_spec(c.shape) for c in consts],
        out_specs=pl.BlockSpec((tt, heads * dv), _seq_map(nt)),
        out_shape=jax.ShapeDtypeStruct((B * S, heads * dv), BF16),
        scratch_shapes=[pltpu.VMEM((heads * dk, heads * dv), F32)],
        compiler_params=_cparams(("parallel", "arbitrary")),
        name="retention",
    )(x, ang, *consts)


def _gla_consts(heads, dk, dv):
    C = CHUNK
    n_lvl = int(math.log2(C))
    t = np.arange(C)
    sums = []
    masks = []
    for l in range(n_lvl):
        bit = (t >> l) & 1
        start = (t >> l) << l
        r = t[None, :]
        upper = (bit[:, None] == 1) & (r >= start[:, None]) & (r <= t[:, None])
        lower = (bit[:, None] == 0) & (r > t[:, None]) & (r < (start + (1 << l))[:, None])
        sums.append((upper | lower).astype(np.float64))
        m = ((bit[:, None] == 1) & (bit[None, :] == 0)
             & ((t[:, None] >> (l + 1)) == (t[None, :] >> (l + 1))))
        masks.append(np.tile(m.astype(np.float64), (heads, 1)))
    masks.append(np.tile(np.eye(C), (heads, 1)))
    sums.append((t[None, :] <= t[:, None]).astype(np.float64))
    sums.append((t[None, :] > t[:, None]).astype(np.float64))
    g = np.concatenate(sums, axis=0)
    gcat = np.concatenate([g, g], axis=1)
    bmask = np.kron(np.eye(heads), np.ones((dk, dv)))
    return jnp.asarray(gcat, BF16), jnp.asarray(np.stack(masks), F32), jnp.asarray(bmask, F32)


def _log_sigmoid(x):
    return jnp.minimum(x, 0.0) - jnp.log(1.0 + jnp.exp(-jnp.abs(x)))


def _gla_kernel(x_ref, wg_ref, bg_ref, gcat_ref, masks_ref, bmask_ref, o_ref, state_ref,
                *, heads, dk, dv, n_chunks):
    hk = heads * dk
    hv = heads * dv
    C = CHUNK
    n_lvl = int(math.log2(C))

    @pl.when(pl.program_id(1) == 0)
    def _():
        state_ref[...] = jnp.zeros_like(state_ref)

    head_of_lane = lax.broadcasted_iota(jnp.int32, (C, hk), 1) // dk

    def stack_heads(x):
        return jnp.concatenate(
            [jnp.where(head_of_lane == h, x, 0.0) for h in range(heads)], axis=0).astype(BF16)

    def chunk(ci, carry):
        r0 = pl.multiple_of(ci * C, C)
        rows = pl.ds(r0, C)
        q = x_ref[rows, 0:hk] * (dk ** -0.5)
        k = x_ref[rows, hk:2 * hk]
        v = x_ref[rows, 2 * hk:2 * hk + hv]
        gate = x_ref[rows, 2 * hk + hv:2 * hk + 2 * hv]
        code = x_ref[rows, 2 * hk + 2 * hv:2 * hk + 2 * hv + LANE]
        vb = v.astype(BF16)
        log_a = _log_sigmoid(_dot(code.astype(BF16), wg_ref[...]) + bg_ref[...]) * (1.0 / GLA_GATE_TEMP)
        hi = log_a.astype(BF16)
        lo = (log_a - hi.astype(F32)).astype(BF16)
        sums = _dot(gcat_ref[...], jnp.concatenate([hi, lo], axis=0))

        scores = masks_ref[n_lvl] * _dot_nt(stack_heads(q), k.astype(BF16))
        for l in range(n_lvl):
            z = jnp.exp(sums[l * C:(l + 1) * C])
            scores = scores + masks_ref[l] * _dot_nt(stack_heads(q * z), (k * z).astype(BF16))
        sb = scores.astype(BF16)
        intra = jnp.concatenate(
            [_dot(sb[h * C:(h + 1) * C], vb[:, h * dv:(h + 1) * dv]) for h in range(heads)], axis=1)

        state = state_ref[...]
        e_pre = sums[n_lvl * C:(n_lvl + 1) * C]
        e_suf = sums[(n_lvl + 1) * C:(n_lvl + 2) * C]
        cross = _dot((q * jnp.exp(e_pre)).astype(BF16), state.astype(BF16))
        kd = (k * jnp.exp(e_suf)).astype(BF16)
        upd = _dot(kd.T, vb)
        total = jnp.broadcast_to(jnp.exp(e_pre[C - 1:C, :]), (LANE, hk))
        dec = jnp.concatenate([total.T] * (hv // LANE), axis=1)
        state_ref[...] = dec * state + bmask_ref[...] * upd

        o = intra + cross
        y = jnp.concatenate(
            [_head_norm(o[:, h * dv:(h + 1) * dv]) for h in range(heads)], axis=1)
        o_ref[rows, :] = (y * _silu(gate)).astype(o_ref.dtype)
        return carry

    lax.fori_loop(0, n_chunks, chunk, 0)


def _gla(x, wg, bg, B, S, heads, dk, dv):
    tt = min(S, SEQ_TILE)
    nt = S // tt
    win = 2 * heads * dk + 2 * heads * dv + LANE
    consts = _gla_consts(heads, dk, dv)
    kern = functools.partial(_gla_kernel, heads=heads, dk=dk, dv=dv, n_chunks=tt // CHUNK)
    return pl.pallas_call(
        kern,
        grid=(B, nt),
        in_specs=[pl.BlockSpec((tt, win), _seq_map(nt)),
                  _const_spec(wg.shape), _const_spec(bg.shape)]
        + [_const_spec(c.shape) for c in consts],
        out_specs=pl.BlockSpec((tt, heads * dv), _seq_map(nt)),
        out_shape=jax.ShapeDtypeStruct((B * S, heads * dv), BF16),
        scratch_shapes=[pltpu.VMEM((heads * dk, heads * dv), F32)],
        compiler_params=_cparams(("parallel", "arbitrary")),
        name="gla",
    )(x, wg, bg, *consts)


def _s5_kernel(u_ref, a_ref, bblk_ref, cblk_ref, d_ref, wglu_ref, o_ref,
               stage_ref, utb_ref, xs_ref, ytb_ref, st_ref, *, tt):
    nb, w = u_ref.shape[0], u_ref.shape[2]
    nblk = w // LANE
    sw = a_ref.shape[1] // (2 * nblk)

    @pl.when(pl.program_id(0) == 0)
    def _():
        st_ref[...] = jnp.zeros_like(st_ref)

    for b in range(nb):
        for m in range(nblk):
            stage_ref[m, b * S5_PITCH:b * S5_PITCH + tt, :] = u_ref[b, :, m * LANE:(m + 1) * LANE]

    def gather(t, carry):
        rows = pl.ds(pl.multiple_of(t * nb, nb), nb)
        for m in range(nblk):
            utb_ref[rows, m * LANE:(m + 1) * LANE] = stage_ref[m, pl.ds(t, nb, stride=S5_PITCH), :]
        return carry

    lax.fori_loop(0, tt, gather, 0)

    u = utb_ref[...]
    ub = u.astype(BF16)
    for m in range(nblk):
        xs_ref[:, 2 * sw * m:2 * sw * (m + 1)] = _dot(ub[:, m * LANE:(m + 1) * LANE], bblk_ref[m])

    def step(t, carry):
        rows = pl.ds(pl.multiple_of(t * nb, nb), nb)
        new = []
        for m in range(nblk):
            xr, xi = carry[2 * m], carry[2 * m + 1]
            re = slice(2 * sw * m, 2 * sw * m + sw)
            im = slice(2 * sw * m + sw, 2 * sw * (m + 1))
            ar, ai = a_ref[:, re], a_ref[:, im]
            nr = ar * xr - ai * xi + xs_ref[rows, re]
            ni = ar * xi + ai * xr + xs_ref[rows, im]
            xs_ref[rows, re] = nr
            xs_ref[rows, im] = ni
            new += [nr, ni]
        return tuple(new)

    init = tuple(st_ref[:, sw * c:sw * (c + 1)] for c in range(2 * nblk))
    last = lax.fori_loop(0, tt, step, init, unroll=2)
    for c in range(2 * nblk):
        st_ref[:, sw * c:sw * (c + 1)] = last[c]

    ys = [_dot(xs_ref[:, 2 * sw * m:2 * sw * (m + 1)].astype(BF16), cblk_ref[m]) for m in range(nblk)]
    y = jnp.concatenate(ys, axis=1) + d_ref[...] * u
    y = jax.nn.gelu(y)
    y = y * jax.nn.sigmoid(_dot(y.astype(BF16), wglu_ref[...]))

    for m in range(nblk):
        ytb_ref[m] = y[:, m * LANE:(m + 1) * LANE]
    for b in range(nb):
        for m in range(nblk):
            o_ref[b, :, m * LANE:(m + 1) * LANE] = (
                ytb_ref[m, pl.ds(b, tt, stride=nb), :].astype(o_ref.dtype))


def _s5(u3, a_row, bblk, cblk, d_row, wglu):
    B, S, W = u3.shape
    tt = min(S, S5_TILE)
    ns2 = a_row.shape[1]
    nblk = W // LANE
    a8 = jnp.broadcast_to(a_row, (B, ns2))
    return pl.pallas_call(
        functools.partial(_s5_kernel, tt=tt),
        grid=(S // tt,),
        in_specs=[pl.BlockSpec((B, tt, W), lambda t: (0, t, 0)),
                  _const_spec(a8.shape), _const_spec(bblk.shape), _const_spec(cblk.shape),
                  _const_spec(d_row.shape), _const_spec(wglu.shape)],
        out_specs=pl.BlockSpec((B, tt, W), lambda t: (0, t, 0)),
        out_shape=jax.ShapeDtypeStruct((B, S, W), BF16),
        scratch_shapes=[pltpu.VMEM((nblk, B * S5_PITCH, LANE), F32),
                        pltpu.VMEM((tt * B, W), F32),
                        pltpu.VMEM((tt * B, ns2), F32),
                        pltpu.VMEM((nblk, tt * B, LANE), F32),
                        pltpu.VMEM((B, ns2), F32)],
        compiler_params=_cparams(("arbitrary",)),
        name="s5",
    )(u3, a8, bblk, cblk, d_row, wglu)


def _s5_params(a_re, a_im, log_dt, b_re, b_im, c_re, c_im):
    G, P = a_re.shape
    n_c = b_re.shape[-1]
    gpb = LANE // n_c
    nblk = G // gpb
    dt = jnp.exp(log_dt)[:, None]
    mag = jnp.exp(a_re * dt)
    abar_re = mag * jnp.cos(a_im * dt)
    abar_im = mag * jnp.sin(a_im * dt)
    den = a_re * a_re + a_im * a_im
    nr, ni = abar_re - 1.0, abar_im
    f_re = (nr * a_re + ni * a_im) / den
    f_im = (ni * a_re - nr * a_im) / den
    bb_re = f_re[..., None] * b_re - f_im[..., None] * b_im
    bb_im = f_re[..., None] * b_im + f_im[..., None] * b_re
    eye = jnp.eye(gpb, dtype=F32)

    def bd_in(m):
        m = m.reshape(nblk, gpb, P, n_c)
        return jnp.einsum('ngpc,gh->ngchp', m, eye).reshape(nblk, gpb * n_c, gpb * P)

    def bd_out(m):
        m = m.reshape(nblk, gpb, n_c, P)
        return jnp.einsum('ngcp,gh->ngphc', m, eye).reshape(nblk, gpb * P, gpb * n_c)

    a_row = jnp.concatenate([abar_re.reshape(nblk, 1, gpb * P), abar_im.reshape(nblk, 1, gpb * P)],
                            axis=2).reshape(1, 2 * G * P)
    bblk = jnp.concatenate([bd_in(bb_re), bd_in(bb_im)], axis=2).astype(BF16)
    cblk = jnp.concatenate([bd_out(c_re), -bd_out(c_im)], axis=1).astype(BF16)
    return a_row, bblk, cblk


def _pool_kernel(p_ref, w_ref, sc_ref, o_ref, ext_ref, *, tt):
    pad = POOL_PAD
    t_idx = pl.program_id(1)

    @pl.when(t_idx == 0)
    def _():
        ext_ref[0:pad, :] = jnp.zeros((pad, ext_ref.shape[1]), F32)

    ext_ref[pad:pad + tt, :] = p_ref[...]
    pos1 = (lax.broadcasted_iota(jnp.int32, (tt, LANE), 0) + t_idx * tt + 1).astype(F32)
    outs = []
    for g, win in enumerate(POOL_WINDOWS):
        cols = slice(g * LANE, (g + 1) * LANE)
        acc = ext_ref[pad:pad + tt, cols]
        for j in range(1, win):
            acc = acc + ext_ref[pad - j:pad - j + tt, cols]
        cur = ext_ref[pad:pad + tt, cols]
        mixed = acc / jnp.minimum(pos1, float(win)) - cur
        outs.append(_dot(mixed.astype(BF16), w_ref[g]))
    o_ref[...] = (jnp.concatenate(outs, axis=1) * sc_ref[...]).astype(o_ref.dtype)
    ext_ref[0:pad, :] = ext_ref[tt:tt + pad, :]


def _pool(p, w, scale, B, S):
    tt = min(S, SEQ_TILE)
    nt = S // tt
    W = w.shape[0] * LANE
    return pl.pallas_call(
        functools.partial(_pool_kernel, tt=tt),
        grid=(B, nt),
        in_specs=[pl.BlockSpec((tt, W), _seq_map(nt)),
                  _const_spec(w.shape), _const_spec(scale.shape)],
        out_specs=pl.BlockSpec((tt, W), _seq_map(nt)),
        out_shape=jax.ShapeDtypeStruct((B * S, W), BF16),
        scratch_shapes=[pltpu.VMEM((tt + POOL_PAD, W), F32)],
        compiler_params=_cparams(("parallel", "arbitrary")),
        name="pool",
    )(p, w, scale)


def _merge_kernel(hn_ref, ya, yb, yc, yd, wgate_ref, wbr_ref, o_ref):
    tm, tn = o_ref.shape
    ys = (ya, yb, yc, yd)
    for rb in range(tm // MERGE_SUB):
        rows = slice(rb * MERGE_SUB, (rb + 1) * MERGE_SUB)
        hn = hn_ref[rows, :]
        acc = None
        for n in range(N_BRANCH):
            logits = _dot(hn, wgate_ref[:, n * tn:(n + 1) * tn])
            term = jax.nn.sigmoid(logits) * _dot(ys[n][rows, :], wbr_ref[n])
            acc = term if acc is None else acc + term
        o_ref[rows, :] = acc.astype(o_ref.dtype)


def _merge(hn, ys, wgate, wbr):
    R, D = hn.shape
    tm = min(R, MERGE_ROW_TILE)
    bw = ys[0].shape[1]
    nc, _, tn4 = wgate.shape
    tn = tn4 // N_BRANCH
    return pl.pallas_call(
        _merge_kernel,
        grid=(R // tm, nc),
        in_specs=[pl.BlockSpec((tm, D), lambda i, c: (i, 0))]
        + [pl.BlockSpec((tm, bw), lambda i, c: (i, 0)) for _ in ys]
        + [pl.BlockSpec((None, D, tn4), lambda i, c: (c, 0, 0)),
           pl.BlockSpec((None, N_BRANCH, bw, tn), lambda i, c: (c, 0, 0, 0))],
        out_specs=pl.BlockSpec((tm, tn), lambda i, c: (i, c)),
        out_shape=jax.ShapeDtypeStruct((R, D), BF16),
        compiler_params=_cparams(("parallel", "arbitrary")),
        name="branch_merge",
    )(hn, *ys, wgate, wbr)


def _outproj_kernel(h_ref, m_ref, w_ref, o_ref):
    o_ref[...] = h_ref[...] + _dot(m_ref[...], w_ref[...])


def _outproj(h, merged, w):
    R, D = h.shape
    tm = min(R, ROW_TILE)
    return pl.pallas_call(
        _outproj_kernel,
        grid=(R // tm,),
        in_specs=[pl.BlockSpec((tm, D), lambda i: (i, 0)),
                  pl.BlockSpec((tm, D), lambda i: (i, 0)),
                  _const_spec(w.shape)],
        out_specs=pl.BlockSpec((tm, D), lambda i: (i, 0)),
        out_shape=jax.ShapeDtypeStruct((R, D), F32),
        compiler_params=_cparams(("parallel",)),
        name="out_proj",
    )(h, merged, w)


def _ffn_kernel(h_ref, g_ref, wa_ref, wv_ref, cw_ref, cb_ref, wd_ref, o_ref,
                hn_ref, carry_ref, *, tn, tiles_per_seq):
    i = pl.program_id(0)
    j = pl.program_id(1)
    tm = hn_ref.shape[0]

    @pl.when(j == 0)
    def _():
        hn_ref[...] = _rms(h_ref[...], g_ref[...]).astype(BF16)
        o_ref[...] = h_ref[...]

    @pl.when(i % tiles_per_seq == 0)
    def _():
        carry_ref[j] = jnp.zeros((SUBLANE, 2 * tn), F32)

    row8 = lax.broadcasted_iota(jnp.int32, (SUBLANE, tn), 0)

    def delayed(u, prev8, k):
        r = pltpu.roll(u, k, 0)
        top = jnp.where(row8 < k, pltpu.roll(prev8, k, 0), r[0:SUBLANE])
        return jnp.concatenate([top, r[SUBLANE:]], axis=0)

    def conv(u, prev8, col0):
        out = cb_ref[:, col0:col0 + tn] + u * cw_ref[CONV_W - 1:CONV_W, col0:col0 + tn]
        for tap in range(CONV_W - 1):
            out = out + delayed(u, prev8, CONV_W - 1 - tap) * cw_ref[tap:tap + 1, col0:col0 + tn]
        return out

    prev = carry_ref[j]
    prev_a, prev_v = prev[:, 0:tn], prev[:, tn:2 * tn]
    def up(rb):
        hn = hn_ref[rb * FF_SUB:(rb + 1) * FF_SUB, :]
        return _dot(hn, wa_ref[...]), _dot(hn, wv_ref[...])

    n_sub = tm // FF_SUB
    nxt = up(0)
    for rb in range(n_sub):
        ua, uv = nxt
        if rb + 1 < n_sub:
            nxt = up(rb + 1)
        a = conv(ua, prev_a, 0)
        v = conv(uv, prev_v, tn)
        prev_a, prev_v = ua[FF_SUB - SUBLANE:], uv[FF_SUB - SUBLANE:]
        act = (_silu(a) * v).astype(BF16)
        o_ref[rb * FF_SUB:(rb + 1) * FF_SUB, :] += _dot(act, wd_ref[...])
    carry_ref[j] = jnp.concatenate([prev_a, prev_v], axis=1)


def _ffn(h, g, wa, wv, cw, cb, wd, S):
    R, D = h.shape
    tm = min(S, FF_ROW_TILE)
    nj, _, tn = wa.shape
    return pl.pallas_call(
        functools.partial(_ffn_kernel, tn=tn, tiles_per_seq=S // tm),
        grid=(R // tm, nj),
        in_specs=[pl.BlockSpec((tm, D), lambda i, j: (i, 0), pipeline_mode=pl.Buffered(1)),
                  pl.BlockSpec((1, D), lambda i, j: (0, 0)),
                  pl.BlockSpec((None, D, tn), lambda i, j: (j, 0, 0)),
                  pl.BlockSpec((None, D, tn), lambda i, j: (j, 0, 0)),
                  pl.BlockSpec((None, CONV_W, 2 * tn), lambda i, j: (j, 0, 0)),
                  pl.BlockSpec((None, 1, 2 * tn), lambda i, j: (j, 0, 0)),
                  pl.BlockSpec((tn, D), lambda i, j: (j, 0))],
        out_specs=pl.BlockSpec((tm, D), lambda i, j: (i, 0)),
        out_shape=jax.ShapeDtypeStruct((R, D), F32),
        scratch_shapes=[pltpu.VMEM((tm, D), BF16),
                        pltpu.VMEM((nj, SUBLANE, 2 * tn), F32)],
        compiler_params=_cparams(("arbitrary", "arbitrary")),
        name="conv_ffn",
    )(h, g.reshape(1, D), wa, wv, cw, cb, wd)


def _pad_cols(w, n):
    return jnp.pad(w, ((0, 0), (0, n - w.shape[1])))


def _split_w_in(w_in, D):
    bw = D // 4
    hk = bw // 2
    sizes = [hk, hk, bw, bw, bw, hk, hk, bw, bw, GLA_RANK, bw, N_BRANCH * D]
    offs = np.concatenate([[0], np.cumsum(sizes)])
    seg = lambda a, b: w_in[:, offs[a]:offs[b]]
    w_ret = seg(0, 4)
    w_s5 = seg(4, 5)
    w_gla = jnp.concatenate([seg(5, 9), _pad_cols(seg(9, 10), LANE)], axis=1)
    w_pool = seg(10, 11)
    nc = D // MERGE_TILE
    w_gate = seg(11, 12).reshape(D, N_BRANCH, nc, MERGE_TILE).transpose(2, 0, 1, 3)
    w_gate = w_gate.reshape(nc, D, N_BRANCH * MERGE_TILE).astype(BF16)
    return [w.astype(BF16) for w in (w_ret, w_s5, w_gla, w_pool)], w_gate


def _branch_weights(w_branch, D):
    nb, bw, _ = w_branch.shape
    nc = D // MERGE_TILE
    return w_branch.reshape(nb, bw, nc, MERGE_TILE).transpose(2, 0, 1, 3).astype(BF16)


def _ffn_weights(w_up, conv_w, conv_b, w_down, tn):
    D, ff2 = w_up.shape
    ff = ff2 // 2
    ffp = -(-ff // tn) * tn
    nj = ffp // tn
    tiles = lambda m: _pad_cols(m, ffp).reshape(m.shape[0], nj, tn)
    wa = tiles(w_up[:, :ff]).transpose(1, 0, 2).astype(BF16)
    wv = tiles(w_up[:, ff:]).transpose(1, 0, 2).astype(BF16)
    cw = jnp.concatenate([tiles(conv_w[:, :ff]), tiles(conv_w[:, ff:])], axis=2).transpose(1, 0, 2)
    cbr = conv_b.reshape(1, ff2)
    cb = jnp.concatenate([tiles(cbr[:, :ff]), tiles(cbr[:, ff:])], axis=2).transpose(1, 0, 2)
    wd = jnp.pad(w_down, ((0, ffp - ff), (0, 0))).astype(BF16)
    return wa, wv, cw, cb, wd


def kernel(x, positions, norm_mix_g, w_in, s5_a_re, s5_a_im, s5_log_dt, s5_b_re, s5_b_im, s5_c_re, s5_c_im, s5_d, s5_w_glu, gla_w_gate, gla_b_gate, pool_w, pool_scale, w_branch, w_out, norm_ffn_g, w_up, conv_w, conv_b, w_down, final_g):
    B, S, D = x.shape
    depth = w_in.shape[0]
    bw = D // 4
    ret_dk, ret_dv = bw // (2 * RET_HEADS), bw // RET_HEADS
    gla_dk, gla_dv = bw // (2 * GLA_HEADS), bw // GLA_HEADS
    assert S % CHUNK == 0 and S % min(S, ROW_TILE) == 0 and B == SUBLANE
    assert ret_dk // 2 * 4 == LANE and bw == S5_BLOCK * LANE

    inv = ROPE_BASE ** (-jnp.arange(0, ret_dk, 2, dtype=F32) / ret_dk)
    ang = positions.astype(F32)[..., None] * inv
    ang = jnp.tile(ang, (1, 1, LANE // (ret_dk // 2))).reshape(B * S, LANE)

    h = x.reshape(B * S, D)
    for l in range(depth):
        ws, w_gate = _split_w_in(w_in[l], D)
        hn, p_ret, p_s5, p_gla, p_pool = _proj(h, norm_mix_g[l], ws)
        ya = _retention(p_ret, ang, B, S, RET_HEADS, ret_dk, ret_dv)
        a_row, bblk, cblk = _s5_params(s5_a_re[l], s5_a_im[l], s5_log_dt[l], s5_b_re[l],
                                       s5_b_im[l], s5_c_re[l], s5_c_im[l])
        yb = _s5(p_s5.reshape(B, S, bw), a_row, bblk, cblk, s5_d[l].reshape(1, bw),
                 s5_w_glu[l].astype(BF16)).reshape(B * S, bw)
        wg = jnp.pad(gla_w_gate[l], ((0, LANE - GLA_RANK), (0, 0))).astype(BF16)
        yc = _gla(p_gla, wg, gla_b_gate[l].reshape(1, -1), B, S, GLA_HEADS, gla_dk, gla_dv)
        yd = _pool(p_pool, pool_w[l].astype(BF16), pool_scale[l].reshape(1, bw), B, S)
        merged = _merge(hn, [ya, yb, yc, yd], w_gate, _branch_weights(w_branch[l], D))
        h = _outproj(h, merged, w_out[l].astype(BF16))
        ffw = _ffn_weights(w_up[l], conv_w[l], conv_b[l], w_down[l], FF_TILE)
        h = _ffn(h, norm_ffn_g[l], *ffw, S=S)
    return _final_norm(h, final_g).reshape(B, S, D)
```

```python
import functools
import math

import numpy as np
import jax
import jax.numpy as jnp
from jax import lax
from jax.experimental import pallas as pl
from jax.experimental.pallas import tpu as pltpu

F32 = jnp.float32
BF16 = jnp.bfloat16
EPS = 1e-6

N_BRANCH = 4
RET_HEADS = 4
GLA_HEADS = 4
GLA_RANK = 16
GLA_GATE_TEMP = 16.0
S5_GROUP = 16
S5_STATE = 64
POOL_WINDOWS = (2, 4, 8, 16)
POOL_PAD = 16
ROPE_BASE = 10000.0
CONV_W = 3

LANE = 128
SUBLANE = 8
CHUNK = 128
VMEM_LIMIT = 56 * 1024 * 1024
ROW_TILE = 512
MERGE_ROW_TILE = 1024
MERGE_TILE = 512
MERGE_SUB = 256
FF_ROW_TILE = 512
FF_TILE = 512
FF_SUB = 256
SEQ_TILE = 512
S5_TILE = 64
S5_PITCH = S5_TILE + 4
S5_BLOCK = 4


def _cparams(sem):
    return pltpu.CompilerParams(dimension_semantics=sem, vmem_limit_bytes=VMEM_LIMIT)


def _const_spec(shape):
    nd = len(shape)
    return pl.BlockSpec(shape, lambda *_: (0,) * nd, pipeline_mode=pl.Buffered(1))


def _dot(a, b):
    return jnp.dot(a, b, preferred_element_type=F32)


def _dot_nt(a, b):
    return lax.dot_general(a, b, (((1,), (1,)), ((), ())), preferred_element_type=F32)


def _rms(x, g):
    r = lax.rsqrt(jnp.mean(x * x, axis=-1, keepdims=True) + EPS)
    return x * r * g


def _head_norm(o):
    mu = jnp.mean(o, axis=-1, keepdims=True)
    d = o - mu
    var = jnp.mean(d * d, axis=-1, keepdims=True)
    return d * lax.rsqrt(var + EPS)


def _silu(x):
    return x * jax.nn.sigmoid(x)


def _seq_map(nt):
    return lambda b, t: (b * nt + t, 0)


def _final_norm_kernel(h_ref, g_ref, o_ref):
    o_ref[...] = _rms(h_ref[...], g_ref[...])


def _final_norm(h, g):
    R, D = h.shape
    tm = min(R, ROW_TILE)
    return pl.pallas_call(
        _final_norm_kernel,
        grid=(R // tm,),
        in_specs=[pl.BlockSpec((tm, D), lambda i: (i, 0)), _const_spec((1, D))],
        out_specs=pl.BlockSpec((tm, D), lambda i: (i, 0)),
        out_shape=jax.ShapeDtypeStruct((R, D), F32),
        compiler_params=_cparams(("parallel",)),
        name="final_norm",
    )(h, g.reshape(1, D))


def _proj_kernel(h_ref, g_ref, w_ret, w_s5, w_gla, w_pool, o_hn, o_ret, o_s5, o_gla, o_pool):
    hn = _rms(h_ref[...], g_ref[...]).astype(BF16)
    o_hn[...] = hn
    o_ret[...] = _dot(hn, w_ret[...])
    o_s5[...] = _dot(hn, w_s5[...])
    o_gla[...] = _dot(hn, w_gla[...])
    o_pool[...] = _dot(hn, w_pool[...])


def _proj(h, g, ws):
    R, D = h.shape
    tm = min(R, ROW_TILE)
    widths = [w.shape[1] for w in ws]
    return pl.pallas_call(
        _proj_kernel,
        grid=(R // tm,),
        in_specs=[pl.BlockSpec((tm, D), lambda i: (i, 0)), _const_spec((1, D))]
        + [_const_spec(w.shape) for w in ws],
        out_specs=[pl.BlockSpec((tm, n), lambda i: (i, 0)) for n in [D] + widths],
        out_shape=[jax.ShapeDtypeStruct((R, D), BF16)]
        + [jax.ShapeDtypeStruct((R, n), F32) for n in widths],
        compiler_params=_cparams(("parallel",)),
        name="mixer_proj",
    )(h, g.reshape(1, D), *ws)


def _ret_consts(heads, dk, dv):
    log_g = np.log(1.0 - 2.0 ** (-5.0 - np.arange(heads, dtype=np.float64)))
    idx = np.arange(CHUNK, dtype=np.float64)
    rel = idx[:, None] - idx[None, :]
    dmask = np.where(rel >= 0, np.exp(np.maximum(rel, 0.0)[None] * log_g[:, None, None]), 0.0)
    lane_g = np.repeat(log_g, dk)[None, :]
    qdec = np.exp((idx[:, None] + 1.0) * lane_g)
    kdec = np.exp((CHUNK - 1.0 - idx)[:, None] * lane_g)
    sdec = np.exp(CHUNK * np.repeat(log_g, dk))[:, None] * np.ones((1, heads * dv))
    bmask = np.kron(np.eye(heads), np.ones((dk, dv)))
    f = lambda a: jnp.asarray(a, F32)
    return f(dmask), f(qdec), f(kdec), f(sdec), f(bmask)


def _rope_kernel(ang_ref, cos_ref, ssin_ref, *, half):
    ang = ang_ref[...]
    lane = lax.broadcasted_iota(jnp.int32, ang.shape, 1)
    cos_ref[...] = jnp.cos(ang)
    s = jnp.sin(ang)
    ssin_ref[...] = jnp.where((lane % (2 * half)) < half, -s, s)


def _rope_tables(ang, half):
    R, W = ang.shape
    tm = min(R, 2048)
    spec = pl.BlockSpec((tm, W), lambda i: (i, 0))
    return pl.pallas_call(
        functools.partial(_rope_kernel, half=half),
        grid=(R // tm,),
        in_specs=[spec],
        out_specs=[spec, spec],
        out_shape=[jax.ShapeDtypeStruct((R, W), F32)] * 2,
        compiler_params=_cparams(("parallel",)),
        name="rope_tables",
    )(ang)


def _ret_kernel(x_ref, cos_ref, ssin_ref, dmask_ref, qdec_ref, kdec_ref, sdec_ref, bmask_ref,
                o_ref, state_ref, *, heads, dk, dv, n_chunks):
    hk = heads * dk
    hv = heads * dv

    @pl.when(pl.program_id(1) == 0)
    def _():
        state_ref[...] = jnp.zeros_like(state_ref)

    lane = lax.broadcasted_iota(jnp.int32, (CHUNK, hk), 1)
    first_half = (lane % dk) < (dk // 2)
    head_of_lane = lane // dk

    def rotary(x, cos2, ssin):
        swapped = jnp.where(first_half, pltpu.roll(x, hk - dk // 2, 1), pltpu.roll(x, dk // 2, 1))
        return x * cos2 + swapped * ssin

    def chunk(ci, carry):
        r0 = pl.multiple_of(ci * CHUNK, CHUNK)
        rows = pl.ds(r0, CHUNK)
        reps = hk // LANE
        cos2 = jnp.concatenate([cos_ref[rows, :]] * reps, axis=1)
        ssin = jnp.concatenate([ssin_ref[rows, :]] * reps, axis=1)
        q = rotary(x_ref[rows, 0:hk], cos2, ssin)
        k = rotary(x_ref[rows, hk:2 * hk], cos2, ssin) * (dk ** -0.5)
        v = x_ref[rows, 2 * hk:2 * hk + hv]
        gate = x_ref[rows, 2 * hk + hv:2 * hk + 2 * hv]
        vb = v.astype(BF16)
        kb = k.astype(BF16)
        state = state_ref[...]
        cross = _dot((q * qdec_ref[...]).astype(BF16), state.astype(BF16))
        outs = []
        for h in range(heads):
            qh = jnp.where(head_of_lane == h, q, 0.0).astype(BF16)
            s = _dot_nt(qh, kb) * dmask_ref[h]
            outs.append(_dot(s.astype(BF16), vb[:, h * dv:(h + 1) * dv]))
        o = jnp.concatenate(outs, axis=1) + cross
        kd = (k * kdec_ref[...]).astype(BF16)
        upd = _dot(kd.T, vb)
        state_ref[...] = sdec_ref[...] * state + bmask_ref[...] * upd
        y = jnp.concatenate(
            [_head_norm(o[:, h * dv:(h + 1) * dv]) for h in range(heads)], axis=1)
        o_ref[rows, :] = (y * _silu(gate)).astype(o_ref.dtype)
        return carry

    lax.fori_loop(0, n_chunks, chunk, 0)


def _retention(x, rope, B, S, heads, dk, dv):
    tt = min(S, SEQ_TILE)
    nt = S // tt
    win = 2 * heads * dk + 2 * heads * dv
    consts = _ret_consts(heads, dk, dv)
    kern = functools.partial(_ret_kernel, heads=heads, dk=dk, dv=dv, n_chunks=tt // CHUNK)
    return pl.pallas_call(
        kern,
        grid=(B, nt),
        in_specs=[pl.BlockSpec((tt, win), _seq_map(nt)),
                  pl.BlockSpec((tt, LANE), _seq_map(nt)), pl.BlockSpec((tt, LANE), _seq_map(nt))]
        + [_const_spec(c.shape) for c in consts],
        out_specs=pl.BlockSpec((tt, heads * dv), _seq_map(nt)),
        out_shape=jax.ShapeDtypeStruct((B * S, heads * dv), BF16),
        scratch_shapes=[pltpu.VMEM((heads * dk, heads * dv), F32)],
        compiler_params=_cparams(("parallel", "arbitrary")),
        name="retention",
    )(x, *rope, *consts)


def _gla_consts(heads, dk, dv):
    C = CHUNK
    n_lvl = int(math.log2(C))
    t = np.arange(C)
    sums = []
    masks = []
    for l in range(n_lvl):
        bit = (t >> l) & 1
        start = (t >> l) << l
        r = t[None, :]
        upper = (bit[:, None] == 1) & (r >= start[:, None]) & (r <= t[:, None])
        lower = (bit[:, None] == 0) & (r > t[:, None]) & (r < (start + (1 << l))[:, None])
        sums.append((upper | lower).astype(np.float64))
        m = ((bit[:, None] == 1) & (bit[None, :] == 0)
             & ((t[:, None] >> (l + 1)) == (t[None, :] >> (l + 1))))
        masks.append(np.tile(m.astype(np.float64), (1, heads)))
    masks.append(np.tile(np.eye(C), (1, heads)))
    sums.append((t[None, :] <= t[:, None]).astype(np.float64))
    sums.append((t[None, :] > t[:, None]).astype(np.float64))
    g = np.concatenate(sums, axis=0)
    gcat = np.concatenate([g, g], axis=1)
    bmask = np.kron(np.eye(heads), np.ones((dk, dv)))
    return jnp.asarray(gcat, BF16), jnp.asarray(np.stack(masks), F32), jnp.asarray(bmask, F32)


def _log_sigmoid(x):
    return jnp.minimum(x, 0.0) - jnp.log(1.0 + jnp.exp(-jnp.abs(x)))


def _gla_kernel(x_ref, wg_ref, bg_ref, gcat_ref, masks_ref, bmask_ref, o_ref, state_ref,
                *, heads, dk, dv, n_chunks):
    hk = heads * dk
    hv = heads * dv
    C = CHUNK
    n_lvl = int(math.log2(C))

    @pl.when(pl.program_id(1) == 0)
    def _():
        state_ref[...] = jnp.zeros_like(state_ref)

    same_head = (lax.broadcasted_iota(jnp.int32, (hk, heads * C), 0) // dk
                 == lax.broadcasted_iota(jnp.int32, (hk, heads * C), 1) // C)

    def key_blocks(kz):
        return jnp.where(same_head, jnp.concatenate([kz.T] * heads, axis=1), 0.0).astype(BF16)

    def chunk(ci, carry):
        r0 = pl.multiple_of(ci * C, C)
        rows = pl.ds(r0, C)
        q = x_ref[rows, 0:hk] * (dk ** -0.5)
        k = x_ref[rows, hk:2 * hk]
        v = x_ref[rows, 2 * hk:2 * hk + hv]
        gate = x_ref[rows, 2 * hk + hv:2 * hk + 2 * hv]
        code = x_ref[rows, 2 * hk + 2 * hv:2 * hk + 2 * hv + LANE]
        vb = v.astype(BF16)
        log_a = _log_sigmoid(_dot(code.astype(BF16), wg_ref[...]) + bg_ref[...]) * (1.0 / GLA_GATE_TEMP)
        hi = log_a.astype(BF16)
        lo = (log_a - hi.astype(F32)).astype(BF16)
        sums = _dot(gcat_ref[...], jnp.concatenate([hi, lo], axis=0))

        scores = masks_ref[n_lvl] * _dot(q.astype(BF16), key_blocks(k))
        for l in range(n_lvl):
            z = jnp.exp(sums[l * C:(l + 1) * C])
            scores = scores + masks_ref[l] * _dot((q * z).astype(BF16), key_blocks(k * z))
        sb = scores.astype(BF16)
        intra = jnp.concatenate(
            [_dot(sb[:, h * C:(h + 1) * C], vb[:, h * dv:(h + 1) * dv]) for h in range(heads)], axis=1)

        state = state_ref[...]
        e_pre = sums[n_lvl * C:(n_lvl + 1) * C]
        e_suf = sums[(n_lvl + 1) * C:(n_lvl + 2) * C]
        cross = _dot((q * jnp.exp(e_pre)).astype(BF16), state.astype(BF16))
        kd = (k * jnp.exp(e_suf)).astype(BF16)
        upd = _dot(kd.T, vb)
        total = jnp.broadcast_to(jnp.exp(e_pre[C - 1:C, :]), (LANE, hk))
        dec = jnp.concatenate([total.T] * (hv // LANE), axis=1)
        state_ref[...] = dec * state + bmask_ref[...] * upd

        o = intra + cross
        y = jnp.concatenate(
            [_head_norm(o[:, h * dv:(h + 1) * dv]) for h in range(heads)], axis=1)
        o_ref[rows, :] = (y * _silu(gate)).astype(o_ref.dtype)
        return carry

    lax.fori_loop(0, n_chunks, chunk, 0)


def _gla(x, wg, bg, B, S, heads, dk, dv):
    tt = min(S, SEQ_TILE)
    nt = S // tt
    win = 2 * heads * dk + 2 * heads * dv + LANE
    consts = _gla_consts(heads, dk, dv)
    kern = functools.partial(_gla_kernel, heads=heads, dk=dk, dv=dv, n_chunks=tt // CHUNK)
    return pl.pallas_call(
        kern,
        grid=(B, nt),
        in_specs=[pl.BlockSpec((tt, win), _seq_map(nt)),
                  _const_spec(wg.shape), _const_spec(bg.shape)]
        + [_const_spec(c.shape) for c in consts],
        out_specs=pl.BlockSpec((tt, heads * dv), _seq_map(nt)),
        out_shape=jax.ShapeDtypeStruct((B * S, heads * dv), BF16),
        scratch_shapes=[pltpu.VMEM((heads * dk, heads * dv), F32)],
        compiler_params=_cparams(("parallel", "arbitrary")),
        name="gla",
    )(x, wg, bg, *consts)


def _s5_kernel(u_ref, a_ref, bblk_ref, cblk_ref, d_ref, wglu_ref, o_ref,
               stage_ref, utb_ref, xs_ref, ytb_ref, st_ref, *, tt):
    nb, w = u_ref.shape[0], u_ref.shape[2]
    nblk = w // LANE
    sw = a_ref.shape[1] // (2 * nblk)

    @pl.when(pl.program_id(0) == 0)
    def _():
        st_ref[...] = jnp.zeros_like(st_ref)

    for b in range(nb):
        for m in range(nblk):
            stage_ref[m, b * S5_PITCH:b * S5_PITCH + tt, :] = u_ref[b, :, m * LANE:(m + 1) * LANE]

    def gather(t, carry):
        rows = pl.ds(pl.multiple_of(t * nb, nb), nb)
        for m in range(nblk):
            utb_ref[rows, m * LANE:(m + 1) * LANE] = stage_ref[m, pl.ds(t, nb, stride=S5_PITCH), :]
        return carry

    lax.fori_loop(0, tt, gather, 0)

    u = utb_ref[...]
    ub = u.astype(BF16)
    for m in range(nblk):
        xs_ref[:, 2 * sw * m:2 * sw * (m + 1)] = _dot(ub[:, m * LANE:(m + 1) * LANE], bblk_ref[m])

    def step(t, carry):
        rows = pl.ds(pl.multiple_of(t * nb, nb), nb)
        new = []
        for m in range(nblk):
            xr, xi = carry[2 * m], carry[2 * m + 1]
            re = slice(2 * sw * m, 2 * sw * m + sw)
            im = slice(2 * sw * m + sw, 2 * sw * (m + 1))
            ar, ai = a_ref[:, re], a_ref[:, im]
            nr = ar * xr - ai * xi + xs_ref[rows, re]
            ni = ar * xi + ai * xr + xs_ref[rows, im]
            xs_ref[rows, re] = nr
            xs_ref[rows, im] = ni
            new += [nr, ni]
        return tuple(new)

    init = tuple(st_ref[:, sw * c:sw * (c + 1)] for c in range(2 * nblk))
    last = lax.fori_loop(0, tt, step, init, unroll=2)
    for c in range(2 * nblk):
        st_ref[:, sw * c:sw * (c + 1)] = last[c]

    ys = [_dot(xs_ref[:, 2 * sw * m:2 * sw * (m + 1)].astype(BF16), cblk_ref[m]) for m in range(nblk)]
    y = jnp.concatenate(ys, axis=1) + d_ref[...] * u
    y = jax.nn.gelu(y)
    y = y * jax.nn.sigmoid(_dot(y.astype(BF16), wglu_ref[...]))

    for m in range(nblk):
        ytb_ref[m] = y[:, m * LANE:(m + 1) * LANE]
    for b in range(nb):
        for m in range(nblk):
            o_ref[b, :, m * LANE:(m + 1) * LANE] = (
                ytb_ref[m, pl.ds(b, tt, stride=nb), :].astype(o_ref.dtype))


def _s5(u3, a_row, bblk, cblk, d_row, wglu):
    B, S, W = u3.shape
    tt = min(S, S5_TILE)
    ns2 = a_row.shape[1]
    nblk = W // LANE
    a8 = jnp.broadcast_to(a_row, (B, ns2))
    return pl.pallas_call(
        functools.partial(_s5_kernel, tt=tt),
        grid=(S // tt,),
        in_specs=[pl.BlockSpec((B, tt, W), lambda t: (0, t, 0)),
                  _const_spec(a8.shape), _const_spec(bblk.shape), _const_spec(cblk.shape),
                  _const_spec(d_row.shape), _const_spec(wglu.shape)],
        out_specs=pl.BlockSpec((B, tt, W), lambda t: (0, t, 0)),
        out_shape=jax.ShapeDtypeStruct((B, S, W), BF16),
        scratch_shapes=[pltpu.VMEM((nblk, B * S5_PITCH, LANE), F32),
                        pltpu.VMEM((tt * B, W), F32),
                        pltpu.VMEM((tt * B, ns2), F32),
                        pltpu.VMEM((nblk, tt * B, LANE), F32),
                        pltpu.VMEM((B, ns2), F32)],
        compiler_params=_cparams(("arbitrary",)),
        name="s5",
    )(u3, a8, bblk, cblk, d_row, wglu)


def _s5_params(a_re, a_im, log_dt, b_re, b_im, c_re, c_im):
    G, P = a_re.shape
    n_c = b_re.shape[-1]
    gpb = LANE // n_c
    nblk = G // gpb
    dt = jnp.exp(log_dt)[:, None]
    mag = jnp.exp(a_re * dt)
    abar_re = mag * jnp.cos(a_im * dt)
    abar_im = mag * jnp.sin(a_im * dt)
    den = a_re * a_re + a_im * a_im
    nr, ni = abar_re - 1.0, abar_im
    f_re = (nr * a_re + ni * a_im) / den
    f_im = (ni * a_re - nr * a_im) / den
    bb_re = f_re[..., None] * b_re - f_im[..., None] * b_im
    bb_im = f_re[..., None] * b_im + f_im[..., None] * b_re
    eye = jnp.eye(gpb, dtype=F32)

    def bd_in(m):
        m = m.reshape(nblk, gpb, P, n_c)
        return jnp.einsum('ngpc,gh->ngchp', m, eye).reshape(nblk, gpb * n_c, gpb * P)

    def bd_out(m):
        m = m.reshape(nblk, gpb, n_c, P)
        return jnp.einsum('ngcp,gh->ngphc', m, eye).reshape(nblk, gpb * P, gpb * n_c)

    a_row = jnp.concatenate([abar_re.reshape(nblk, 1, gpb * P), abar_im.reshape(nblk, 1, gpb * P)],
                            axis=2).reshape(1, 2 * G * P)
    bblk = jnp.concatenate([bd_in(bb_re), bd_in(bb_im)], axis=2).astype(BF16)
    cblk = jnp.concatenate([bd_out(c_re), -bd_out(c_im)], axis=1).astype(BF16)
    return a_row, bblk, cblk


def _pool_kernel(p_ref, w_ref, sc_ref, o_ref, ext_ref, *, tt):
    pad = POOL_PAD
    t_idx = pl.program_id(1)

    @pl.when(t_idx == 0)
    def _():
        ext_ref[0:pad, :] = jnp.zeros((pad, ext_ref.shape[1]), F32)

    ext_ref[pad:pad + tt, :] = p_ref[...]
    pos1 = (lax.broadcasted_iota(jnp.int32, (tt, LANE), 0) + t_idx * tt + 1).astype(F32)
    outs = []
    for g, win in enumerate(POOL_WINDOWS):
        cols = slice(g * LANE, (g + 1) * LANE)
        acc = ext_ref[pad:pad + tt, cols]
        for j in range(1, win):
            acc = acc + ext_ref[pad - j:pad - j + tt, cols]
        cur = ext_ref[pad:pad + tt, cols]
        mixed = acc / jnp.minimum(pos1, float(win)) - cur
        outs.append(_dot(mixed.astype(BF16), w_ref[g]))
    o_ref[...] = (jnp.concatenate(outs, axis=1) * sc_ref[...]).astype(o_ref.dtype)
    ext_ref[0:pad, :] = ext_ref[tt:tt + pad, :]


def _pool(p, w, scale, B, S):
    tt = min(S, SEQ_TILE)
    nt = S // tt
    W = w.shape[0] * LANE
    return pl.pallas_call(
        functools.partial(_pool_kernel, tt=tt),
        grid=(B, nt),
        in_specs=[pl.BlockSpec((tt, W), _seq_map(nt)),
                  _const_spec(w.shape), _const_spec(scale.shape)],
        out_specs=pl.BlockSpec((tt, W), _seq_map(nt)),
        out_shape=jax.ShapeDtypeStruct((B * S, W), BF16),
        scratch_shapes=[pltpu.VMEM((tt + POOL_PAD, W), F32)],
        compiler_params=_cparams(("parallel", "arbitrary")),
        name="pool",
    )(p, w, scale)


def _merge_kernel(hn_ref, *refs):
    ys = refs[0:N_BRANCH]
    wgates = refs[N_BRANCH:2 * N_BRANCH]
    wbrs = refs[2 * N_BRANCH:3 * N_BRANCH]
    o_ref = refs[3 * N_BRANCH]
    tm = o_ref.shape[0]
    for rb in range(tm // MERGE_SUB):
        rows = slice(rb * MERGE_SUB, (rb + 1) * MERGE_SUB)
        hn = hn_ref[rows, :]
        acc = None
        for n in range(N_BRANCH):
            logits = _dot(hn, wgates[n][...])
            term = jax.nn.sigmoid(logits) * _dot(ys[n][rows, :], wbrs[n][...])
            acc = term if acc is None else acc + term
        o_ref[rows, :] = acc.astype(o_ref.dtype)


def _merge(hn, ys, wgate, wbr):
    R, D = hn.shape
    tm = min(R, MERGE_ROW_TILE)
    bw = ys[0].shape[1]
    tn = MERGE_TILE
    nc = D // tn
    gate_spec = lambda n: pl.BlockSpec((D, tn), lambda i, c: (0, n * nc + c))
    br_spec = lambda n: pl.BlockSpec((None, bw, tn), lambda i, c: (n, 0, c))
    return pl.pallas_call(
        _merge_kernel,
        grid=(R // tm, nc),
        in_specs=[pl.BlockSpec((tm, D), lambda i, c: (i, 0))]
        + [pl.BlockSpec((tm, bw), lambda i, c: (i, 0)) for _ in ys]
        + [gate_spec(n) for n in range(N_BRANCH)] + [br_spec(n) for n in range(N_BRANCH)],
        out_specs=pl.BlockSpec((tm, tn), lambda i, c: (i, c)),
        out_shape=jax.ShapeDtypeStruct((R, D), BF16),
        compiler_params=_cparams(("parallel", "arbitrary")),
        name="branch_merge",
    )(hn, *ys, *([wgate] * N_BRANCH), *([wbr] * N_BRANCH))


def _outproj_kernel(h_ref, m_ref, w_ref, o_ref):
    o_ref[...] = h_ref[...] + _dot(m_ref[...], w_ref[...])


def _outproj(h, merged, w):
    R, D = h.shape
    tm = min(R, ROW_TILE)
    return pl.pallas_call(
        _outproj_kernel,
        grid=(R // tm,),
        in_specs=[pl.BlockSpec((tm, D), lambda i: (i, 0)),
                  pl.BlockSpec((tm, D), lambda i: (i, 0)),
                  _const_spec(w.shape)],
        out_specs=pl.BlockSpec((tm, D), lambda i: (i, 0)),
        out_shape=jax.ShapeDtypeStruct((R, D), F32),
        compiler_params=_cparams(("parallel",)),
        name="out_proj",
    )(h, merged, w)


def _ffn_kernel(h_ref, g_ref, wa_ref, wv_ref, cw_ref, cb_ref, wd_ref, o_ref,
                hn_ref, carry_ref, *, tn, tiles_per_seq):
    i = pl.program_id(0)
    j = pl.program_id(1)
    tm = hn_ref.shape[0]

    @pl.when(j == 0)
    def _():
        hn_ref[...] = _rms(h_ref[...], g_ref[...]).astype(BF16)
        o_ref[...] = h_ref[...]

    @pl.when(i % tiles_per_seq == 0)
    def _():
        carry_ref[j] = jnp.zeros((SUBLANE, 2 * tn), F32)

    row8 = lax.broadcasted_iota(jnp.int32, (SUBLANE, tn), 0)

    def delayed(u, prev8, k):
        r = pltpu.roll(u, k, 0)
        top = jnp.where(row8 < k, pltpu.roll(prev8, k, 0), r[0:SUBLANE])
        return jnp.concatenate([top, r[SUBLANE:]], axis=0)

    def conv(u, prev8, col0):
        out = cb_ref[:, col0:col0 + tn] + u * cw_ref[CONV_W - 1:CONV_W, col0:col0 + tn]
        for tap in range(CONV_W - 1):
            out = out + delayed(u, prev8, CONV_W - 1 - tap) * cw_ref[tap:tap + 1, col0:col0 + tn]
        return out

    prev = carry_ref[j]
    prev_a, prev_v = prev[:, 0:tn], prev[:, tn:2 * tn]
    def up(rb):
        hn = hn_ref[rb * FF_SUB:(rb + 1) * FF_SUB, :]
        return _dot(hn, wa_ref[...]), _dot(hn, wv_ref[...])

    n_sub = tm // FF_SUB
    nxt = up(0)
    for rb in range(n_sub):
        ua, uv = nxt
        if rb + 1 < n_sub:
            nxt = up(rb + 1)
        a = conv(ua, prev_a, 0)
        v = conv(uv, prev_v, tn)
        prev_a, prev_v = ua[FF_SUB - SUBLANE:], uv[FF_SUB - SUBLANE:]
        act = (_silu(a) * v).astype(BF16)
        o_ref[rb * FF_SUB:(rb + 1) * FF_SUB, :] += _dot(act, wd_ref[...])
    carry_ref[j] = jnp.concatenate([prev_a, prev_v], axis=1)


def _ffn(h, g, wa, wv, cw, cb, wd, S):
    R, D = h.shape
    tm = min(S, FF_ROW_TILE)
    tn = FF_TILE
    nj = wa.shape[1] // tn
    return pl.pallas_call(
        functools.partial(_ffn_kernel, tn=tn, tiles_per_seq=S // tm),
        grid=(R // tm, nj),
        in_specs=[pl.BlockSpec((tm, D), lambda i, j: (i, 0)),
                  pl.BlockSpec((1, D), lambda i, j: (0, 0)),
                  pl.BlockSpec((D, tn), lambda i, j: (0, j)),
                  pl.BlockSpec((D, tn), lambda i, j: (0, j)),
                  pl.BlockSpec((None, CONV_W, 2 * tn), lambda i, j: (j, 0, 0)),
                  pl.BlockSpec((None, 1, 2 * tn), lambda i, j: (j, 0, 0)),
                  pl.BlockSpec((tn, D), lambda i, j: (j, 0))],
        out_specs=pl.BlockSpec((tm, D), lambda i, j: (i, 0)),
        out_shape=jax.ShapeDtypeStruct((R, D), F32),
        scratch_shapes=[pltpu.VMEM((tm, D), BF16),
                        pltpu.VMEM((nj, SUBLANE, 2 * tn), F32)],
        compiler_params=_cparams(("arbitrary", "arbitrary")),
        name="conv_ffn",
    )(h, g.reshape(1, D), wa, wv, cw, cb, wd)


def _pad_cols(w, n):
    return jnp.pad(w, ((0, 0), (0, n - w.shape[1])))


def _split_w_in(w_in, D):
    bw = D // 4
    hk = bw // 2
    sizes = [hk, hk, bw, bw, bw, hk, hk, bw, bw, GLA_RANK, bw, N_BRANCH * D]
    offs = np.concatenate([[0], np.cumsum(sizes)])
    seg = lambda a, b: w_in[:, offs[a]:offs[b]]
    w_ret = seg(0, 4)
    w_s5 = seg(4, 5)
    w_gla = jnp.concatenate([seg(5, 9), _pad_cols(seg(9, 10), LANE)], axis=1)
    w_pool = seg(10, 11)
    return [w.astype(BF16) for w in (w_ret, w_s5, w_gla, w_pool)], seg(11, 12).astype(BF16)


def _ffn_weights(w_up, conv_w, conv_b, w_down, tn):
    D, ff2 = w_up.shape
    ff = ff2 // 2
    ffp = -(-ff // tn) * tn
    nj = ffp // tn
    wa = _pad_cols(w_up[:, :ff], ffp).astype(BF16)
    wv = _pad_cols(w_up[:, ff:], ffp).astype(BF16)
    tiles = lambda m: _pad_cols(m, ffp).reshape(m.shape[0], nj, tn)
    cw = jnp.concatenate([tiles(conv_w[:, :ff]), tiles(conv_w[:, ff:])], axis=2).transpose(1, 0, 2)
    cbr = conv_b.reshape(1, ff2)
    cb = jnp.concatenate([tiles(cbr[:, :ff]), tiles(cbr[:, ff:])], axis=2).transpose(1, 0, 2)
    wd = jnp.pad(w_down, ((0, ffp - ff), (0, 0))).astype(BF16)
    return wa, wv, cw, cb, wd


def kernel(x, positions, norm_mix_g, w_in, s5_a_re, s5_a_im, s5_log_dt, s5_b_re, s5_b_im, s5_c_re, s5_c_im, s5_d, s5_w_glu, gla_w_gate, gla_b_gate, pool_w, pool_scale, w_branch, w_out, norm_ffn_g, w_up, conv_w, conv_b, w_down, final_g):
    B, S, D = x.shape
    depth = w_in.shape[0]
    bw = D // 4
    ret_dk, ret_dv = bw // (2 * RET_HEADS), bw // RET_HEADS
    gla_dk, gla_dv = bw // (2 * GLA_HEADS), bw // GLA_HEADS
    assert S % CHUNK == 0 and S % min(S, ROW_TILE) == 0 and B == SUBLANE
    assert ret_dk // 2 * 4 == LANE and bw == S5_BLOCK * LANE

    inv = ROPE_BASE ** (-jnp.arange(0, ret_dk, 2, dtype=F32) / ret_dk)
    ang = positions.astype(F32)[..., None] * inv
    ang = jnp.tile(ang, (1, 1, LANE // (ret_dk // 2))).reshape(B * S, LANE)
    rope = _rope_tables(ang, ret_dk // 2)

    h = x.reshape(B * S, D)
    for l in range(depth):
        ws, w_gate = _split_w_in(w_in[l], D)
        hn, p_ret, p_s5, p_gla, p_pool = _proj(h, norm_mix_g[l], ws)
        ya = _retention(p_ret, rope, B, S, RET_HEADS, ret_dk, ret_dv)
        a_row, bblk, cblk = _s5_params(s5_a_re[l], s5_a_im[l], s5_log_dt[l], s5_b_re[l],
                                       s5_b_im[l], s5_c_re[l], s5_c_im[l])
        yb = _s5(p_s5.reshape(B, S, bw), a_row, bblk, cblk, s5_d[l].reshape(1, bw),
                 s5_w_glu[l].astype(BF16)).reshape(B * S, bw)
        wg = jnp.pad(gla_w_gate[l], ((0, LANE - GLA_RANK), (0, 0))).astype(BF16)
        yc = _gla(p_gla, wg, gla_b_gate[l].reshape(1, -1), B, S, GLA_HEADS, gla_dk, gla_dv)
        yd = _pool(p_pool, pool_w[l].astype(BF16), pool_scale[l].reshape(1, bw), B, S)
        merged = _merge(hn, [ya, yb, yc, yd], w_gate, w_branch[l].astype(BF16))
        h = _outproj(h, merged, w_out[l].astype(BF16))
        ffw = _ffn_weights(w_up[l], conv_w[l], conv_b[l], w_down[l], FF_TILE)
        h = _ffn(h, norm_ffn_g[l], *ffw, S=S)
    return _final_norm(h, final_g).reshape(B, S, D)
```

```python
import functools
import math

import numpy as np
import jax
import jax.numpy as jnp
from jax import lax
from jax.experimental import pallas as pl
from jax.experimental.pallas import tpu as pltpu

F32 = jnp.float32
BF16 = jnp.bfloat16
EPS = 1e-6

N_BRANCH = 4
RET_HEADS = 4
GLA_HEADS = 4
GLA_RANK = 16
GLA_GATE_TEMP = 16.0
S5_GROUP = 16
S5_STATE = 64
POOL_WINDOWS = (2, 4, 8, 16)
POOL_PAD = 16
ROPE_BASE = 10000.0
CONV_W = 3

LANE = 128
SUBLANE = 8
CHUNK = 128
VMEM_LIMIT = 56 * 1024 * 1024
ROW_TILE = 512
MERGE_ROW_TILE = 1024
MERGE_TILE = 512
MERGE_SUB = 256
FF_ROW_TILE = 512
FF_TILE = 512
FF_SUB = 128
SEQ_TILE = 512
S5_TILE = 64
S5_PITCH = S5_TILE + 4
S5_BLOCK = 4


def _cparams(sem):
    return pltpu.CompilerParams(dimension_semantics=sem, vmem_limit_bytes=VMEM_LIMIT)


def _const_spec(shape):
    nd = len(shape)
    return pl.BlockSpec(shape, lambda *_: (0,) * nd, pipeline_mode=pl.Buffered(1))


def _dot(a, b):
    return jnp.dot(a, b, preferred_element_type=F32)


def _dot_nt(a, b):
    return lax.dot_general(a, b, (((1,), (1,)), ((), ())), preferred_element_type=F32)


def _rms(x, g):
    r = lax.rsqrt(jnp.mean(x * x, axis=-1, keepdims=True) + EPS)
    return x * r * g


def _head_norm(o):
    mu = jnp.mean(o, axis=-1, keepdims=True)
    d = o - mu
    var = jnp.mean(d * d, axis=-1, keepdims=True)
    return d * lax.rsqrt(var + EPS)


def _silu(x):
    return x * jax.nn.sigmoid(x)


def _key_blocks(k, heads, dk):
    kt = k.T.astype(BF16)
    zero = jnp.zeros((dk, kt.shape[1]), BF16)
    cols = [jnp.concatenate([kt[h * dk:(h + 1) * dk] if r == h else zero for r in range(heads)], axis=0)
            for h in range(heads)]
    return jnp.concatenate(cols, axis=1)


def _seq_map(nt):
    return lambda b, t: (b * nt + t, 0)


def _final_norm_kernel(h_ref, g_ref, o_ref):
    o_ref[...] = _rms(h_ref[...], g_ref[...])


def _final_norm(h, g):
    R, D = h.shape
    tm = min(R, ROW_TILE)
    return pl.pallas_call(
        _final_norm_kernel,
        grid=(R // tm,),
        in_specs=[pl.BlockSpec((tm, D), lambda i: (i, 0)), _const_spec((1, D))],
        out_specs=pl.BlockSpec((tm, D), lambda i: (i, 0)),
        out_shape=jax.ShapeDtypeStruct((R, D), F32),
        compiler_params=_cparams(("parallel",)),
        name="final_norm",
    )(h, g.reshape(1, D))


def _proj_kernel(h_ref, g_ref, w_ret, w_s5, w_gla, w_pool, o_hn, o_ret, o_s5, o_gla, o_pool):
    hn = _rms(h_ref[...], g_ref[...]).astype(BF16)
    o_hn[...] = hn
    o_ret[...] = _dot(hn, w_ret[...])
    o_s5[...] = _dot(hn, w_s5[...])
    o_gla[...] = _dot(hn, w_gla[...])
    o_pool[...] = _dot(hn, w_pool[...])


def _proj(h, g, ws):
    R, D = h.shape
    tm = min(R, ROW_TILE)
    widths = [w.shape[1] for w in ws]
    return pl.pallas_call(
        _proj_kernel,
        grid=(R // tm,),
        in_specs=[pl.BlockSpec((tm, D), lambda i: (i, 0)), _const_spec((1, D))]
        + [_const_spec(w.shape) for w in ws],
        out_specs=[pl.BlockSpec((tm, n), lambda i: (i, 0)) for n in [D] + widths],
        out_shape=[jax.ShapeDtypeStruct((R, D), BF16)]
        + [jax.ShapeDtypeStruct((R, n), F32) for n in widths],
        compiler_params=_cparams(("parallel",)),
        name="mixer_proj",
    )(h, g.reshape(1, D), *ws)


def _ret_consts(heads, dk, dv):
    log_g = np.log(1.0 - 2.0 ** (-5.0 - np.arange(heads, dtype=np.float64)))
    idx = np.arange(CHUNK, dtype=np.float64)
    rel = idx[:, None] - idx[None, :]
    dmask = np.where(rel >= 0, np.exp(np.maximum(rel, 0.0)[None] * log_g[:, None, None]), 0.0)
    dmask = np.concatenate(list(dmask), axis=1)
    lane_g = np.repeat(log_g, dk)[None, :]
    qdec = np.exp((idx[:, None] + 1.0) * lane_g)
    kdec = np.exp((CHUNK - 1.0 - idx)[:, None] * lane_g)
    sdec = np.exp(CHUNK * np.repeat(log_g, dk))[:, None] * np.ones((1, heads * dv))
    bmask = np.kron(np.eye(heads), np.ones((dk, dv)))
    f = lambda a: jnp.asarray(a, F32)
    return f(dmask), f(qdec), f(kdec), f(sdec), f(bmask)


def _rope_kernel(ang_ref, cos_ref, ssin_ref, *, half):
    ang = ang_ref[...]
    lane = lax.broadcasted_iota(jnp.int32, ang.shape, 1)
    cos_ref[...] = jnp.cos(ang)
    s = jnp.sin(ang)
    ssin_ref[...] = jnp.where((lane % (2 * half)) < half, -s, s)


def _rope_tables(ang, half):
    R, W = ang.shape
    tm = min(R, 2048)
    spec = pl.BlockSpec((tm, W), lambda i: (i, 0))
    return pl.pallas_call(
        functools.partial(_rope_kernel, half=half),
        grid=(R // tm,),
        in_specs=[spec],
        out_specs=[spec, spec],
        out_shape=[jax.ShapeDtypeStruct((R, W), F32)] * 2,
        compiler_params=_cparams(("parallel",)),
        name="rope_tables",
    )(ang)


def _ret_kernel(x_ref, cos_ref, ssin_ref, dmask_ref, qdec_ref, kdec_ref, sdec_ref, bmask_ref,
                o_ref, state_ref, *, heads, dk, dv, n_chunks):
    hk = heads * dk
    hv = heads * dv

    @pl.when(pl.program_id(1) == 0)
    def _():
        state_ref[...] = jnp.zeros_like(state_ref)

    lane = lax.broadcasted_iota(jnp.int32, (CHUNK, hk), 1)
    first_half = (lane % dk) < (dk // 2)

    def rotary(x, cos2, ssin):
        swapped = jnp.where(first_half, pltpu.roll(x, hk - dk // 2, 1), pltpu.roll(x, dk // 2, 1))
        return x * cos2 + swapped * ssin

    def chunk(ci, carry):
        r0 = pl.multiple_of(ci * CHUNK, CHUNK)
        rows = pl.ds(r0, CHUNK)
        reps = hk // LANE
        cos2 = jnp.concatenate([cos_ref[rows, :]] * reps, axis=1)
        ssin = jnp.concatenate([ssin_ref[rows, :]] * reps, axis=1)
        q = rotary(x_ref[rows, 0:hk], cos2, ssin)
        k = rotary(x_ref[rows, hk:2 * hk], cos2, ssin) * (dk ** -0.5)
        v = x_ref[rows, 2 * hk:2 * hk + hv]
        gate = x_ref[rows, 2 * hk + hv:2 * hk + 2 * hv]
        vb = v.astype(BF16)
        state = state_ref[...]
        cross = _dot((q * qdec_ref[...]).astype(BF16), state.astype(BF16))
        sb = (_dot(q.astype(BF16), _key_blocks(k, heads, dk)) * dmask_ref[...]).astype(BF16)
        outs = [_dot(sb[:, h * CHUNK:(h + 1) * CHUNK], vb[:, h * dv:(h + 1) * dv]) for h in range(heads)]
        o = jnp.concatenate(outs, axis=1) + cross
        kd = (k * kdec_ref[...]).astype(BF16)
        upd = _dot(kd.T, vb)
        state_ref[...] = sdec_ref[...] * state + bmask_ref[...] * upd
        y = jnp.concatenate(
            [_head_norm(o[:, h * dv:(h + 1) * dv]) for h in range(heads)], axis=1)
        o_ref[rows, :] = (y * _silu(gate)).astype(o_ref.dtype)
        return carry

    lax.fori_loop(0, n_chunks, chunk, 0)


def _retention(x, rope, B, S, heads, dk, dv):
    tt = min(S, SEQ_TILE)
    nt = S // tt
    win = 2 * heads * dk + 2 * heads * dv
    consts = _ret_consts(heads, dk, dv)
    kern = functools.partial(_ret_kernel, heads=heads, dk=dk, dv=dv, n_chunks=tt // CHUNK)
    return pl.pallas_call(
        kern,
        grid=(B, nt),
        in_specs=[pl.BlockSpec((tt, win), _seq_map(nt)),
                  pl.BlockSpec((tt, LANE), _seq_map(nt)), pl.BlockSpec((tt, LANE), _seq_map(nt))]
        + [_const_spec(c.shape) for c in consts],
        out_specs=pl.BlockSpec((tt, heads * dv), _seq_map(nt)),
        out_shape=jax.ShapeDtypeStruct((B * S, heads * dv), BF16),
        scratch_shapes=[pltpu.VMEM((heads * dk, heads * dv), F32)],
        compiler_params=_cparams(("parallel", "arbitrary")),
        name="retention",
    )(x, *rope, *consts)


def _gla_consts(heads, dk, dv):
    C = CHUNK
    n_lvl = int(math.log2(C))
    t = np.arange(C)
    sums = []
    masks = []
    for l in range(n_lvl):
        bit = (t >> l) & 1
        start = (t >> l) << l
        r = t[None, :]
        upper = (bit[:, None] == 1) & (r >= start[:, None]) & (r <= t[:, None])
        lower = (bit[:, None] == 0) & (r > t[:, None]) & (r < (start + (1 << l))[:, None])
        sums.append((upper | lower).astype(np.float64))
        m = ((bit[:, None] == 1) & (bit[None, :] == 0)
             & ((t[:, None] >> (l + 1)) == (t[None, :] >> (l + 1))))
        masks.append(np.tile(m.astype(np.float64), (1, heads)))
    masks.append(np.tile(np.eye(C), (1, heads)))
    sums.append((t[None, :] <= t[:, None]).astype(np.float64))
    sums.append((t[None, :] > t[:, None]).astype(np.float64))
    g = np.concatenate(sums, axis=0)
    gcat = np.concatenate([g, g], axis=1)
    bmask = np.kron(np.eye(heads), np.ones((dk, dv)))
    return jnp.asarray(gcat, BF16), jnp.asarray(np.stack(masks), F32), jnp.asarray(bmask, F32)


def _log_sigmoid(x):
    return jnp.minimum(x, 0.0) - jnp.log(1.0 + jnp.exp(-jnp.abs(x)))


def _gla_kernel(x_ref, wg_ref, bg_ref, gcat_ref, masks_ref, bmask_ref, o_ref, state_ref,
                *, heads, dk, dv, n_chunks):
    hk = heads * dk
    hv = heads * dv
    C = CHUNK
    n_lvl = int(math.log2(C))

    @pl.when(pl.program_id(1) == 0)
    def _():
        state_ref[...] = jnp.zeros_like(state_ref)

    key_blocks = functools.partial(_key_blocks, heads=heads, dk=dk)

    def chunk(ci, carry):
        r0 = pl.multiple_of(ci * C, C)
        rows = pl.ds(r0, C)
        q = x_ref[rows, 0:hk] * (dk ** -0.5)
        k = x_ref[rows, hk:2 * hk]
        v = x_ref[rows, 2 * hk:2 * hk + hv]
        gate = x_ref[rows, 2 * hk + hv:2 * hk + 2 * hv]
        code = x_ref[rows, 2 * hk + 2 * hv:2 * hk + 2 * hv + LANE]
        vb = v.astype(BF16)
        log_a = _log_sigmoid(_dot(code.astype(BF16), wg_ref[...]) + bg_ref[...]) * (1.0 / GLA_GATE_TEMP)
        hi = log_a.astype(BF16)
        lo = (log_a - hi.astype(F32)).astype(BF16)
        sums = _dot(gcat_ref[...], jnp.concatenate([hi, lo], axis=0))

        scores = masks_ref[n_lvl] * _dot(q.astype(BF16), key_blocks(k))
        for l in range(n_lvl):
            z = jnp.exp(sums[l * C:(l + 1) * C])
            scores = scores + masks_ref[l] * _dot((q * z).astype(BF16), key_blocks(k * z))
        sb = scores.astype(BF16)
        intra = jnp.concatenate(
            [_dot(sb[:, h * C:(h + 1) * C], vb[:, h * dv:(h + 1) * dv]) for h in range(heads)], axis=1)

        state = state_ref[...]
        e_pre = sums[n_lvl * C:(n_lvl + 1) * C]
        e_suf = sums[(n_lvl + 1) * C:(n_lvl + 2) * C]
        cross = _dot((q * jnp.exp(e_pre)).astype(BF16), state.astype(BF16))
        kd = (k * jnp.exp(e_suf)).astype(BF16)
        upd = _dot(kd.T, vb)
        total = jnp.broadcast_to(jnp.exp(e_pre[C - 1:C, :]), (LANE, hk))
        dec = jnp.concatenate([total.T] * (hv // LANE), axis=1)
        state_ref[...] = dec * state + bmask_ref[...] * upd

        o = intra + cross
        y = jnp.concatenate(
            [_head_norm(o[:, h * dv:(h + 1) * dv]) for h in range(heads)], axis=1)
        o_ref[rows, :] = (y * _silu(gate)).astype(o_ref.dtype)
        return carry

    lax.fori_loop(0, n_chunks, chunk, 0)


def _gla(x, wg, bg, B, S, heads, dk, dv):
    tt = min(S, SEQ_TILE)
    nt = S // tt
    win = 2 * heads * dk + 2 * heads * dv + LANE
    consts = _gla_consts(heads, dk, dv)
    kern = functools.partial(_gla_kernel, heads=heads, dk=dk, dv=dv, n_chunks=tt // CHUNK)
    return pl.pallas_call(
        kern,
        grid=(B, nt),
        in_specs=[pl.BlockSpec((tt, win), _seq_map(nt)),
                  _const_spec(wg.shape), _const_spec(bg.shape)]
        + [_const_spec(c.shape) for c in consts],
        out_specs=pl.BlockSpec((tt, heads * dv), _seq_map(nt)),
        out_shape=jax.ShapeDtypeStruct((B * S, heads * dv), BF16),
        scratch_shapes=[pltpu.VMEM((heads * dk, heads * dv), F32)],
        compiler_params=_cparams(("parallel", "arbitrary")),
        name="gla",
    )(x, wg, bg, *consts)


def _s5_kernel(u_ref, a_ref, bblk_ref, cblk_ref, d_ref, wglu_ref, o_ref,
               stage_ref, utb_ref, xs_ref, ytb_ref, st_ref, *, tt):
    nb, w = u_ref.shape[0], u_ref.shape[2]
    nblk = w // LANE
    sw = a_ref.shape[1] // (2 * nblk)

    @pl.when(pl.program_id(0) == 0)
    def _():
        st_ref[...] = jnp.zeros_like(st_ref)

    for b in range(nb):
        for m in range(nblk):
            stage_ref[m, b * S5_PITCH:b * S5_PITCH + tt, :] = u_ref[b, :, m * LANE:(m + 1) * LANE]

    def gather(t, carry):
        rows = pl.ds(pl.multiple_of(t * nb, nb), nb)
        for m in range(nblk):
            utb_ref[rows, m * LANE:(m + 1) * LANE] = stage_ref[m, pl.ds(t, nb, stride=S5_PITCH), :]
        return carry

    lax.fori_loop(0, tt, gather, 0)

    u = utb_ref[...]
    ub = u.astype(BF16)
    for m in range(nblk):
        xs_ref[:, 2 * sw * m:2 * sw * (m + 1)] = _dot(ub[:, m * LANE:(m + 1) * LANE], bblk_ref[m])

    def step(t, carry):
        rows = pl.ds(pl.multiple_of(t * nb, nb), nb)
        new = []
        for m in range(nblk):
            xr, xi = carry[2 * m], carry[2 * m + 1]
            re = slice(2 * sw * m, 2 * sw * m + sw)
            im = slice(2 * sw * m + sw, 2 * sw * (m + 1))
            ar, ai = a_ref[:, re], a_ref[:, im]
            nr = ar * xr - ai * xi + xs_ref[rows, re]
            ni = ar * xi + ai * xr + xs_ref[rows, im]
            xs_ref[rows, re] = nr
            xs_ref[rows, im] = ni
            new += [nr, ni]
        return tuple(new)

    init = tuple(st_ref[:, sw * c:sw * (c + 1)] for c in range(2 * nblk))
    last = lax.fori_loop(0, tt, step, init, unroll=2)
    for c in range(2 * nblk):
        st_ref[:, sw * c:sw * (c + 1)] = last[c]

    ys = [_dot(xs_ref[:, 2 * sw * m:2 * sw * (m + 1)].astype(BF16), cblk_ref[m]) for m in range(nblk)]
    y = jnp.concatenate(ys, axis=1) + d_ref[...] * u
    y = jax.nn.gelu(y)
    y = y * jax.nn.sigmoid(_dot(y.astype(BF16), wglu_ref[...]))

    for m in range(nblk):
        ytb_ref[m] = y[:, m * LANE:(m + 1) * LANE]
    for b in range(nb):
        for m in range(nblk):
            o_ref[b, :, m * LANE:(m + 1) * LANE] = (
                ytb_ref[m, pl.ds(b, tt, stride=nb), :].astype(o_ref.dtype))


def _s5(u3, a_row, bblk, cblk, d_row, wglu):
    B, S, W = u3.shape
    tt = min(S, S5_TILE)
    ns2 = a_row.shape[1]
    nblk = W // LANE
    a8 = jnp.broadcast_to(a_row, (B, ns2))
    return pl.pallas_call(
        functools.partial(_s5_kernel, tt=tt),
        grid=(S // tt,),
        in_specs=[pl.BlockSpec((B, tt, W), lambda t: (0, t, 0)),
                  _const_spec(a8.shape), _const_spec(bblk.shape), _const_spec(cblk.shape),
                  _const_spec(d_row.shape), _const_spec(wglu.shape)],
        out_specs=pl.BlockSpec((B, tt, W), lambda t: (0, t, 0)),
        out_shape=jax.ShapeDtypeStruct((B, S, W), BF16),
        scratch_shapes=[pltpu.VMEM((nblk, B * S5_PITCH, LANE), F32),
                        pltpu.VMEM((tt * B, W), F32),
                        pltpu.VMEM((tt * B, ns2), F32),
                        pltpu.VMEM((nblk, tt * B, LANE), F32),
                        pltpu.VMEM((B, ns2), F32)],
        compiler_params=_cparams(("arbitrary",)),
        name="s5",
    )(u3, a8, bblk, cblk, d_row, wglu)


def _s5_params(a_re, a_im, log_dt, b_re, b_im, c_re, c_im):
    G, P = a_re.shape
    n_c = b_re.shape[-1]
    gpb = LANE // n_c
    nblk = G // gpb
    dt = jnp.exp(log_dt)[:, None]
    mag = jnp.exp(a_re * dt)
    abar_re = mag * jnp.cos(a_im * dt)
    abar_im = mag * jnp.sin(a_im * dt)
    den = a_re * a_re + a_im * a_im
    nr, ni = abar_re - 1.0, abar_im
    f_re = (nr * a_re + ni * a_im) / den
    f_im = (ni * a_re - nr * a_im) / den
    bb_re = f_re[..., None] * b_re - f_im[..., None] * b_im
    bb_im = f_re[..., None] * b_im + f_im[..., None] * b_re
    eye = jnp.eye(gpb, dtype=F32)

    def bd_in(m):
        m = m.reshape(nblk, gpb, P, n_c)
        return jnp.einsum('ngpc,gh->ngchp', m, eye).reshape(nblk, gpb * n_c, gpb * P)

    def bd_out(m):
        m = m.reshape(nblk, gpb, n_c, P)
        return jnp.einsum('ngcp,gh->ngphc', m, eye).reshape(nblk, gpb * P, gpb * n_c)

    a_row = jnp.concatenate([abar_re.reshape(nblk, 1, gpb * P), abar_im.reshape(nblk, 1, gpb * P)],
                            axis=2).reshape(1, 2 * G * P)
    bblk = jnp.concatenate([bd_in(bb_re), bd_in(bb_im)], axis=2).astype(BF16)
    cblk = jnp.concatenate([bd_out(c_re), -bd_out(c_im)], axis=1).astype(BF16)
    return a_row, bblk, cblk


def _pool_kernel(p_ref, w_ref, sc_ref, o_ref, ext_ref, *, tt):
    pad = POOL_PAD
    t_idx = pl.program_id(1)

    @pl.when(t_idx == 0)
    def _():
        ext_ref[0:pad, :] = jnp.zeros((pad, ext_ref.shape[1]), F32)

    ext_ref[pad:pad + tt, :] = p_ref[...]
    pos1 = (lax.broadcasted_iota(jnp.int32, (tt, LANE), 0) + t_idx * tt + 1).astype(F32)
    outs = []
    for g, win in enumerate(POOL_WINDOWS):
        cols = slice(g * LANE, (g + 1) * LANE)
        acc = ext_ref[pad:pad + tt, cols]
        for j in range(1, win):
            acc = acc + ext_ref[pad - j:pad - j + tt, cols]
        cur = ext_ref[pad:pad + tt, cols]
        mixed = acc / jnp.minimum(pos1, float(win)) - cur
        outs.append(_dot(mixed.astype(BF16), w_ref[g]))
    o_ref[...] = (jnp.concatenate(outs, axis=1) * sc_ref[...]).astype(o_ref.dtype)
    ext_ref[0:pad, :] = ext_ref[tt:tt + pad, :]


def _pool(p, w, scale, B, S):
    tt = min(S, SEQ_TILE)
    nt = S // tt
    W = w.shape[0] * LANE
    return pl.pallas_call(
        functools.partial(_pool_kernel, tt=tt),
        grid=(B, nt),
        in_specs=[pl.BlockSpec((tt, W), _seq_map(nt)),
                  _const_spec(w.shape), _const_spec(scale.shape)],
        out_specs=pl.BlockSpec((tt, W), _seq_map(nt)),
        out_shape=jax.ShapeDtypeStruct((B * S, W), BF16),
        scratch_shapes=[pltpu.VMEM((tt + POOL_PAD, W), F32)],
        compiler_params=_cparams(("parallel", "arbitrary")),
        name="pool",
    )(p, w, scale)


def _merge_kernel(hn_ref, *refs):
    ys = refs[0:N_BRANCH]
    wgates = refs[N_BRANCH:2 * N_BRANCH]
    wbrs = refs[2 * N_BRANCH:3 * N_BRANCH]
    o_ref = refs[3 * N_BRANCH]
    tm = o_ref.shape[0]
    for rb in range(tm // MERGE_SUB):
        rows = slice(rb * MERGE_SUB, (rb + 1) * MERGE_SUB)
        hn = hn_ref[rows, :]
        acc = None
        for n in range(N_BRANCH):
            logits = _dot(hn, wgates[n][...])
            term = jax.nn.sigmoid(logits) * _dot(ys[n][rows, :], wbrs[n][...])
            acc = term if acc is None else acc + term
        o_ref[rows, :] = acc.astype(o_ref.dtype)


def _merge(hn, ys, wgate, wbr):
    R, D = hn.shape
    tm = min(R, MERGE_ROW_TILE)
    bw = ys[0].shape[1]
    tn = MERGE_TILE
    nc = D // tn
    gate_spec = lambda n: pl.BlockSpec((D, tn), lambda i, c: (0, n * nc + c))
    br_spec = lambda n: pl.BlockSpec((None, bw, tn), lambda i, c: (n, 0, c))
    return pl.pallas_call(
        _merge_kernel,
        grid=(R // tm, nc),
        in_specs=[pl.BlockSpec((tm, D), lambda i, c: (i, 0))]
        + [pl.BlockSpec((tm, bw), lambda i, c: (i, 0)) for _ in ys]
        + [gate_spec(n) for n in range(N_BRANCH)] + [br_spec(n) for n in range(N_BRANCH)],
        out_specs=pl.BlockSpec((tm, tn), lambda i, c: (i, c)),
        out_shape=jax.ShapeDtypeStruct((R, D), BF16),
        compiler_params=_cparams(("parallel", "arbitrary")),
        name="branch_merge",
    )(hn, *ys, *([wgate] * N_BRANCH), *([wbr] * N_BRANCH))


def _outproj_kernel(h_ref, m_ref, w_ref, o_ref):
    o_ref[...] = h_ref[...] + _dot(m_ref[...], w_ref[...])


def _outproj(h, merged, w):
    R, D = h.shape
    tm = min(R, ROW_TILE)
    return pl.pallas_call(
        _outproj_kernel,
        grid=(R // tm,),
        in_specs=[pl.BlockSpec((tm, D), lambda i: (i, 0)),
                  pl.BlockSpec((tm, D), lambda i: (i, 0)),
                  _const_spec(w.shape)],
        out_specs=pl.BlockSpec((tm, D), lambda i: (i, 0)),
        out_shape=jax.ShapeDtypeStruct((R, D), F32),
        compiler_params=_cparams(("parallel",)),
        name="out_proj",
    )(h, merged, w)


def _ffn_kernel(h_ref, g_ref, wa_ref, wv_ref, cw_ref, cb_ref, wd_ref, o_ref,
                hn_ref, act_ref, carry_ref, *, tn, nj, tiles_per_seq, n_steps):
    s = pl.program_id(0)
    su = jnp.minimum(s, n_steps - 1)
    i = su // nj
    j = su % nj
    jd = jnp.maximum(s - 1, 0) % nj
    slot = s % 2
    tm = hn_ref.shape[0]

    @pl.when(s == 0)
    def _():
        act_ref[...] = jnp.zeros_like(act_ref)
        o_ref[...] = jnp.zeros_like(o_ref)

    @pl.when(j == 0)
    def _():
        hn_ref[...] = _rms(h_ref[...], g_ref[...]).astype(BF16)

    @pl.when(jnp.logical_and(jd == 0, s > 0))
    def _():
        o_ref[...] = h_ref[...]

    @pl.when(i % tiles_per_seq == 0)
    def _():
        carry_ref[j] = jnp.zeros((SUBLANE, 2 * tn), F32)

    row8 = lax.broadcasted_iota(jnp.int32, (SUBLANE, tn), 0)

    def delayed(u, prev8, k):
        r = pltpu.roll(u, k, 0)
        top = jnp.where(row8 < k, pltpu.roll(prev8, k, 0), r[0:SUBLANE])
        return jnp.concatenate([top, r[SUBLANE:]], axis=0)

    def conv(u, prev8, col0):
        out = cb_ref[:, col0:col0 + tn] + u * cw_ref[CONV_W - 1:CONV_W, col0:col0 + tn]
        for tap in range(CONV_W - 1):
            out = out + delayed(u, prev8, CONV_W - 1 - tap) * cw_ref[tap:tap + 1, col0:col0 + tn]
        return out

    n_sub = tm // FF_SUB
    ups = []
    for rb in range(n_sub):
        hn = hn_ref[rb * FF_SUB:(rb + 1) * FF_SUB, :]
        ups.append((_dot(hn, wa_ref[...]), _dot(hn, wv_ref[...])))
    for rb in range(n_sub):
        rows = slice(rb * FF_SUB, (rb + 1) * FF_SUB)
        o_ref[rows, :] += _dot(act_ref[1 - slot, rows, :], wd_ref[...])

    prev = carry_ref[j]
    prev_a, prev_v = prev[:, 0:tn], prev[:, tn:2 * tn]
    for rb in range(n_sub):
        ua, uv = ups[rb]
        a = conv(ua, prev_a, 0)
        v = conv(uv, prev_v, tn)
        prev_a, prev_v = ua[FF_SUB - SUBLANE:], uv[FF_SUB - SUBLANE:]
        act_ref[slot, rb * FF_SUB:(rb + 1) * FF_SUB, :] = (_silu(a) * v).astype(BF16)
    carry_ref[j] = jnp.concatenate([prev_a, prev_v], axis=1)


def _ffn(h, g, wa, wv, cw, cb, wd, S):
    R, D = h.shape
    tm = min(S, FF_ROW_TILE)
    tn = FF_TILE
    nj = wa.shape[1] // tn
    assert nj >= 2
    n_steps = (R // tm) * nj
    up_step = lambda s: jnp.minimum(s, n_steps - 1)
    down_step = lambda s: jnp.maximum(s - 1, 0)
    return pl.pallas_call(
        functools.partial(_ffn_kernel, tn=tn, nj=nj, tiles_per_seq=S // tm, n_steps=n_steps),
        grid=(n_steps + 1,),
        in_specs=[pl.BlockSpec((tm, D), lambda s: (up_step(s) // nj, 0)),
                  pl.BlockSpec((1, D), lambda s: (0, 0)),
                  pl.BlockSpec((D, tn), lambda s: (0, up_step(s) % nj)),
                  pl.BlockSpec((D, tn), lambda s: (0, up_step(s) % nj)),
                  pl.BlockSpec((None, CONV_W, 2 * tn), lambda s: (up_step(s) % nj, 0, 0)),
                  pl.BlockSpec((None, 1, 2 * tn), lambda s: (up_step(s) % nj, 0, 0)),
                  pl.BlockSpec((tn, D), lambda s: (down_step(s) % nj, 0))],
        out_specs=pl.BlockSpec((tm, D), lambda s: (down_step(s) // nj, 0)),
        out_shape=jax.ShapeDtypeStruct((R, D), F32),
        scratch_shapes=[pltpu.VMEM((tm, D), BF16),
                        pltpu.VMEM((2, tm, tn), BF16),
                        pltpu.VMEM((nj, SUBLANE, 2 * tn), F32)],
        compiler_params=_cparams(("arbitrary",)),
        name="conv_ffn",
    )(h, g.reshape(1, D), wa, wv, cw, cb, wd)


def _pad_cols(w, n):
    return jnp.pad(w, ((0, 0), (0, n - w.shape[1])))


def _split_w_in(w_in, D):
    bw = D // 4
    hk = bw // 2
    sizes = [hk, hk, bw, bw, bw, hk, hk, bw, bw, GLA_RANK, bw, N_BRANCH * D]
    offs = np.concatenate([[0], np.cumsum(sizes)])
    seg = lambda a, b: w_in[:, offs[a]:offs[b]]
    w_ret = seg(0, 4)
    w_s5 = seg(4, 5)
    w_gla = jnp.concatenate([seg(5, 9), _pad_cols(seg(9, 10), LANE)], axis=1)
    w_pool = seg(10, 11)
    return [w.astype(BF16) for w in (w_ret, w_s5, w_gla, w_pool)], seg(11, 12).astype(BF16)


def _ffn_weights(w_up, conv_w, conv_b, w_down, tn):
    D, ff2 = w_up.shape
    ff = ff2 // 2
    ffp = -(-ff // tn) * tn
    nj = ffp // tn
    wa = _pad_cols(w_up[:, :ff], ffp).astype(BF16)
    wv = _pad_cols(w_up[:, ff:], ffp).astype(BF16)
    tiles = lambda m: _pad_cols(m, ffp).reshape(m.shape[0], nj, tn)
    cw = jnp.concatenate([tiles(conv_w[:, :ff]), tiles(conv_w[:, ff:])], axis=2).transpose(1, 0, 2)
    cbr = conv_b.reshape(1, ff2)
    cb = jnp.concatenate([tiles(cbr[:, :ff]), tiles(cbr[:, ff:])], axis=2).transpose(1, 0, 2)
    wd = jnp.pad(w_down, ((0, ffp - ff), (0, 0))).astype(BF16)
    return wa, wv, cw, cb, wd


def kernel(x, positions, norm_mix_g, w_in, s5_a_re, s5_a_im, s5_log_dt, s5_b_re, s5_b_im, s5_c_re, s5_c_im, s5_d, s5_w_glu, gla_w_gate, gla_b_gate, pool_w, pool_scale, w_branch, w_out, norm_ffn_g, w_up, conv_w, conv_b, w_down, final_g):
    B, S, D = x.shape
    depth = w_in.shape[0]
    bw = D // 4
    ret_dk, ret_dv = bw // (2 * RET_HEADS), bw // RET_HEADS
    gla_dk, gla_dv = bw // (2 * GLA_HEADS), bw // GLA_HEADS
    assert S % CHUNK == 0 and S % min(S, ROW_TILE) == 0 and B == SUBLANE
    assert ret_dk // 2 * 4 == LANE and bw == S5_BLOCK * LANE

    inv = ROPE_BASE ** (-jnp.arange(0, ret_dk, 2, dtype=F32) / ret_dk)
    ang = positions.astype(F32)[..., None] * inv
    ang = jnp.tile(ang, (1, 1, LANE // (ret_dk // 2))).reshape(B * S, LANE)
    rope = _rope_tables(ang, ret_dk // 2)

    h = x.reshape(B * S, D)
    for l in range(depth):
        ws, w_gate = _split_w_in(w_in[l], D)
        hn, p_ret, p_s5, p_gla, p_pool = _proj(h, norm_mix_g[l], ws)
        ya = _retention(p_ret, rope, B, S, RET_HEADS, ret_dk, ret_dv)
        a_row, bblk, cblk = _s5_params(s5_a_re[l], s5_a_im[l], s5_log_dt[l], s5_b_re[l],
                                       s5_b_im[l], s5_c_re[l], s5_c_im[l])
        yb = _s5(p_s5.reshape(B, S, bw), a_row, bblk, cblk, s5_d[l].reshape(1, bw),
                 s5_w_glu[l].astype(BF16)).reshape(B * S, bw)
        wg = jnp.pad(gla_w_gate[l], ((0, LANE - GLA_RANK), (0, 0))).astype(BF16)
        yc = _gla(p_gla, wg, gla_b_gate[l].reshape(1, -1), B, S, GLA_HEADS, gla_dk, gla_dv)
        yd = _pool(p_pool, pool_w[l].astype(BF16), pool_scale[l].reshape(1, bw), B, S)
        merged = _merge(hn, [ya, yb, yc, yd], w_gate, w_branch[l].astype(BF16))
        h = _outproj(h, merged, w_out[l].astype(BF16))
        ffw = _ffn_weights(w_up[l], conv_w[l], conv_b[l], w_down[l], FF_TILE)
        h = _ffn(h, norm_ffn_g[l], *ffw, S=S)
    return _final_norm(h, final_g).reshape(B, S, D)
```

```python
import functools
import math

import numpy as np
import jax
import jax.numpy as jnp
from jax import lax
from jax.experimental import pallas as pl
from jax.experimental.pallas import tpu as pltpu

F32 = jnp.float32
BF16 = jnp.bfloat16
EPS = 1e-6

N_BRANCH = 4
RET_HEADS = 4
GLA_HEADS = 4
GLA_RANK = 16
GLA_GATE_TEMP = 16.0
S5_GROUP = 16
S5_STATE = 64
POOL_WINDOWS = (2, 4, 8, 16)
POOL_PAD = 16
ROPE_BASE = 10000.0
CONV_W = 3

LANE = 128
SUBLANE = 8
CHUNK = 128
VMEM_LIMIT = 56 * 1024 * 1024
ROW_TILE = 512
MERGE_ROW_TILE = 1024
MERGE_TILE = 512
MERGE_SUB = 256
OUT_SUB = 256
FF_ROW_TILE = 1024
FF_TILE = 512
FF_SUB = 256
SEQ_TILE = 512
S5_TILE = 64
S5_PITCH = S5_TILE + 4
S5_BLOCK = 4


def _cparams(sem):
    return pltpu.CompilerParams(dimension_semantics=sem, vmem_limit_bytes=VMEM_LIMIT)


def _const_spec(shape):
    nd = len(shape)
    return pl.BlockSpec(shape, lambda *_: (0,) * nd, pipeline_mode=pl.Buffered(1))


def _dot(a, b):
    return jnp.dot(a, b, preferred_element_type=F32)


def _dot_nt(a, b):
    return lax.dot_general(a, b, (((1,), (1,)), ((), ())), preferred_element_type=F32)


def _rms(x, g):
    r = lax.rsqrt(jnp.mean(x * x, axis=-1, keepdims=True) + EPS)
    return x * r * g


def _head_norm(o):
    mu = jnp.mean(o, axis=-1, keepdims=True)
    d = o - mu
    var = jnp.mean(d * d, axis=-1, keepdims=True)
    return d * lax.rsqrt(var + EPS)


def _silu(x):
    return x * jax.nn.sigmoid(x)


def _key_blocks(k, heads, dk):
    kt = k.T.astype(BF16)
    zero = jnp.zeros((dk, kt.shape[1]), BF16)
    cols = [jnp.concatenate([kt[h * dk:(h + 1) * dk] if r == h else zero for r in range(heads)], axis=0)
            for h in range(heads)]
    return jnp.concatenate(cols, axis=1)


def _seq_map(nt):
    return lambda b, t: (b * nt + t, 0)


def _proj_kernel(h_ref, g_ref, w_ret, w_s5, w_gla, w_pool, o_hn, o_ret, o_s5, o_gla, o_pool):
    hn = _rms(h_ref[...], g_ref[...]).astype(BF16)
    o_hn[...] = hn
    o_ret[...] = _dot(hn, w_ret[...])
    o_s5[...] = _dot(hn, w_s5[...])
    o_gla[...] = _dot(hn, w_gla[...])
    o_pool[...] = _dot(hn, w_pool[...])


def _proj(h, g, ws):
    R, D = h.shape
    tm = min(R, ROW_TILE)
    widths = [w.shape[1] for w in ws]
    return pl.pallas_call(
        _proj_kernel,
        grid=(R // tm,),
        in_specs=[pl.BlockSpec((tm, D), lambda i: (i, 0)), _const_spec((1, D))]
        + [_const_spec(w.shape) for w in ws],
        out_specs=[pl.BlockSpec((tm, n), lambda i: (i, 0)) for n in [D] + widths],
        out_shape=[jax.ShapeDtypeStruct((R, D), BF16)]
        + [jax.ShapeDtypeStruct((R, n), F32) for n in widths],
        compiler_params=_cparams(("parallel",)),
        name="mixer_proj",
    )(h, g.reshape(1, D), *ws)


def _ret_consts(heads, dk, dv):
    log_g = np.log(1.0 - 2.0 ** (-5.0 - np.arange(heads, dtype=np.float64)))
    idx = np.arange(CHUNK, dtype=np.float64)
    rel = idx[:, None] - idx[None, :]
    dmask = np.where(rel >= 0, np.exp(np.maximum(rel, 0.0)[None] * log_g[:, None, None]), 0.0)
    dmask = np.concatenate(list(dmask), axis=1)
    lane_g = np.repeat(log_g, dk)[None, :]
    qdec = np.exp((idx[:, None] + 1.0) * lane_g)
    kdec = np.exp((CHUNK - 1.0 - idx)[:, None] * lane_g)
    sdec = np.exp(CHUNK * np.repeat(log_g, dk))[:, None] * np.ones((1, heads * dv))
    bmask = np.kron(np.eye(heads), np.ones((dk, dv)))
    f = lambda a: jnp.asarray(a, F32)
    return f(dmask), f(qdec), f(kdec), f(sdec), f(bmask)


def _rope_kernel(ang_ref, cos_ref, ssin_ref, *, half):
    ang = ang_ref[...]
    lane = lax.broadcasted_iota(jnp.int32, ang.shape, 1)
    cos_ref[...] = jnp.cos(ang)
    s = jnp.sin(ang)
    ssin_ref[...] = jnp.where((lane % (2 * half)) < half, -s, s)


def _rope_tables(ang, half):
    R, W = ang.shape
    tm = min(R, 2048)
    spec = pl.BlockSpec((tm, W), lambda i: (i, 0))
    return pl.pallas_call(
        functools.partial(_rope_kernel, half=half),
        grid=(R // tm,),
        in_specs=[spec],
        out_specs=[spec, spec],
        out_shape=[jax.ShapeDtypeStruct((R, W), F32)] * 2,
        compiler_params=_cparams(("parallel",)),
        name="rope_tables",
    )(ang)


def _ret_kernel(x_ref, cos_ref, ssin_ref, dmask_ref, qdec_ref, kdec_ref, sdec_ref, bmask_ref,
                o_ref, state_ref, *, heads, dk, dv, n_chunks):
    hk = heads * dk
    hv = heads * dv

    @pl.when(pl.program_id(1) == 0)
    def _():
        state_ref[...] = jnp.zeros_like(state_ref)

    lane = lax.broadcasted_iota(jnp.int32, (CHUNK, hk), 1)
    first_half = (lane % dk) < (dk // 2)

    def rotary(x, cos2, ssin):
        swapped = jnp.where(first_half, pltpu.roll(x, hk - dk // 2, 1), pltpu.roll(x, dk // 2, 1))
        return x * cos2 + swapped * ssin

    def chunk(ci, carry):
        r0 = pl.multiple_of(ci * CHUNK, CHUNK)
        rows = pl.ds(r0, CHUNK)
        reps = hk // LANE
        cos2 = jnp.concatenate([cos_ref[rows, :]] * reps, axis=1)
        ssin = jnp.concatenate([ssin_ref[rows, :]] * reps, axis=1)
        q = rotary(x_ref[rows, 0:hk], cos2, ssin)
        k = rotary(x_ref[rows, hk:2 * hk], cos2, ssin) * (dk ** -0.5)
        v = x_ref[rows, 2 * hk:2 * hk + hv]
        gate = x_ref[rows, 2 * hk + hv:2 * hk + 2 * hv]
        vb = v.astype(BF16)
        state = state_ref[...]
        cross = _dot((q * qdec_ref[...]).astype(BF16), state.astype(BF16))
        sb = (_dot(q.astype(BF16), _key_blocks(k, heads, dk)) * dmask_ref[...]).astype(BF16)
        outs = [_dot(sb[:, h * CHUNK:(h + 1) * CHUNK], vb[:, h * dv:(h + 1) * dv]) for h in range(heads)]
        o = jnp.concatenate(outs, axis=1) + cross
        kd = (k * kdec_ref[...]).astype(BF16)
        upd = _dot(kd.T, vb)
        state_ref[...] = sdec_ref[...] * state + bmask_ref[...] * upd
        y = jnp.concatenate(
            [_head_norm(o[:, h * dv:(h + 1) * dv]) for h in range(heads)], axis=1)
        o_ref[rows, :] = (y * _silu(gate)).astype(o_ref.dtype)
        return carry

    lax.fori_loop(0, n_chunks, chunk, 0)


def _retention(x, rope, B, S, heads, dk, dv):
    tt = min(S, SEQ_TILE)
    nt = S // tt
    win = 2 * heads * dk + 2 * heads * dv
    consts = _ret_consts(heads, dk, dv)
    kern = functools.partial(_ret_kernel, heads=heads, dk=dk, dv=dv, n_chunks=tt // CHUNK)
    return pl.pallas_call(
        kern,
        grid=(B, nt),
        in_specs=[pl.BlockSpec((tt, win), _seq_map(nt)),
                  pl.BlockSpec((tt, LANE), _seq_map(nt)), pl.BlockSpec((tt, LANE), _seq_map(nt))]
        + [_const_spec(c.shape) for c in consts],
        out_specs=pl.BlockSpec((tt, heads * dv), _seq_map(nt)),
        out_shape=jax.ShapeDtypeStruct((B * S, heads * dv), BF16),
        scratch_shapes=[pltpu.VMEM((heads * dk, heads * dv), F32)],
        compiler_params=_cparams(("parallel", "arbitrary")),
        name="retention",
    )(x, *rope, *consts)


def _gla_consts(heads, dk, dv):
    C = CHUNK
    n_lvl = int(math.log2(C))
    t = np.arange(C)
    sums = []
    masks = []
    for l in range(n_lvl):
        bit = (t >> l) & 1
        start = (t >> l) << l
        r = t[None, :]
        upper = (bit[:, None] == 1) & (r >= start[:, None]) & (r <= t[:, None])
        lower = (bit[:, None] == 0) & (r > t[:, None]) & (r < (start + (1 << l))[:, None])
        sums.append((upper | lower).astype(np.float64))
        m = ((bit[:, None] == 1) & (bit[None, :] == 0)
             & ((t[:, None] >> (l + 1)) == (t[None, :] >> (l + 1))))
        masks.append(np.tile(m.astype(np.float64), (1, heads)))
    masks.append(np.tile(np.eye(C), (1, heads)))
    sums.append((t[None, :] <= t[:, None]).astype(np.float64))
    sums.append((t[None, :] > t[:, None]).astype(np.float64))
    g = np.concatenate(sums, axis=0)
    gcat = np.concatenate([g, g], axis=1)
    bmask = np.kron(np.eye(heads), np.ones((dk, dv)))
    return jnp.asarray(gcat, BF16), jnp.asarray(np.stack(masks), F32), jnp.asarray(bmask, F32)


def _log_sigmoid(x):
    return jnp.minimum(x, 0.0) - jnp.log(1.0 + jnp.exp(-jnp.abs(x)))


def _gla_kernel(x_ref, wg_ref, bg_ref, gcat_ref, masks_ref, bmask_ref, o_ref, state_ref,
                *, heads, dk, dv, n_chunks):
    hk = heads * dk
    hv = heads * dv
    C = CHUNK
    n_lvl = int(math.log2(C))

    @pl.when(pl.program_id(1) == 0)
    def _():
        state_ref[...] = jnp.zeros_like(state_ref)

    key_blocks = functools.partial(_key_blocks, heads=heads, dk=dk)

    def chunk(ci, carry):
        r0 = pl.multiple_of(ci * C, C)
        rows = pl.ds(r0, C)
        q = x_ref[rows, 0:hk] * (dk ** -0.5)
        k = x_ref[rows, hk:2 * hk]
        v = x_ref[rows, 2 * hk:2 * hk + hv]
        gate = x_ref[rows, 2 * hk + hv:2 * hk + 2 * hv]
        code = x_ref[rows, 2 * hk + 2 * hv:2 * hk + 2 * hv + LANE]
        vb = v.astype(BF16)
        log_a = _log_sigmoid(_dot(code.astype(BF16), wg_ref[...]) + bg_ref[...]) * (1.0 / GLA_GATE_TEMP)
        hi = log_a.astype(BF16)
        lo = (log_a - hi.astype(F32)).astype(BF16)
        sums = _dot(gcat_ref[...], jnp.concatenate([hi, lo], axis=0))

        scores = masks_ref[n_lvl] * _dot(q.astype(BF16), key_blocks(k))
        for l in range(n_lvl):
            z = jnp.exp(sums[l * C:(l + 1) * C])
            scores = scores + masks_ref[l] * _dot((q * z).astype(BF16), key_blocks(k * z))
        sb = scores.astype(BF16)
        intra = jnp.concatenate(
            [_dot(sb[:, h * C:(h + 1) * C], vb[:, h * dv:(h + 1) * dv]) for h in range(heads)], axis=1)

        state = state_ref[...]
        e_pre = sums[n_lvl * C:(n_lvl + 1) * C]
        e_suf = sums[(n_lvl + 1) * C:(n_lvl + 2) * C]
        cross = _dot((q * jnp.exp(e_pre)).astype(BF16), state.astype(BF16))
        kd = (k * jnp.exp(e_suf)).astype(BF16)
        upd = _dot(kd.T, vb)
        total = jnp.broadcast_to(jnp.exp(e_pre[C - 1:C, :]), (LANE, hk))
        dec = jnp.concatenate([total.T] * (hv // LANE), axis=1)
        state_ref[...] = dec * state + bmask_ref[...] * upd

        o = intra + cross
        y = jnp.concatenate(
            [_head_norm(o[:, h * dv:(h + 1) * dv]) for h in range(heads)], axis=1)
        o_ref[rows, :] = (y * _silu(gate)).astype(o_ref.dtype)
        return carry

    lax.fori_loop(0, n_chunks, chunk, 0)


def _gla(x, wg, bg, B, S, heads, dk, dv):
    tt = min(S, SEQ_TILE)
    nt = S // tt
    win = 2 * heads * dk + 2 * heads * dv + LANE
    consts = _gla_consts(heads, dk, dv)
    kern = functools.partial(_gla_kernel, heads=heads, dk=dk, dv=dv, n_chunks=tt // CHUNK)
    return pl.pallas_call(
        kern,
        grid=(B, nt),
        in_specs=[pl.BlockSpec((tt, win), _seq_map(nt)),
                  _const_spec(wg.shape), _const_spec(bg.shape)]
        + [_const_spec(c.shape) for c in consts],
        out_specs=pl.BlockSpec((tt, heads * dv), _seq_map(nt)),
        out_shape=jax.ShapeDtypeStruct((B * S, heads * dv), BF16),
        scratch_shapes=[pltpu.VMEM((heads * dk, heads * dv), F32)],
        compiler_params=_cparams(("parallel", "arbitrary")),
        name="gla",
    )(x, wg, bg, *consts)


def _s5_kernel(u_ref, a_ref, bblk_ref, cblk_ref, d_ref, wglu_ref, o_ref,
               stage_ref, utb_ref, xs_ref, ytb_ref, st_ref, *, tt):
    nb, w = u_ref.shape[0], u_ref.shape[2]
    nblk = w // LANE
    sw = a_ref.shape[1] // (2 * nblk)

    @pl.when(pl.program_id(0) == 0)
    def _():
        st_ref[...] = jnp.zeros_like(st_ref)

    for b in range(nb):
        for m in range(nblk):
            stage_ref[m, b * S5_PITCH:b * S5_PITCH + tt, :] = u_ref[b, :, m * LANE:(m + 1) * LANE]

    def gather(t, carry):
        rows = pl.ds(pl.multiple_of(t * nb, nb), nb)
        for m in range(nblk):
            utb_ref[rows, m * LANE:(m + 1) * LANE] = stage_ref[m, pl.ds(t, nb, stride=S5_PITCH), :]
        return carry

    lax.fori_loop(0, tt, gather, 0)

    u = utb_ref[...]
    ub = u.astype(BF16)
    for m in range(nblk):
        xs_ref[:, 2 * sw * m:2 * sw * (m + 1)] = _dot(ub[:, m * LANE:(m + 1) * LANE], bblk_ref[m])

    def step(t, carry):
        rows = pl.ds(pl.multiple_of(t * nb, nb), nb)
        new = []
        for m in range(nblk):
            xr, xi = carry[2 * m], carry[2 * m + 1]
            re = slice(2 * sw * m, 2 * sw * m + sw)
            im = slice(2 * sw * m + sw, 2 * sw * (m + 1))
            ar, ai = a_ref[:, re], a_ref[:, im]
            nr = ar * xr - ai * xi + xs_ref[rows, re]
            ni = ar * xi + ai * xr + xs_ref[rows, im]
            xs_ref[rows, re] = nr
            xs_ref[rows, im] = ni
            new += [nr, ni]
        return tuple(new)

    init = tuple(st_ref[:, sw * c:sw * (c + 1)] for c in range(2 * nblk))
    last = lax.fori_loop(0, tt, step, init, unroll=2)
    for c in range(2 * nblk):
        st_ref[:, sw * c:sw * (c + 1)] = last[c]

    ys = [_dot(xs_ref[:, 2 * sw * m:2 * sw * (m + 1)].astype(BF16), cblk_ref[m]) for m in range(nblk)]
    y = jnp.concatenate(ys, axis=1) + d_ref[...] * u
    y = jax.nn.gelu(y)
    y = y * jax.nn.sigmoid(_dot(y.astype(BF16), wglu_ref[...]))

    for m in range(nblk):
        ytb_ref[m] = y[:, m * LANE:(m + 1) * LANE]
    for b in range(nb):
        for m in range(nblk):
            o_ref[b, :, m * LANE:(m + 1) * LANE] = (
                ytb_ref[m, pl.ds(b, tt, stride=nb), :].astype(o_ref.dtype))


def _s5(u3, a_row, bblk, cblk, d_row, wglu):
    B, S, W = u3.shape
    tt = min(S, S5_TILE)
    ns2 = a_row.shape[1]
    nblk = W // LANE
    a8 = jnp.broadcast_to(a_row, (B, ns2))
    return pl.pallas_call(
        functools.partial(_s5_kernel, tt=tt),
        grid=(S // tt,),
        in_specs=[pl.BlockSpec((B, tt, W), lambda t: (0, t, 0)),
                  _const_spec(a8.shape), _const_spec(bblk.shape), _const_spec(cblk.shape),
                  _const_spec(d_row.shape), _const_spec(wglu.shape)],
        out_specs=pl.BlockSpec((B, tt, W), lambda t: (0, t, 0)),
        out_shape=jax.ShapeDtypeStruct((B, S, W), BF16),
        scratch_shapes=[pltpu.VMEM((nblk, B * S5_PITCH, LANE), F32),
                        pltpu.VMEM((tt * B, W), F32),
                        pltpu.VMEM((tt * B, ns2), F32),
                        pltpu.VMEM((nblk, tt * B, LANE), F32),
                        pltpu.VMEM((B, ns2), F32)],
        compiler_params=_cparams(("arbitrary",)),
        name="s5",
    )(u3, a8, bblk, cblk, d_row, wglu)


def _s5_params(a_re, a_im, log_dt, b_re, b_im, c_re, c_im):
    G, P = a_re.shape
    n_c = b_re.shape[-1]
    gpb = LANE // n_c
    nblk = G // gpb
    dt = jnp.exp(log_dt)[:, None]
    mag = jnp.exp(a_re * dt)
    abar_re = mag * jnp.cos(a_im * dt)
    abar_im = mag * jnp.sin(a_im * dt)
    den = a_re * a_re + a_im * a_im
    nr, ni = abar_re - 1.0, abar_im
    f_re = (nr * a_re + ni * a_im) / den
    f_im = (ni * a_re - nr * a_im) / den
    bb_re = f_re[..., None] * b_re - f_im[..., None] * b_im
    bb_im = f_re[..., None] * b_im + f_im[..., None] * b_re
    eye = jnp.eye(gpb, dtype=F32)

    def bd_in(m):
        m = m.reshape(nblk, gpb, P, n_c)
        return jnp.einsum('ngpc,gh->ngchp', m, eye).reshape(nblk, gpb * n_c, gpb * P)

    def bd_out(m):
        m = m.reshape(nblk, gpb, n_c, P)
        return jnp.einsum('ngcp,gh->ngphc', m, eye).reshape(nblk, gpb * P, gpb * n_c)

    a_row = jnp.concatenate([abar_re.reshape(nblk, 1, gpb * P), abar_im.reshape(nblk, 1, gpb * P)],
                            axis=2).reshape(1, 2 * G * P)
    bblk = jnp.concatenate([bd_in(bb_re), bd_in(bb_im)], axis=2).astype(BF16)
    cblk = jnp.concatenate([bd_out(c_re), -bd_out(c_im)], axis=1).astype(BF16)
    return a_row, bblk, cblk


def _pool_kernel(p_ref, w_ref, sc_ref, o_ref, ext_ref, *, tt):
    pad = POOL_PAD
    t_idx = pl.program_id(1)

    @pl.when(t_idx == 0)
    def _():
        ext_ref[0:pad, :] = jnp.zeros((pad, ext_ref.shape[1]), F32)

    ext_ref[pad:pad + tt, :] = p_ref[...]
    pos1 = (lax.broadcasted_iota(jnp.int32, (tt, LANE), 0) + t_idx * tt + 1).astype(F32)
    outs = []
    for g, win in enumerate(POOL_WINDOWS):
        cols = slice(g * LANE, (g + 1) * LANE)
        acc = ext_ref[pad:pad + tt, cols]
        for j in range(1, win):
            acc = acc + ext_ref[pad - j:pad - j + tt, cols]
        cur = ext_ref[pad:pad + tt, cols]
        mixed = acc / jnp.minimum(pos1, float(win)) - cur
        outs.append(_dot(mixed.astype(BF16), w_ref[g]))
    o_ref[...] = (jnp.concatenate(outs, axis=1) * sc_ref[...]).astype(o_ref.dtype)
    ext_ref[0:pad, :] = ext_ref[tt:tt + pad, :]


def _pool(p, w, scale, B, S):
    tt = min(S, SEQ_TILE)
    nt = S // tt
    W = w.shape[0] * LANE
    return pl.pallas_call(
        functools.partial(_pool_kernel, tt=tt),
        grid=(B, nt),
        in_specs=[pl.BlockSpec((tt, W), _seq_map(nt)),
                  _const_spec(w.shape), _const_spec(scale.shape)],
        out_specs=pl.BlockSpec((tt, W), _seq_map(nt)),
        out_shape=jax.ShapeDtypeStruct((B * S, W), BF16),
        scratch_shapes=[pltpu.VMEM((tt + POOL_PAD, W), F32)],
        compiler_params=_cparams(("parallel", "arbitrary")),
        name="pool",
    )(p, w, scale)


def _merge_kernel(hn_ref, *refs):
    ys = refs[0:N_BRANCH]
    wgates = refs[N_BRANCH:2 * N_BRANCH]
    wbrs = refs[2 * N_BRANCH:3 * N_BRANCH]
    o_ref = refs[3 * N_BRANCH]
    tm = o_ref.shape[0]
    for rb in range(tm // MERGE_SUB):
        rows = slice(rb * MERGE_SUB, (rb + 1) * MERGE_SUB)
        hn = hn_ref[rows, :]
        acc = None
        for n in range(N_BRANCH):
            logits = _dot(hn, wgates[n][...])
            term = jax.nn.sigmoid(logits) * _dot(ys[n][rows, :], wbrs[n][...])
            acc = term if acc is None else acc + term
        o_ref[rows, :] = acc.astype(o_ref.dtype)


def _merge(hn, ys, wgate, wbr):
    R, D = hn.shape
    tm = min(R, MERGE_ROW_TILE)
    bw = ys[0].shape[1]
    tn = MERGE_TILE
    nc = D // tn
    gate_spec = lambda n: pl.BlockSpec((D, tn), lambda i, c: (0, n * nc + c))
    br_spec = lambda n: pl.BlockSpec((None, bw, tn), lambda i, c: (n, 0, c))
    return pl.pallas_call(
        _merge_kernel,
        grid=(R // tm, nc),
        in_specs=[pl.BlockSpec((tm, D), lambda i, c: (i, 0))]
        + [pl.BlockSpec((tm, bw), lambda i, c: (i, 0)) for _ in ys]
        + [gate_spec(n) for n in range(N_BRANCH)] + [br_spec(n) for n in range(N_BRANCH)],
        out_specs=pl.BlockSpec((tm, tn), lambda i, c: (i, c)),
        out_shape=jax.ShapeDtypeStruct((R, D), BF16),
        compiler_params=_cparams(("parallel", "arbitrary")),
        name="branch_merge",
    )(hn, *ys, *([wgate] * N_BRANCH), *([wbr] * N_BRANCH))


def _outproj_kernel(h_ref, m_ref, w_ref, g_ref, o_ref, hn_ref):
    for rb in range(h_ref.shape[0] // OUT_SUB):
        rows = slice(rb * OUT_SUB, (rb + 1) * OUT_SUB)
        h = h_ref[rows, :] + _dot(m_ref[rows, :], w_ref[...])
        o_ref[rows, :] = h
        hn_ref[rows, :] = _rms(h, g_ref[...]).astype(BF16)


def _outproj(h, merged, w, g):
    R, D = h.shape
    tm = min(R, ROW_TILE)
    row_spec = pl.BlockSpec((tm, D), lambda i: (i, 0))
    return pl.pallas_call(
        _outproj_kernel,
        grid=(R // tm,),
        in_specs=[row_spec, row_spec, _const_spec(w.shape), _const_spec((1, D))],
        out_specs=[row_spec, row_spec],
        out_shape=[jax.ShapeDtypeStruct((R, D), F32), jax.ShapeDtypeStruct((R, D), BF16)],
        compiler_params=_cparams(("parallel",)),
        name="out_proj",
    )(h, merged, w, g.reshape(1, D))


def _ffn_up_kernel(hn_ref, wa_ref, wv_ref, cw_ref, cb_ref, o_ref, carry_ref, *, tn, tiles_per_seq):
    i = pl.program_id(0)
    j = pl.program_id(1)
    tm = hn_ref.shape[0]

    @pl.when(i % tiles_per_seq == 0)
    def _():
        carry_ref[j] = jnp.zeros((SUBLANE, 2 * tn), F32)

    row8 = lax.broadcasted_iota(jnp.int32, (SUBLANE, tn), 0)

    def delayed(u, prev8, k):
        r = pltpu.roll(u, k, 0)
        top = jnp.where(row8 < k, pltpu.roll(prev8, k, 0), r[0:SUBLANE])
        return jnp.concatenate([top, r[SUBLANE:]], axis=0)

    def conv(u, prev8, col0):
        out = cb_ref[:, col0:col0 + tn] + u * cw_ref[CONV_W - 1:CONV_W, col0:col0 + tn]
        for tap in range(CONV_W - 1):
            out = out + delayed(u, prev8, CONV_W - 1 - tap) * cw_ref[tap:tap + 1, col0:col0 + tn]
        return out

    prev = carry_ref[j]
    prev_a, prev_v = prev[:, 0:tn], prev[:, tn:2 * tn]
    def up(rb):
        hn = hn_ref[rb * FF_SUB:(rb + 1) * FF_SUB, :]
        return _dot(hn, wa_ref[...]), _dot(hn, wv_ref[...])

    n_sub = tm // FF_SUB
    nxt = up(0)
    for rb in range(n_sub):
        rows = slice(rb * FF_SUB, (rb + 1) * FF_SUB)
        ua, uv = nxt
        if rb + 1 < n_sub:
            nxt = up(rb + 1)
        a = conv(ua, prev_a, 0)
        v = conv(uv, prev_v, tn)
        prev_a, prev_v = ua[FF_SUB - SUBLANE:], uv[FF_SUB - SUBLANE:]
        o_ref[rows, :] = (_silu(a) * v).astype(o_ref.dtype)
    carry_ref[j] = jnp.concatenate([prev_a, prev_v], axis=1)


def _ffn_up(hn, wa, wv, cw, cb, S):
    R, D = hn.shape
    tm = min(S, FF_ROW_TILE)
    tn = FF_TILE
    ffp = wa.shape[1]
    nj = ffp // tn
    return pl.pallas_call(
        functools.partial(_ffn_up_kernel, tn=tn, tiles_per_seq=S // tm),
        grid=(R // tm, nj),
        in_specs=[pl.BlockSpec((tm, D), lambda i, j: (i, 0)),
                  pl.BlockSpec((D, tn), lambda i, j: (0, j)),
                  pl.BlockSpec((D, tn), lambda i, j: (0, j)),
                  pl.BlockSpec((None, CONV_W, 2 * tn), lambda i, j: (j, 0, 0)),
                  pl.BlockSpec((None, 1, 2 * tn), lambda i, j: (j, 0, 0))],
        out_specs=pl.BlockSpec((tm, tn), lambda i, j: (i, j)),
        out_shape=jax.ShapeDtypeStruct((R, ffp), BF16),
        scratch_shapes=[pltpu.VMEM((nj, SUBLANE, 2 * tn), F32)],
        compiler_params=_cparams(("arbitrary", "arbitrary")),
        name="ffn_up",
    )(hn, wa, wv, cw, cb)


def _ffn_down_kernel(h_ref, act_ref, wd_ref, g_ref, o_ref, *, final):
    for rb in range(h_ref.shape[0] // OUT_SUB):
        rows = slice(rb * OUT_SUB, (rb + 1) * OUT_SUB)
        h = h_ref[rows, :] + _dot(act_ref[rows, :], wd_ref[...])
        o_ref[rows, :] = _rms(h, g_ref[...]) if final else h


def _ffn_down(h, act, wd, g, final):
    R, D = h.shape
    tm = min(R, ROW_TILE)
    ffp = act.shape[1]
    row_spec = pl.BlockSpec((tm, D), lambda i: (i, 0))
    return pl.pallas_call(
        functools.partial(_ffn_down_kernel, final=final),
        grid=(R // tm,),
        in_specs=[row_spec, pl.BlockSpec((tm, ffp), lambda i: (i, 0)),
                  _const_spec(wd.shape), _const_spec((1, D))],
        out_specs=row_spec,
        out_shape=jax.ShapeDtypeStruct((R, D), F32),
        compiler_params=_cparams(("parallel",)),
        name="ffn_down",
    )(h, act, wd, g.reshape(1, D))


def _pad_cols(w, n):
    return jnp.pad(w, ((0, 0), (0, n - w.shape[1])))


def _split_w_in(w_in, D):
    bw = D // 4
    hk = bw // 2
    sizes = [hk, hk, bw, bw, bw, hk, hk, bw, bw, GLA_RANK, bw, N_BRANCH * D]
    offs = np.concatenate([[0], np.cumsum(sizes)])
    seg = lambda a, b: w_in[:, offs[a]:offs[b]]
    w_ret = seg(0, 4)
    w_s5 = seg(4, 5)
    w_gla = jnp.concatenate([seg(5, 9), _pad_cols(seg(9, 10), LANE)], axis=1)
    w_pool = seg(10, 11)
    return [w.astype(BF16) for w in (w_ret, w_s5, w_gla, w_pool)], seg(11, 12).astype(BF16)


def _ffn_weights(w_up, conv_w, conv_b, w_down, tn):
    D, ff2 = w_up.shape
    ff = ff2 // 2
    ffp = -(-ff // tn) * tn
    nj = ffp // tn
    wa = _pad_cols(w_up[:, :ff], ffp).astype(BF16)
    wv = _pad_cols(w_up[:, ff:], ffp).astype(BF16)
    tiles = lambda m: _pad_cols(m, ffp).reshape(m.shape[0], nj, tn)
    cw = jnp.concatenate([tiles(conv_w[:, :ff]), tiles(conv_w[:, ff:])], axis=2).transpose(1, 0, 2)
    cbr = conv_b.reshape(1, ff2)
    cb = jnp.concatenate([tiles(cbr[:, :ff]), tiles(cbr[:, ff:])], axis=2).transpose(1, 0, 2)
    wd = jnp.pad(w_down, ((0, ffp - ff), (0, 0))).astype(BF16)
    return wa, wv, cw, cb, wd


def kernel(x, positions, norm_mix_g, w_in, s5_a_re, s5_a_im, s5_log_dt, s5_b_re, s5_b_im, s5_c_re, s5_c_im, s5_d, s5_w_glu, gla_w_gate, gla_b_gate, pool_w, pool_scale, w_branch, w_out, norm_ffn_g, w_up, conv_w, conv_b, w_down, final_g):
    B, S, D = x.shape
    depth = w_in.shape[0]
    bw = D // 4
    ret_dk, ret_dv = bw // (2 * RET_HEADS), bw // RET_HEADS
    gla_dk, gla_dv = bw // (2 * GLA_HEADS), bw // GLA_HEADS
    assert S % CHUNK == 0 and S % min(S, ROW_TILE) == 0 and B == SUBLANE
    assert ret_dk // 2 * 4 == LANE and bw == S5_BLOCK * LANE

    inv = ROPE_BASE ** (-jnp.arange(0, ret_dk, 2, dtype=F32) / ret_dk)
    ang = positions.astype(F32)[..., None] * inv
    ang = jnp.tile(ang, (1, 1, LANE // (ret_dk // 2))).reshape(B * S, LANE)
    rope = _rope_tables(ang, ret_dk // 2)

    h = x.reshape(B * S, D)
    for l in range(depth):
        ws, w_gate = _split_w_in(w_in[l], D)
        hn, p_ret, p_s5, p_gla, p_pool = _proj(h, norm_mix_g[l], ws)
        ya = _retention(p_ret, rope, B, S, RET_HEADS, ret_dk, ret_dv)
        a_row, bblk, cblk = _s5_params(s5_a_re[l], s5_a_im[l], s5_log_dt[l], s5_b_re[l],
                                       s5_b_im[l], s5_c_re[l], s5_c_im[l])
        yb = _s5(p_s5.reshape(B, S, bw), a_row, bblk, cblk, s5_d[l].reshape(1, bw),
                 s5_w_glu[l].astype(BF16)).reshape(B * S, bw)
        wg = jnp.pad(gla_w_gate[l], ((0, LANE - GLA_RANK), (0, 0))).astype(BF16)
        yc = _gla(p_gla, wg, gla_b_gate[l].reshape(1, -1), B, S, GLA_HEADS, gla_dk, gla_dv)
        yd = _pool(p_pool, pool_w[l].astype(BF16), pool_scale[l].reshape(1, bw), B, S)
        merged = _merge(hn, [ya, yb, yc, yd], w_gate, w_branch[l].astype(BF16))
        h, hn = _outproj(h, merged, w_out[l].astype(BF16), norm_ffn_g[l])
        wa, wv, cw, cb, wd = _ffn_weights(w_up[l], conv_w[l], conv_b[l], w_down[l], FF_TILE)
        act = _ffn_up(hn, wa, wv, cw, cb, S)
        h = _ffn_down(h, act, wd, final_g, final=(l == depth - 1))
    return h.reshape(B, S, D)
```

```python
import functools
import math

import numpy as np
import jax
import jax.numpy as jnp
from jax import lax
from jax.experimental import pallas as pl
from jax.experimental.pallas import tpu as pltpu

F32 = jnp.float32
BF16 = jnp.bfloat16
EPS = 1e-6

N_BRANCH = 4
RET_HEADS = 4
GLA_HEADS = 4
GLA_RANK = 16
GLA_GATE_TEMP = 16.0
S5_GROUP = 16
S5_STATE = 64
POOL_WINDOWS = (2, 4, 8, 16)
POOL_PAD = 16
ROPE_BASE = 10000.0
CONV_W = 3

LANE = 128
SUBLANE = 8
CHUNK = 128
VMEM_LIMIT = 56 * 1024 * 1024
ROW_TILE = 512
MERGE_ROW_TILE = 1024
MERGE_TILE = 512
MERGE_SUB = 256
OUT_SUB = 256
FF_ROW_TILE = 2048
FF_TILE = 512
FF_SUB = 256
SEQ_TILE = 512
S5_TILE = 128
S5_PITCH = S5_TILE + 4
S5_BLOCK = 4


def _cparams(sem):
    return pltpu.CompilerParams(dimension_semantics=sem, vmem_limit_bytes=VMEM_LIMIT)


def _const_spec(shape):
    nd = len(shape)
    return pl.BlockSpec(shape, lambda *_: (0,) * nd, pipeline_mode=pl.Buffered(1))


def _layer_spec(arr, layer):
    nd = arr.ndim - 1
    return pl.BlockSpec((None,) + arr.shape[1:], lambda *_: (layer,) + (0,) * nd,
                        pipeline_mode=pl.Buffered(1))


def _dot(a, b):
    return jnp.dot(a, b, preferred_element_type=F32)


def _dot_nt(a, b):
    return lax.dot_general(a, b, (((1,), (1,)), ((), ())), preferred_element_type=F32)


def _rms(x, g):
    r = lax.rsqrt(jnp.mean(x * x, axis=-1, keepdims=True) + EPS)
    return x * r * g


def _head_norm(o):
    mu = jnp.mean(o, axis=-1, keepdims=True)
    d = o - mu
    var = jnp.mean(d * d, axis=-1, keepdims=True)
    return d * lax.rsqrt(var + EPS)


def _silu(x):
    return x * jax.nn.sigmoid(x)


def _key_blocks(k, heads, dk):
    kt = k.T.astype(BF16)
    zero = jnp.zeros((dk, kt.shape[1]), BF16)
    cols = [jnp.concatenate([kt[h * dk:(h + 1) * dk] if r == h else zero for r in range(heads)], axis=0)
            for h in range(heads)]
    return jnp.concatenate(cols, axis=1)


def _seq_map(nt):
    return lambda b, t: (b * nt + t, 0)


def _proj_kernel(h_ref, g_ref, w_ret, w_s5, w_gla, w_pool, o_hn, o_ret, o_s5, o_gla, o_pool):
    hn = _rms(h_ref[...], g_ref[...]).astype(BF16)
    o_hn[...] = hn
    o_ret[...] = _dot(hn, w_ret[...])
    o_s5[...] = _dot(hn, w_s5[...])
    o_gla[...] = _dot(hn, w_gla[...])
    o_pool[...] = _dot(hn, w_pool[...])


def _proj(h, g, ws, layer):
    R, D = h.shape
    tm = min(R, ROW_TILE)
    widths = [w.shape[2] for w in ws]
    return pl.pallas_call(
        _proj_kernel,
        grid=(R // tm,),
        in_specs=[pl.BlockSpec((tm, D), lambda i: (i, 0)), _const_spec((1, D))]
        + [_layer_spec(w, layer) for w in ws],
        out_specs=[pl.BlockSpec((tm, n), lambda i: (i, 0)) for n in [D] + widths],
        out_shape=[jax.ShapeDtypeStruct((R, D), BF16)]
        + [jax.ShapeDtypeStruct((R, n), F32) for n in widths],
        compiler_params=_cparams(("parallel",)),
        name="mixer_proj",
    )(h, g.reshape(1, D), *ws)


def _ret_consts(heads, dk, dv):
    log_g = np.log(1.0 - 2.0 ** (-5.0 - np.arange(heads, dtype=np.float64)))
    idx = np.arange(CHUNK, dtype=np.float64)
    rel = idx[:, None] - idx[None, :]
    dmask = np.where(rel >= 0, np.exp(np.maximum(rel, 0.0)[None] * log_g[:, None, None]), 0.0)
    dmask = np.concatenate(list(dmask), axis=1)
    lane_g = np.repeat(log_g, dk)[None, :]
    qdec = np.exp((idx[:, None] + 1.0) * lane_g)
    kdec = np.exp((CHUNK - 1.0 - idx)[:, None] * lane_g)
    sdec = np.exp(CHUNK * np.repeat(log_g, dk))[:, None] * np.ones((1, heads * dv))
    bmask = np.kron(np.eye(heads), np.ones((dk, dv)))
    f = lambda a: jnp.asarray(a, F32)
    return f(dmask), f(qdec), f(kdec), f(sdec), f(bmask)


def _rope_kernel(ang_ref, cos_ref, ssin_ref, *, half):
    ang = ang_ref[...]
    lane = lax.broadcasted_iota(jnp.int32, ang.shape, 1)
    cos_ref[...] = jnp.cos(ang)
    s = jnp.sin(ang)
    ssin_ref[...] = jnp.where((lane % (2 * half)) < half, -s, s)


def _rope_tables(ang, half):
    R, W = ang.shape
    tm = min(R, 2048)
    spec = pl.BlockSpec((tm, W), lambda i: (i, 0))
    return pl.pallas_call(
        functools.partial(_rope_kernel, half=half),
        grid=(R // tm,),
        in_specs=[spec],
        out_specs=[spec, spec],
        out_shape=[jax.ShapeDtypeStruct((R, W), F32)] * 2,
        compiler_params=_cparams(("parallel",)),
        name="rope_tables",
    )(ang)


def _ret_kernel(x_ref, cos_ref, ssin_ref, dmask_ref, qdec_ref, kdec_ref, sdec_ref, bmask_ref,
                o_ref, state_ref, *, heads, dk, dv, n_chunks):
    hk = heads * dk
    hv = heads * dv

    @pl.when(pl.program_id(1) == 0)
    def _():
        state_ref[...] = jnp.zeros_like(state_ref)

    lane = lax.broadcasted_iota(jnp.int32, (CHUNK, hk), 1)
    first_half = (lane % dk) < (dk // 2)

    def rotary(x, cos2, ssin):
        swapped = jnp.where(first_half, pltpu.roll(x, hk - dk // 2, 1), pltpu.roll(x, dk // 2, 1))
        return x * cos2 + swapped * ssin

    def chunk(ci, carry):
        r0 = pl.multiple_of(ci * CHUNK, CHUNK)
        rows = pl.ds(r0, CHUNK)
        reps = hk // LANE
        cos2 = jnp.concatenate([cos_ref[rows, :]] * reps, axis=1)
        ssin = jnp.concatenate([ssin_ref[rows, :]] * reps, axis=1)
        q = rotary(x_ref[rows, 0:hk], cos2, ssin)
        k = rotary(x_ref[rows, hk:2 * hk], cos2, ssin) * (dk ** -0.5)
        v = x_ref[rows, 2 * hk:2 * hk + hv]
        gate = x_ref[rows, 2 * hk + hv:2 * hk + 2 * hv]
        vb = v.astype(BF16)
        state = state_ref[...]
        cross = _dot((q * qdec_ref[...]).astype(BF16), state.astype(BF16))
        sb = (_dot(q.astype(BF16), _key_blocks(k, heads, dk)) * dmask_ref[...]).astype(BF16)
        outs = [_dot(sb[:, h * CHUNK:(h + 1) * CHUNK], vb[:, h * dv:(h + 1) * dv]) for h in range(heads)]
        o = jnp.concatenate(outs, axis=1) + cross
        kd = (k * kdec_ref[...]).astype(BF16)
        upd = _dot(kd.T, vb)
        state_ref[...] = sdec_ref[...] * state + bmask_ref[...] * upd
        y = jnp.concatenate(
            [_head_norm(o[:, h * dv:(h + 1) * dv]) for h in range(heads)], axis=1)
        o_ref[rows, :] = (y * _silu(gate)).astype(o_ref.dtype)
        return carry

    lax.fori_loop(0, n_chunks, chunk, 0)


def _retention(x, rope, B, S, heads, dk, dv):
    tt = min(S, SEQ_TILE)
    nt = S // tt
    win = 2 * heads * dk + 2 * heads * dv
    consts = _ret_consts(heads, dk, dv)
    kern = functools.partial(_ret_kernel, heads=heads, dk=dk, dv=dv, n_chunks=tt // CHUNK)
    return pl.pallas_call(
        kern,
        grid=(B, nt),
        in_specs=[pl.BlockSpec((tt, win), _seq_map(nt)),
                  pl.BlockSpec((tt, LANE), _seq_map(nt)), pl.BlockSpec((tt, LANE), _seq_map(nt))]
        + [_const_spec(c.shape) for c in consts],
        out_specs=pl.BlockSpec((tt, heads * dv), _seq_map(nt)),
        out_shape=jax.ShapeDtypeStruct((B * S, heads * dv), BF16),
        scratch_shapes=[pltpu.VMEM((heads * dk, heads * dv), F32)],
        compiler_params=_cparams(("parallel", "arbitrary")),
        name="retention",
    )(x, *rope, *consts)


def _gla_consts(heads, dk, dv):
    C = CHUNK
    n_lvl = int(math.log2(C))
    t = np.arange(C)
    sums = []
    masks = []
    for l in range(n_lvl):
        bit = (t >> l) & 1
        start = (t >> l) << l
        r = t[None, :]
        upper = (bit[:, None] == 1) & (r >= start[:, None]) & (r <= t[:, None])
        lower = (bit[:, None] == 0) & (r > t[:, None]) & (r < (start + (1 << l))[:, None])
        sums.append((upper | lower).astype(np.float64))
        m = ((bit[:, None] == 1) & (bit[None, :] == 0)
             & ((t[:, None] >> (l + 1)) == (t[None, :] >> (l + 1))))
        masks.append(np.tile(m.astype(np.float64), (1, heads)))
    masks.append(np.tile(np.eye(C), (1, heads)))
    sums.append((t[None, :] <= t[:, None]).astype(np.float64))
    sums.append((t[None, :] > t[:, None]).astype(np.float64))
    g = np.concatenate(sums, axis=0)
    gcat = np.concatenate([g, g], axis=1)
    bmask = np.kron(np.eye(heads), np.ones((dk, dv)))
    return jnp.asarray(gcat, BF16), jnp.asarray(np.stack(masks), F32), jnp.asarray(bmask, F32)


def _log_sigmoid(x):
    return jnp.minimum(x, 0.0) - jnp.log(1.0 + jnp.exp(-jnp.abs(x)))


def _gla_kernel(x_ref, wg_ref, bg_ref, gcat_ref, masks_ref, bmask_ref, o_ref, state_ref,
                *, heads, dk, dv, n_chunks):
    hk = heads * dk
    hv = heads * dv
    C = CHUNK
    n_lvl = int(math.log2(C))

    @pl.when(pl.program_id(1) == 0)
    def _():
        state_ref[...] = jnp.zeros_like(state_ref)

    key_blocks = functools.partial(_key_blocks, heads=heads, dk=dk)

    def chunk(ci, carry):
        r0 = pl.multiple_of(ci * C, C)
        rows = pl.ds(r0, C)
        q = x_ref[rows, 0:hk] * (dk ** -0.5)
        k = x_ref[rows, hk:2 * hk]
        v = x_ref[rows, 2 * hk:2 * hk + hv]
        gate = x_ref[rows, 2 * hk + hv:2 * hk + 2 * hv]
        code = x_ref[rows, 2 * hk + 2 * hv:2 * hk + 2 * hv + LANE]
        vb = v.astype(BF16)
        log_a = _log_sigmoid(_dot(code.astype(BF16), wg_ref[...]) + bg_ref[...]) * (1.0 / GLA_GATE_TEMP)
        hi = log_a.astype(BF16)
        lo = (log_a - hi.astype(F32)).astype(BF16)
        sums = _dot(gcat_ref[...], jnp.concatenate([hi, lo], axis=0))

        scores = masks_ref[n_lvl] * _dot(q.astype(BF16), key_blocks(k))
        for l in range(n_lvl):
            z = jnp.exp(sums[l * C:(l + 1) * C])
            scores = scores + masks_ref[l] * _dot((q * z).astype(BF16), key_blocks(k * z))
        sb = scores.astype(BF16)
        intra = jnp.concatenate(
            [_dot(sb[:, h * C:(h + 1) * C], vb[:, h * dv:(h + 1) * dv]) for h in range(heads)], axis=1)

        state = state_ref[...]
        e_pre = sums[n_lvl * C:(n_lvl + 1) * C]
        e_suf = sums[(n_lvl + 1) * C:(n_lvl + 2) * C]
        cross = _dot((q * jnp.exp(e_pre)).astype(BF16), state.astype(BF16))
        kd = (k * jnp.exp(e_suf)).astype(BF16)
        upd = _dot(kd.T, vb)
        total = jnp.broadcast_to(jnp.exp(e_pre[C - 1:C, :]), (LANE, hk))
        dec = jnp.concatenate([total.T] * (hv // LANE), axis=1)
        state_ref[...] = dec * state + bmask_ref[...] * upd

        o = intra + cross
        y = jnp.concatenate(
            [_head_norm(o[:, h * dv:(h + 1) * dv]) for h in range(heads)], axis=1)
        o_ref[rows, :] = (y * _silu(gate)).astype(o_ref.dtype)
        return carry

    lax.fori_loop(0, n_chunks, chunk, 0)


def _gla(x, wg, bg, B, S, heads, dk, dv):
    tt = min(S, SEQ_TILE)
    nt = S // tt
    win = 2 * heads * dk + 2 * heads * dv + LANE
    consts = _gla_consts(heads, dk, dv)
    kern = functools.partial(_gla_kernel, heads=heads, dk=dk, dv=dv, n_chunks=tt // CHUNK)
    return pl.pallas_call(
        kern,
        grid=(B, nt),
        in_specs=[pl.BlockSpec((tt, win), _seq_map(nt)),
                  _const_spec(wg.shape), _const_spec(bg.shape)]
        + [_const_spec(c.shape) for c in consts],
        out_specs=pl.BlockSpec((tt, heads * dv), _seq_map(nt)),
        out_shape=jax.ShapeDtypeStruct((B * S, heads * dv), BF16),
        scratch_shapes=[pltpu.VMEM((heads * dk, heads * dv), F32)],
        compiler_params=_cparams(("parallel", "arbitrary")),
        name="gla",
    )(x, wg, bg, *consts)


def _s5_kernel(u_ref, a_ref, bblk_ref, cblk_ref, d_ref, wglu_ref, o_ref,
               stage_ref, utb_ref, xs_ref, ytb_ref, st_ref, *, tt):
    nb, w = u_ref.shape[0], u_ref.shape[2]
    nblk = w // LANE
    sw = a_ref.shape[1] // (2 * nblk)

    @pl.when(pl.program_id(0) == 0)
    def _():
        st_ref[...] = jnp.zeros_like(st_ref)

    for b in range(nb):
        for m in range(nblk):
            stage_ref[m, b * S5_PITCH:b * S5_PITCH + tt, :] = u_ref[b, :, m * LANE:(m + 1) * LANE]

    def gather(t, carry):
        rows = pl.ds(pl.multiple_of(t * nb, nb), nb)
        for m in range(nblk):
            utb_ref[rows, m * LANE:(m + 1) * LANE] = stage_ref[m, pl.ds(t, nb, stride=S5_PITCH), :]
        return carry

    lax.fori_loop(0, tt, gather, 0)

    u = utb_ref[...]
    ub = u.astype(BF16)
    for m in range(nblk):
        xs_ref[:, 2 * sw * m:2 * sw * (m + 1)] = _dot(ub[:, m * LANE:(m + 1) * LANE], bblk_ref[m])

    def step(t, carry):
        rows = pl.ds(pl.multiple_of(t * nb, nb), nb)
        new = []
        for m in range(nblk):
            xr, xi = carry[2 * m], carry[2 * m + 1]
            re = slice(2 * sw * m, 2 * sw * m + sw)
            im = slice(2 * sw * m + sw, 2 * sw * (m + 1))
            ar, ai = a_ref[:, re], a_ref[:, im]
            nr = ar * xr - ai * xi + xs_ref[rows, re]
            ni = ar * xi + ai * xr + xs_ref[rows, im]
            xs_ref[rows, re] = nr
            xs_ref[rows, im] = ni
            new += [nr, ni]
        return tuple(new)

    init = tuple(st_ref[:, sw * c:sw * (c + 1)] for c in range(2 * nblk))
    last = lax.fori_loop(0, tt, step, init, unroll=2)
    for c in range(2 * nblk):
        st_ref[:, sw * c:sw * (c + 1)] = last[c]

    ys = [_dot(xs_ref[:, 2 * sw * m:2 * sw * (m + 1)].astype(BF16), cblk_ref[m]) for m in range(nblk)]
    y = jnp.concatenate(ys, axis=1) + d_ref[...] * u
    y = jax.nn.gelu(y)
    y = y * jax.nn.sigmoid(_dot(y.astype(BF16), wglu_ref[...]))

    for m in range(nblk):
        ytb_ref[m] = y[:, m * LANE:(m + 1) * LANE]
    for b in range(nb):
        for m in range(nblk):
            o_ref[b, :, m * LANE:(m + 1) * LANE] = (
                ytb_ref[m, pl.ds(b, tt, stride=nb), :].astype(o_ref.dtype))


def _s5(u3, a_row, bblk, cblk, d_row, wglu, layer):
    B, S, W = u3.shape
    tt = min(S, S5_TILE)
    ns2 = a_row.shape[1]
    nblk = W // LANE
    a8 = jnp.broadcast_to(a_row, (B, ns2))
    return pl.pallas_call(
        functools.partial(_s5_kernel, tt=tt),
        grid=(S // tt,),
        in_specs=[pl.BlockSpec((B, tt, W), lambda t: (0, t, 0)),
                  _const_spec(a8.shape), _layer_spec(bblk, layer), _layer_spec(cblk, layer),
                  _const_spec(d_row.shape), _layer_spec(wglu, layer)],
        out_specs=pl.BlockSpec((B, tt, W), lambda t: (0, t, 0)),
        out_shape=jax.ShapeDtypeStruct((B, S, W), BF16),
        scratch_shapes=[pltpu.VMEM((nblk, B * S5_PITCH, LANE), F32),
                        pltpu.VMEM((tt * B, W), F32),
                        pltpu.VMEM((tt * B, ns2), F32),
                        pltpu.VMEM((nblk, tt * B, LANE), F32),
                        pltpu.VMEM((B, ns2), F32)],
        compiler_params=_cparams(("arbitrary",)),
        name="s5",
    )(u3, a8, bblk, cblk, d_row, wglu)


def _s5_params(a_re, a_im, log_dt, b_re, b_im, c_re, c_im):
    G, P = a_re.shape
    n_c = b_re.shape[-1]
    gpb = LANE // n_c
    nblk = G // gpb
    dt = jnp.exp(log_dt)[:, None]
    mag = jnp.exp(a_re * dt)
    abar_re = mag * jnp.cos(a_im * dt)
    abar_im = mag * jnp.sin(a_im * dt)
    den = a_re * a_re + a_im * a_im
    nr, ni = abar_re - 1.0, abar_im
    f_re = (nr * a_re + ni * a_im) / den
    f_im = (ni * a_re - nr * a_im) / den
    bb_re = f_re[..., None] * b_re - f_im[..., None] * b_im
    bb_im = f_re[..., None] * b_im + f_im[..., None] * b_re
    eye = jnp.eye(gpb, dtype=F32)

    def bd_in(m):
        m = m.reshape(nblk, gpb, P, n_c)
        return jnp.einsum('ngpc,gh->ngchp', m, eye).reshape(nblk, gpb * n_c, gpb * P)

    def bd_out(m):
        m = m.reshape(nblk, gpb, n_c, P)
        return jnp.einsum('ngcp,gh->ngphc', m, eye).reshape(nblk, gpb * P, gpb * n_c)

    a_row = jnp.concatenate([abar_re.reshape(nblk, 1, gpb * P), abar_im.reshape(nblk, 1, gpb * P)],
                            axis=2).reshape(1, 2 * G * P)
    bblk = jnp.concatenate([bd_in(bb_re), bd_in(bb_im)], axis=2).astype(BF16)
    cblk = jnp.concatenate([bd_out(c_re), -bd_out(c_im)], axis=1).astype(BF16)
    return a_row, bblk, cblk


def _pool_kernel(p_ref, w_ref, sc_ref, o_ref, ext_ref, *, tt):
    pad = POOL_PAD
    t_idx = pl.program_id(1)

    @pl.when(t_idx == 0)
    def _():
        ext_ref[0:pad, :] = jnp.zeros((pad, ext_ref.shape[1]), F32)

    ext_ref[pad:pad + tt, :] = p_ref[...]
    pos1 = (lax.broadcasted_iota(jnp.int32, (tt, LANE), 0) + t_idx * tt + 1).astype(F32)
    outs = []
    for g, win in enumerate(POOL_WINDOWS):
        cols = slice(g * LANE, (g + 1) * LANE)
        acc = ext_ref[pad:pad + tt, cols]
        for j in range(1, win):
            acc = acc + ext_ref[pad - j:pad - j + tt, cols]
        cur = ext_ref[pad:pad + tt, cols]
        mixed = acc / jnp.minimum(pos1, float(win)) - cur
        outs.append(_dot(mixed.astype(BF16), w_ref[g]))
    o_ref[...] = (jnp.concatenate(outs, axis=1) * sc_ref[...]).astype(o_ref.dtype)
    ext_ref[0:pad, :] = ext_ref[tt:tt + pad, :]


def _pool(p, w, scale, B, S):
    tt = min(S, SEQ_TILE)
    nt = S // tt
    W = w.shape[0] * LANE
    return pl.pallas_call(
        functools.partial(_pool_kernel, tt=tt),
        grid=(B, nt),
        in_specs=[pl.BlockSpec((tt, W), _seq_map(nt)),
                  _const_spec(w.shape), _const_spec(scale.shape)],
        out_specs=pl.BlockSpec((tt, W), _seq_map(nt)),
        out_shape=jax.ShapeDtypeStruct((B * S, W), BF16),
        scratch_shapes=[pltpu.VMEM((tt + POOL_PAD, W), F32)],
        compiler_params=_cparams(("parallel", "arbitrary")),
        name="pool",
    )(p, w, scale)


def _merge_kernel(hn_ref, *refs):
    ys = refs[0:N_BRANCH]
    wgates = refs[N_BRANCH:2 * N_BRANCH]
    wbrs = refs[2 * N_BRANCH:3 * N_BRANCH]
    o_ref = refs[3 * N_BRANCH]
    tm = o_ref.shape[0]
    for rb in range(tm // MERGE_SUB):
        rows = slice(rb * MERGE_SUB, (rb + 1) * MERGE_SUB)
        hn = hn_ref[rows, :]
        acc = None
        for n in range(N_BRANCH):
            logits = _dot(hn, wgates[n][...])
            term = jax.nn.sigmoid(logits) * _dot(ys[n][rows, :], wbrs[n][...])
            acc = term if acc is None else acc + term
        o_ref[rows, :] = acc.astype(o_ref.dtype)


def _merge(hn, ys, wgate, wbr, layer):
    R, D = hn.shape
    tm = min(R, MERGE_ROW_TILE)
    bw = ys[0].shape[1]
    tn = MERGE_TILE
    nc = D // tn
    gate_spec = lambda n: pl.BlockSpec((None, D, tn), lambda i, c: (layer, 0, n * nc + c))
    br_spec = lambda n: pl.BlockSpec((None, None, bw, tn), lambda i, c: (layer, n, 0, c))
    return pl.pallas_call(
        _merge_kernel,
        grid=(R // tm, nc),
        in_specs=[pl.BlockSpec((tm, D), lambda i, c: (i, 0))]
        + [pl.BlockSpec((tm, bw), lambda i, c: (i, 0)) for _ in ys]
        + [gate_spec(n) for n in range(N_BRANCH)] + [br_spec(n) for n in range(N_BRANCH)],
        out_specs=pl.BlockSpec((tm, tn), lambda i, c: (i, c)),
        out_shape=jax.ShapeDtypeStruct((R, D), BF16),
        compiler_params=_cparams(("parallel", "arbitrary")),
        name="branch_merge",
    )(hn, *ys, *([wgate] * N_BRANCH), *([wbr] * N_BRANCH))


def _outproj_kernel(h_ref, m_ref, w_ref, g_ref, o_ref, hn_ref):
    for rb in range(h_ref.shape[0] // OUT_SUB):
        rows = slice(rb * OUT_SUB, (rb + 1) * OUT_SUB)
        h = h_ref[rows, :] + _dot(m_ref[rows, :], w_ref[...])
        o_ref[rows, :] = h
        hn_ref[rows, :] = _rms(h, g_ref[...]).astype(BF16)


def _outproj(h, merged, w, g, layer):
    R, D = h.shape
    tm = min(R, ROW_TILE)
    row_spec = pl.BlockSpec((tm, D), lambda i: (i, 0))
    return pl.pallas_call(
        _outproj_kernel,
        grid=(R // tm,),
        in_specs=[row_spec, row_spec, _layer_spec(w, layer), _const_spec((1, D))],
        out_specs=[row_spec, row_spec],
        out_shape=[jax.ShapeDtypeStruct((R, D), F32), jax.ShapeDtypeStruct((R, D), BF16)],
        compiler_params=_cparams(("parallel",)),
        name="out_proj",
    )(h, merged, w, g.reshape(1, D))


def _ffn_up_kernel(hn_ref, wa_ref, wv_ref, cw_ref, cb_ref, o_ref, carry_ref, *, tn, tiles_per_seq):
    i = pl.program_id(0)
    j = pl.program_id(1)
    tm = hn_ref.shape[0]

    @pl.when(i % tiles_per_seq == 0)
    def _():
        carry_ref[j] = jnp.zeros((SUBLANE, 2 * tn), F32)

    row8 = lax.broadcasted_iota(jnp.int32, (SUBLANE, tn), 0)

    def delayed(u, prev8, k):
        r = pltpu.roll(u, k, 0)
        top = jnp.where(row8 < k, pltpu.roll(prev8, k, 0), r[0:SUBLANE])
        return jnp.concatenate([top, r[SUBLANE:]], axis=0)

    def conv(u, prev8, col0):
        out = cb_ref[:, col0:col0 + tn] + u * cw_ref[CONV_W - 1:CONV_W, col0:col0 + tn]
        for tap in range(CONV_W - 1):
            out = out + delayed(u, prev8, CONV_W - 1 - tap) * cw_ref[tap:tap + 1, col0:col0 + tn]
        return out

    prev = carry_ref[j]
    prev_a, prev_v = prev[:, 0:tn], prev[:, tn:2 * tn]
    def up(rb):
        hn = hn_ref[rb * FF_SUB:(rb + 1) * FF_SUB, :]
        return _dot(hn, wa_ref[...]), _dot(hn, wv_ref[...])

    n_sub = tm // FF_SUB
    nxt = up(0)
    for rb in range(n_sub):
        rows = slice(rb * FF_SUB, (rb + 1) * FF_SUB)
        ua, uv = nxt
        if rb + 1 < n_sub:
            nxt = up(rb + 1)
        a = conv(ua, prev_a, 0)
        v = conv(uv, prev_v, tn)
        prev_a, prev_v = ua[FF_SUB - SUBLANE:], uv[FF_SUB - SUBLANE:]
        o_ref[rows, :] = (_silu(a) * v).astype(o_ref.dtype)
    carry_ref[j] = jnp.concatenate([prev_a, prev_v], axis=1)


def _ffn_up(hn, wa, wv, cw, cb, S, layer):
    R, D = hn.shape
    tm = min(S, FF_ROW_TILE)
    tn = FF_TILE
    ffp = wa.shape[2]
    nj = ffp // tn
    return pl.pallas_call(
        functools.partial(_ffn_up_kernel, tn=tn, tiles_per_seq=S // tm),
        grid=(R // tm, nj),
        in_specs=[pl.BlockSpec((tm, D), lambda i, j: (i, 0)),
                  pl.BlockSpec((None, D, tn), lambda i, j: (layer, 0, j)),
                  pl.BlockSpec((None, D, tn), lambda i, j: (layer, 0, j)),
                  pl.BlockSpec((None, None, CONV_W, 2 * tn), lambda i, j: (layer, j, 0, 0)),
                  pl.BlockSpec((None, None, 1, 2 * tn), lambda i, j: (layer, j, 0, 0))],
        out_specs=pl.BlockSpec((tm, tn), lambda i, j: (i, j)),
        out_shape=jax.ShapeDtypeStruct((R, ffp), BF16),
        scratch_shapes=[pltpu.VMEM((nj, SUBLANE, 2 * tn), F32)],
        compiler_params=_cparams(("arbitrary", "arbitrary")),
        name="ffn_up",
    )(hn, wa, wv, cw, cb)


def _ffn_down_kernel(h_ref, act_ref, wd_ref, g_ref, o_ref, *, final):
    for rb in range(h_ref.shape[0] // OUT_SUB):
        rows = slice(rb * OUT_SUB, (rb + 1) * OUT_SUB)
        h = h_ref[rows, :] + _dot(act_ref[rows, :], wd_ref[...])
        o_ref[rows, :] = _rms(h, g_ref[...]) if final else h


def _ffn_down(h, act, wd, g, final, layer):
    R, D = h.shape
    tm = min(R, ROW_TILE)
    ffp = act.shape[1]
    row_spec = pl.BlockSpec((tm, D), lambda i: (i, 0))
    return pl.pallas_call(
        functools.partial(_ffn_down_kernel, final=final),
        grid=(R // tm,),
        in_specs=[row_spec, pl.BlockSpec((tm, ffp), lambda i: (i, 0)),
                  _layer_spec(wd, layer), _const_spec((1, D))],
        out_specs=row_spec,
        out_shape=jax.ShapeDtypeStruct((R, D), F32),
        compiler_params=_cparams(("parallel",)),
        name="ffn_down",
    )(h, act, wd, g.reshape(1, D))


def _pad_cols(w, n):
    return jnp.pad(w, ((0, 0), (0, n - w.shape[1])))


def _split_w_in(w_in, D):
    bw = D // 4
    hk = bw // 2
    sizes = [hk, hk, bw, bw, bw, hk, hk, bw, bw, GLA_RANK, bw, N_BRANCH * D]
    offs = np.concatenate([[0], np.cumsum(sizes)])
    seg = lambda a, b: w_in[:, offs[a]:offs[b]]
    w_ret = seg(0, 4)
    w_s5 = seg(4, 5)
    w_gla = jnp.concatenate([seg(5, 9), _pad_cols(seg(9, 10), LANE)], axis=1)
    w_pool = seg(10, 11)
    return [w.astype(BF16) for w in (w_ret, w_s5, w_gla, w_pool)], seg(11, 12).astype(BF16)


def _ffn_weights(w_up, conv_w, conv_b, w_down, tn):
    D, ff2 = w_up.shape
    ff = ff2 // 2
    ffp = -(-ff // tn) * tn
    nj = ffp // tn
    wa = _pad_cols(w_up[:, :ff], ffp).astype(BF16)
    wv = _pad_cols(w_up[:, ff:], ffp).astype(BF16)
    tiles = lambda m: _pad_cols(m, ffp).reshape(m.shape[0], nj, tn)
    cw = jnp.concatenate([tiles(conv_w[:, :ff]), tiles(conv_w[:, ff:])], axis=2).transpose(1, 0, 2)
    cbr = conv_b.reshape(1, ff2)
    cb = jnp.concatenate([tiles(cbr[:, :ff]), tiles(cbr[:, ff:])], axis=2).transpose(1, 0, 2)
    wd = jnp.pad(w_down, ((0, ffp - ff), (0, 0))).astype(BF16)
    return wa, wv, cw, cb, wd


def kernel(x, positions, norm_mix_g, w_in, s5_a_re, s5_a_im, s5_log_dt, s5_b_re, s5_b_im, s5_c_re, s5_c_im, s5_d, s5_w_glu, gla_w_gate, gla_b_gate, pool_w, pool_scale, w_branch, w_out, norm_ffn_g, w_up, conv_w, conv_b, w_down, final_g):
    B, S, D = x.shape
    depth = w_in.shape[0]
    bw = D // 4
    ret_dk, ret_dv = bw // (2 * RET_HEADS), bw // RET_HEADS
    gla_dk, gla_dv = bw // (2 * GLA_HEADS), bw // GLA_HEADS
    assert S % CHUNK == 0 and S % min(S, ROW_TILE) == 0 and B == SUBLANE
    assert ret_dk // 2 * 4 == LANE and bw == S5_BLOCK * LANE

    inv = ROPE_BASE ** (-jnp.arange(0, ret_dk, 2, dtype=F32) / ret_dk)
    ang = positions.astype(F32)[..., None] * inv
    ang = jnp.tile(ang, (1, 1, LANE // (ret_dk // 2))).reshape(B * S, LANE)
    rope = _rope_tables(ang, ret_dk // 2)

    ws, w_gate = jax.vmap(lambda w: _split_w_in(w, D))(w_in)
    a_row, bblk, cblk = jax.vmap(_s5_params)(s5_a_re, s5_a_im, s5_log_dt, s5_b_re, s5_b_im,
                                             s5_c_re, s5_c_im)
    wglu = s5_w_glu.astype(BF16)
    wg = jnp.pad(gla_w_gate, ((0, 0), (0, LANE - GLA_RANK), (0, 0))).astype(BF16)
    pw = pool_w.astype(BF16)
    wbr = w_branch.astype(BF16)
    wo = w_out.astype(BF16)
    wa, wv, cw, cb, wd = jax.vmap(lambda a, b, c, d: _ffn_weights(a, b, c, d, FF_TILE))(
        w_up, conv_w, conv_b, w_down)

    h = x.reshape(B * S, D)
    for l in range(depth):
        hn, p_ret, p_s5, p_gla, p_pool = _proj(h, norm_mix_g[l], ws, l)
        ya = _retention(p_ret, rope, B, S, RET_HEADS, ret_dk, ret_dv)
        yb = _s5(p_s5.reshape(B, S, bw), a_row[l], bblk, cblk, s5_d[l].reshape(1, bw),
                 wglu, l).reshape(B * S, bw)
        yc = _gla(p_gla, wg[l], gla_b_gate[l].reshape(1, -1), B, S, GLA_HEADS, gla_dk, gla_dv)
        yd = _pool(p_pool, pw[l], pool_scale[l].reshape(1, bw), B, S)
        merged = _merge(hn, [ya, yb, yc, yd], w_gate, wbr, l)
        h, hn = _outproj(h, merged, wo, norm_ffn_g[l], l)
        act = _ffn_up(hn, wa, wv, cw, cb, S, l)
        h = _ffn_down(h, act, wd, final_g, final=(l == depth - 1), layer=l)
    return h.reshape(B, S, D)
```

```python
import functools
import math

import numpy as np
import jax
import jax.numpy as jnp
from jax import lax
from jax.experimental import pallas as pl
from jax.experimental.pallas import tpu as pltpu

F32 = jnp.float32
BF16 = jnp.bfloat16
EPS = 1e-6

N_BRANCH = 4
RET_HEADS = 4
GLA_HEADS = 4
GLA_RANK = 16
GLA_GATE_TEMP = 16.0
S5_GROUP = 16
S5_STATE = 64
POOL_WINDOWS = (2, 4, 8, 16)
POOL_PAD = 16
ROPE_BASE = 10000.0
CONV_W = 3

LANE = 128
SUBLANE = 8
CHUNK = 128
VMEM_LIMIT = 56 * 1024 * 1024
ROW_TILE = 512
MERGE_ROW_TILE = 1024
MERGE_TILE = 512
MERGE_SUB = 256
OUT_SUB = 256
FF_ROW_TILE = 2048
FF_TILE = 512
FF_COL = 512
FF_SUB = 128
SEQ_TILE = 512
S5_TILE = 128
S5_PITCH = S5_TILE + 4
S5_BLOCK = 4


def _cparams(sem):
    return pltpu.CompilerParams(dimension_semantics=sem, vmem_limit_bytes=VMEM_LIMIT)


def _const_spec(shape):
    nd = len(shape)
    return pl.BlockSpec(shape, lambda *_: (0,) * nd, pipeline_mode=pl.Buffered(1))


def _layer_spec(arr, layer):
    nd = arr.ndim - 1
    return pl.BlockSpec((None,) + arr.shape[1:], lambda *_: (layer,) + (0,) * nd,
                        pipeline_mode=pl.Buffered(1))


def _dot(a, b):
    return jnp.dot(a, b, preferred_element_type=F32)


def _dot_nt(a, b):
    return lax.dot_general(a, b, (((1,), (1,)), ((), ())), preferred_element_type=F32)


def _rms(x, g):
    r = lax.rsqrt(jnp.mean(x * x, axis=-1, keepdims=True) + EPS)
    return x * r * g


def _head_norm(o):
    mu = jnp.mean(o, axis=-1, keepdims=True)
    d = o - mu
    var = jnp.mean(d * d, axis=-1, keepdims=True)
    return d * lax.rsqrt(var + EPS)


def _silu(x):
    return x * jax.nn.sigmoid(x)


def _key_blocks(k, heads, dk):
    kt = k.T.astype(BF16)
    zero = jnp.zeros((dk, kt.shape[1]), BF16)
    cols = [jnp.concatenate([kt[h * dk:(h + 1) * dk] if r == h else zero for r in range(heads)], axis=0)
            for h in range(heads)]
    return jnp.concatenate(cols, axis=1)


def _seq_map(nt):
    return lambda b, t: (b * nt + t, 0)


def _proj_kernel(h_ref, g_ref, w_ret, w_s5, w_gla, w_pool, o_hn, o_ret, o_s5, o_gla, o_pool):
    hn = _rms(h_ref[...], g_ref[...]).astype(BF16)
    o_hn[...] = hn
    o_ret[...] = _dot(hn, w_ret[...])
    o_s5[...] = _dot(hn, w_s5[...])
    o_gla[...] = _dot(hn, w_gla[...])
    o_pool[...] = _dot(hn, w_pool[...])


def _proj(h, g, ws, layer):
    R, D = h.shape
    tm = min(R, ROW_TILE)
    widths = [w.shape[2] for w in ws]
    return pl.pallas_call(
        _proj_kernel,
        grid=(R // tm,),
        in_specs=[pl.BlockSpec((tm, D), lambda i: (i, 0)), _const_spec((1, D))]
        + [_layer_spec(w, layer) for w in ws],
        out_specs=[pl.BlockSpec((tm, n), lambda i: (i, 0)) for n in [D] + widths],
        out_shape=[jax.ShapeDtypeStruct((R, D), BF16)]
        + [jax.ShapeDtypeStruct((R, n), F32) for n in widths],
        compiler_params=_cparams(("parallel",)),
        name="mixer_proj",
    )(h, g.reshape(1, D), *ws)


def _ret_consts(heads, dk, dv):
    log_g = np.log(1.0 - 2.0 ** (-5.0 - np.arange(heads, dtype=np.float64)))
    idx = np.arange(CHUNK, dtype=np.float64)
    rel = idx[:, None] - idx[None, :]
    dmask = np.where(rel >= 0, np.exp(np.maximum(rel, 0.0)[None] * log_g[:, None, None]), 0.0)
    dmask = np.concatenate(list(dmask), axis=1)
    lane_g = np.repeat(log_g, dk)[None, :]
    qdec = np.exp((idx[:, None] + 1.0) * lane_g)
    kdec = np.exp((CHUNK - 1.0 - idx)[:, None] * lane_g)
    sdec = np.exp(CHUNK * np.repeat(log_g, dk))[:, None] * np.ones((1, heads * dv))
    bmask = np.kron(np.eye(heads), np.ones((dk, dv)))
    f = lambda a: jnp.asarray(a, F32)
    return f(dmask), f(qdec), f(kdec), f(sdec), f(bmask)


def _rope_kernel(ang_ref, cos_ref, ssin_ref, *, half):
    ang = ang_ref[...]
    lane = lax.broadcasted_iota(jnp.int32, ang.shape, 1)
    cos_ref[...] = jnp.cos(ang)
    s = jnp.sin(ang)
    ssin_ref[...] = jnp.where((lane % (2 * half)) < half, -s, s)


def _rope_tables(ang, half):
    R, W = ang.shape
    tm = min(R, 2048)
    spec = pl.BlockSpec((tm, W), lambda i: (i, 0))
    return pl.pallas_call(
        functools.partial(_rope_kernel, half=half),
        grid=(R // tm,),
        in_specs=[spec],
        out_specs=[spec, spec],
        out_shape=[jax.ShapeDtypeStruct((R, W), F32)] * 2,
        compiler_params=_cparams(("parallel",)),
        name="rope_tables",
    )(ang)


def _ret_kernel(x_ref, cos_ref, ssin_ref, dmask_ref, qdec_ref, kdec_ref, sdec_ref, bmask_ref,
                o_ref, state_ref, *, heads, dk, dv, n_chunks):
    hk = heads * dk
    hv = heads * dv

    @pl.when(pl.program_id(1) == 0)
    def _():
        state_ref[...] = jnp.zeros_like(state_ref)

    lane = lax.broadcasted_iota(jnp.int32, (CHUNK, hk), 1)
    first_half = (lane % dk) < (dk // 2)

    def rotary(x, cos2, ssin):
        swapped = jnp.where(first_half, pltpu.roll(x, hk - dk // 2, 1), pltpu.roll(x, dk // 2, 1))
        return x * cos2 + swapped * ssin

    def chunk(ci, carry):
        r0 = pl.multiple_of(ci * CHUNK, CHUNK)
        rows = pl.ds(r0, CHUNK)
        reps = hk // LANE
        cos2 = jnp.concatenate([cos_ref[rows, :]] * reps, axis=1)
        ssin = jnp.concatenate([ssin_ref[rows, :]] * reps, axis=1)
        q = rotary(x_ref[rows, 0:hk], cos2, ssin)
        k = rotary(x_ref[rows, hk:2 * hk], cos2, ssin) * (dk ** -0.5)
        v = x_ref[rows, 2 * hk:2 * hk + hv]
        gate = x_ref[rows, 2 * hk + hv:2 * hk + 2 * hv]
        vb = v.astype(BF16)
        state = state_ref[...]
        cross = _dot((q * qdec_ref[...]).astype(BF16), state.astype(BF16))
        sb = (_dot(q.astype(BF16), _key_blocks(k, heads, dk)) * dmask_ref[...]).astype(BF16)
        outs = [_dot(sb[:, h * CHUNK:(h + 1) * CHUNK], vb[:, h * dv:(h + 1) * dv]) for h in range(heads)]
        o = jnp.concatenate(outs, axis=1) + cross
        kd = (k * kdec_ref[...]).astype(BF16)
        upd = _dot(kd.T, vb)
        state_ref[...] = sdec_ref[...] * state + bmask_ref[...] * upd
        y = jnp.concatenate(
            [_head_norm(o[:, h * dv:(h + 1) * dv]) for h in range(heads)], axis=1)
        o_ref[rows, :] = (y * _silu(gate)).astype(o_ref.dtype)
        return carry

    lax.fori_loop(0, n_chunks, chunk, 0, unroll=True)


def _retention(x, rope, B, S, heads, dk, dv):
    tt = min(S, SEQ_TILE)
    nt = S // tt
    win = 2 * heads * dk + 2 * heads * dv
    consts = _ret_consts(heads, dk, dv)
    kern = functools.partial(_ret_kernel, heads=heads, dk=dk, dv=dv, n_chunks=tt // CHUNK)
    return pl.pallas_call(
        kern,
        grid=(B, nt),
        in_specs=[pl.BlockSpec((tt, win), _seq_map(nt)),
                  pl.BlockSpec((tt, LANE), _seq_map(nt)), pl.BlockSpec((tt, LANE), _seq_map(nt))]
        + [_const_spec(c.shape) for c in consts],
        out_specs=pl.BlockSpec((tt, heads * dv), _seq_map(nt)),
        out_shape=jax.ShapeDtypeStruct((B * S, heads * dv), BF16),
        scratch_shapes=[pltpu.VMEM((heads * dk, heads * dv), F32)],
        compiler_params=_cparams(("parallel", "arbitrary")),
        name="retention",
    )(x, *rope, *consts)


def _gla_consts(heads, dk, dv):
    C = CHUNK
    n_lvl = int(math.log2(C))
    t = np.arange(C)
    sums = []
    masks = []
    for l in range(n_lvl):
        bit = (t >> l) & 1
        start = (t >> l) << l
        r = t[None, :]
        upper = (bit[:, None] == 1) & (r >= start[:, None]) & (r <= t[:, None])
        lower = (bit[:, None] == 0) & (r > t[:, None]) & (r < (start + (1 << l))[:, None])
        sums.append((upper | lower).astype(np.float64))
        m = ((bit[:, None] == 1) & (bit[None, :] == 0)
             & ((t[:, None] >> (l + 1)) == (t[None, :] >> (l + 1))))
        masks.append(np.tile(m.astype(np.float64), (1, heads)))
    masks.append(np.tile(np.eye(C), (1, heads)))
    sums.append((t[None, :] <= t[:, None]).astype(np.float64))
    sums.append((t[None, :] > t[:, None]).astype(np.float64))
    g = np.concatenate(sums, axis=0)
    gcat = np.concatenate([g, g], axis=1)
    bmask = np.kron(np.eye(heads), np.ones((dk, dv)))
    return jnp.asarray(gcat, BF16), jnp.asarray(np.stack(masks), F32), jnp.asarray(bmask, F32)


def _log_sigmoid(x):
    return jnp.minimum(x, 0.0) - jnp.log(1.0 + jnp.exp(-jnp.abs(x)))


def _gla_kernel(x_ref, wg_ref, bg_ref, gcat_ref, masks_ref, bmask_ref, o_ref, state_ref,
                *, heads, dk, dv, n_chunks):
    hk = heads * dk
    hv = heads * dv
    C = CHUNK
    n_lvl = int(math.log2(C))

    @pl.when(pl.program_id(1) == 0)
    def _():
        state_ref[...] = jnp.zeros_like(state_ref)

    key_blocks = functools.partial(_key_blocks, heads=heads, dk=dk)

    def chunk(ci, carry):
        r0 = pl.multiple_of(ci * C, C)
        rows = pl.ds(r0, C)
        q = x_ref[rows, 0:hk] * (dk ** -0.5)
        k = x_ref[rows, hk:2 * hk]
        v = x_ref[rows, 2 * hk:2 * hk + hv]
        gate = x_ref[rows, 2 * hk + hv:2 * hk + 2 * hv]
        code = x_ref[rows, 2 * hk + 2 * hv:2 * hk + 2 * hv + LANE]
        vb = v.astype(BF16)
        log_a = _log_sigmoid(_dot(code.astype(BF16), wg_ref[...]) + bg_ref[...]) * (1.0 / GLA_GATE_TEMP)
        hi = log_a.astype(BF16)
        lo = (log_a - hi.astype(F32)).astype(BF16)
        sums = _dot(gcat_ref[...], jnp.concatenate([hi, lo], axis=0))

        scores = masks_ref[n_lvl] * _dot(q.astype(BF16), key_blocks(k))
        for l in range(n_lvl):
            z = jnp.exp(sums[l * C:(l + 1) * C])
            scores = scores + masks_ref[l] * _dot((q * z).astype(BF16), key_blocks(k * z))
        sb = scores.astype(BF16)
        intra = jnp.concatenate(
            [_dot(sb[:, h * C:(h + 1) * C], vb[:, h * dv:(h + 1) * dv]) for h in range(heads)], axis=1)

        state = state_ref[...]
        e_pre = sums[n_lvl * C:(n_lvl + 1) * C]
        e_suf = sums[(n_lvl + 1) * C:(n_lvl + 2) * C]
        cross = _dot((q * jnp.exp(e_pre)).astype(BF16), state.astype(BF16))
        kd = (k * jnp.exp(e_suf)).astype(BF16)
        upd = _dot(kd.T, vb)
        total = jnp.broadcast_to(jnp.exp(e_pre[C - 1:C, :]), (LANE, hk))
        dec = jnp.concatenate([total.T] * (hv // LANE), axis=1)
        state_ref[...] = dec * state + bmask_ref[...] * upd

        o = intra + cross
        y = jnp.concatenate(
            [_head_norm(o[:, h * dv:(h + 1) * dv]) for h in range(heads)], axis=1)
        o_ref[rows, :] = (y * _silu(gate)).astype(o_ref.dtype)
        return carry

    lax.fori_loop(0, n_chunks, chunk, 0, unroll=True)


def _gla(x, wg, bg, B, S, heads, dk, dv):
    tt = min(S, SEQ_TILE)
    nt = S // tt
    win = 2 * heads * dk + 2 * heads * dv + LANE
    consts = _gla_consts(heads, dk, dv)
    kern = functools.partial(_gla_kernel, heads=heads, dk=dk, dv=dv, n_chunks=tt // CHUNK)
    return pl.pallas_call(
        kern,
        grid=(B, nt),
        in_specs=[pl.BlockSpec((tt, win), _seq_map(nt)),
                  _const_spec(wg.shape), _const_spec(bg.shape)]
        + [_const_spec(c.shape) for c in consts],
        out_specs=pl.BlockSpec((tt, heads * dv), _seq_map(nt)),
        out_shape=jax.ShapeDtypeStruct((B * S, heads * dv), BF16),
        scratch_shapes=[pltpu.VMEM((heads * dk, heads * dv), F32)],
        compiler_params=_cparams(("parallel", "arbitrary")),
        name="gla",
    )(x, wg, bg, *consts)


def _s5_kernel(u_ref, a_ref, bblk_ref, cblk_ref, d_ref, wglu_ref, o_ref,
               stage_ref, utb_ref, xs_ref, ytb_ref, st_ref, *, tt):
    nb, w = u_ref.shape[0], u_ref.shape[2]
    nblk = w // LANE
    sw = a_ref.shape[1] // (2 * nblk)

    @pl.when(pl.program_id(0) == 0)
    def _():
        st_ref[...] = jnp.zeros_like(st_ref)

    for b in range(nb):
        for m in range(nblk):
            stage_ref[m, b * S5_PITCH:b * S5_PITCH + tt, :] = u_ref[b, :, m * LANE:(m + 1) * LANE]

    def gather(t, carry):
        rows = pl.ds(pl.multiple_of(t * nb, nb), nb)
        for m in range(nblk):
            utb_ref[rows, m * LANE:(m + 1) * LANE] = stage_ref[m, pl.ds(t, nb, stride=S5_PITCH), :]
        return carry

    lax.fori_loop(0, tt, gather, 0)

    u = utb_ref[...]
    ub = u.astype(BF16)
    for m in range(nblk):
        xs_ref[:, 2 * sw * m:2 * sw * (m + 1)] = _dot(ub[:, m * LANE:(m + 1) * LANE], bblk_ref[m])

    def step(t, carry):
        rows = pl.ds(pl.multiple_of(t * nb, nb), nb)
        new = []
        for m in range(nblk):
            xr, xi = carry[2 * m], carry[2 * m + 1]
            re = slice(2 * sw * m, 2 * sw * m + sw)
            im = slice(2 * sw * m + sw, 2 * sw * (m + 1))
            ar, ai = a_ref[:, re], a_ref[:, im]
            nr = ar * xr - ai * xi + xs_ref[rows, re]
            ni = ar * xi + ai * xr + xs_ref[rows, im]
            xs_ref[rows, re] = nr
            xs_ref[rows, im] = ni
            new += [nr, ni]
        return tuple(new)

    init = tuple(st_ref[:, sw * c:sw * (c + 1)] for c in range(2 * nblk))
    last = lax.fori_loop(0, tt, step, init, unroll=2)
    for c in range(2 * nblk):
        st_ref[:, sw * c:sw * (c + 1)] = last[c]

    ys = [_dot(xs_ref[:, 2 * sw * m:2 * sw * (m + 1)].astype(BF16), cblk_ref[m]) for m in range(nblk)]
    y = jnp.concatenate(ys, axis=1) + d_ref[...] * u
    y = jax.nn.gelu(y)
    y = y * jax.nn.sigmoid(_dot(y.astype(BF16), wglu_ref[...]))

    for m in range(nblk):
        ytb_ref[m] = y[:, m * LANE:(m + 1) * LANE]
    for b in range(nb):
        for m in range(nblk):
            o_ref[b, :, m * LANE:(m + 1) * LANE] = (
                ytb_ref[m, pl.ds(b, tt, stride=nb), :].astype(o_ref.dtype))


def _s5(u3, a_row, bblk, cblk, d_row, wglu, layer):
    B, S, W = u3.shape
    tt = min(S, S5_TILE)
    ns2 = a_row.shape[1]
    nblk = W // LANE
    a8 = jnp.broadcast_to(a_row, (B, ns2))
    return pl.pallas_call(
        functools.partial(_s5_kernel, tt=tt),
        grid=(S // tt,),
        in_specs=[pl.BlockSpec((B, tt, W), lambda t: (0, t, 0)),
                  _const_spec(a8.shape), _layer_spec(bblk, layer), _layer_spec(cblk, layer),
                  _const_spec(d_row.shape), _layer_spec(wglu, layer)],
        out_specs=pl.BlockSpec((B, tt, W), lambda t: (0, t, 0)),
        out_shape=jax.ShapeDtypeStruct((B, S, W), BF16),
        scratch_shapes=[pltpu.VMEM((nblk, B * S5_PITCH, LANE), F32),
                        pltpu.VMEM((tt * B, W), F32),
                        pltpu.VMEM((tt * B, ns2), F32),
                        pltpu.VMEM((nblk, tt * B, LANE), F32),
                        pltpu.VMEM((B, ns2), F32)],
        compiler_params=_cparams(("arbitrary",)),
        name="s5",
    )(u3, a8, bblk, cblk, d_row, wglu)


def _s5_params(a_re, a_im, log_dt, b_re, b_im, c_re, c_im):
    G, P = a_re.shape
    n_c = b_re.shape[-1]
    gpb = LANE // n_c
    nblk = G // gpb
    dt = jnp.exp(log_dt)[:, None]
    mag = jnp.exp(a_re * dt)
    abar_re = mag * jnp.cos(a_im * dt)
    abar_im = mag * jnp.sin(a_im * dt)
    den = a_re * a_re + a_im * a_im
    nr, ni = abar_re - 1.0, abar_im
    f_re = (nr * a_re + ni * a_im) / den
    f_im = (ni * a_re - nr * a_im) / den
    bb_re = f_re[..., None] * b_re - f_im[..., None] * b_im
    bb_im = f_re[..., None] * b_im + f_im[..., None] * b_re
    eye = jnp.eye(gpb, dtype=F32)

    def bd_in(m):
        m = m.reshape(nblk, gpb, P, n_c)
        return jnp.einsum('ngpc,gh->ngchp', m, eye).reshape(nblk, gpb * n_c, gpb * P)

    def bd_out(m):
        m = m.reshape(nblk, gpb, n_c, P)
        return jnp.einsum('ngcp,gh->ngphc', m, eye).reshape(nblk, gpb * P, gpb * n_c)

    a_row = jnp.concatenate([abar_re.reshape(nblk, 1, gpb * P), abar_im.reshape(nblk, 1, gpb * P)],
                            axis=2).reshape(1, 2 * G * P)
    bblk = jnp.concatenate([bd_in(bb_re), bd_in(bb_im)], axis=2).astype(BF16)
    cblk = jnp.concatenate([bd_out(c_re), -bd_out(c_im)], axis=1).astype(BF16)
    return a_row, bblk, cblk


def _pool_kernel(p_ref, w_ref, sc_ref, o_ref, ext_ref, *, tt):
    pad = POOL_PAD
    t_idx = pl.program_id(1)

    @pl.when(t_idx == 0)
    def _():
        ext_ref[0:pad, :] = jnp.zeros((pad, ext_ref.shape[1]), F32)

    ext_ref[pad:pad + tt, :] = p_ref[...]
    pos1 = (lax.broadcasted_iota(jnp.int32, (tt, LANE), 0) + t_idx * tt + 1).astype(F32)
    outs = []
    for g, win in enumerate(POOL_WINDOWS):
        cols = slice(g * LANE, (g + 1) * LANE)
        acc = ext_ref[:, cols]
        k = 1
        while k < win:
            acc = acc + pltpu.roll(acc, k, 0)
            k *= 2
        acc = acc[pad:pad + tt]
        cur = ext_ref[pad:pad + tt, cols]
        mixed = acc / jnp.minimum(pos1, float(win)) - cur
        outs.append(_dot(mixed.astype(BF16), w_ref[g]))
    o_ref[...] = (jnp.concatenate(outs, axis=1) * sc_ref[...]).astype(o_ref.dtype)
    ext_ref[0:pad, :] = ext_ref[tt:tt + pad, :]


def _pool(p, w, scale, B, S):
    tt = min(S, SEQ_TILE)
    nt = S // tt
    W = w.shape[0] * LANE
    return pl.pallas_call(
        functools.partial(_pool_kernel, tt=tt),
        grid=(B, nt),
        in_specs=[pl.BlockSpec((tt, W), _seq_map(nt)),
                  _const_spec(w.shape), _const_spec(scale.shape)],
        out_specs=pl.BlockSpec((tt, W), _seq_map(nt)),
        out_shape=jax.ShapeDtypeStruct((B * S, W), BF16),
        scratch_shapes=[pltpu.VMEM((tt + POOL_PAD, W), F32)],
        compiler_params=_cparams(("parallel", "arbitrary")),
        name="pool",
    )(p, w, scale)


def _merge_kernel(hn_ref, *refs):
    ys = refs[0:N_BRANCH]
    wgates = refs[N_BRANCH:2 * N_BRANCH]
    wbrs = refs[2 * N_BRANCH:3 * N_BRANCH]
    o_ref = refs[3 * N_BRANCH]
    tm = o_ref.shape[0]
    for rb in range(tm // MERGE_SUB):
        rows = slice(rb * MERGE_SUB, (rb + 1) * MERGE_SUB)
        hn = hn_ref[rows, :]
        acc = None
        for n in range(N_BRANCH):
            logits = _dot(hn, wgates[n][...])
            term = jax.nn.sigmoid(logits) * _dot(ys[n][rows, :], wbrs[n][...])
            acc = term if acc is None else acc + term
        o_ref[rows, :] = acc.astype(o_ref.dtype)


def _merge(hn, ys, wgate, wbr, layer):
    R, D = hn.shape
    tm = min(R, MERGE_ROW_TILE)
    bw = ys[0].shape[1]
    tn = MERGE_TILE
    nc = D // tn
    gate_spec = lambda n: pl.BlockSpec((None, D, tn), lambda i, c: (layer, 0, n * nc + c))
    br_spec = lambda n: pl.BlockSpec((None, None, bw, tn), lambda i, c: (layer, n, 0, c))
    return pl.pallas_call(
        _merge_kernel,
        grid=(R // tm, nc),
        in_specs=[pl.BlockSpec((tm, D), lambda i, c: (i, 0))]
        + [pl.BlockSpec((tm, bw), lambda i, c: (i, 0)) for _ in ys]
        + [gate_spec(n) for n in range(N_BRANCH)] + [br_spec(n) for n in range(N_BRANCH)],
        out_specs=pl.BlockSpec((tm, tn), lambda i, c: (i, c)),
        out_shape=jax.ShapeDtypeStruct((R, D), BF16),
        compiler_params=_cparams(("parallel", "arbitrary")),
        name="branch_merge",
    )(hn, *ys, *([wgate] * N_BRANCH), *([wbr] * N_BRANCH))


def _outproj_kernel(h_ref, m_ref, w_ref, g_ref, o_ref, hn_ref):
    for rb in range(h_ref.shape[0] // OUT_SUB):
        rows = slice(rb * OUT_SUB, (rb + 1) * OUT_SUB)
        h = h_ref[rows, :] + _dot(m_ref[rows, :], w_ref[...])
        o_ref[rows, :] = h
        hn_ref[rows, :] = _rms(h, g_ref[...]).astype(BF16)


def _outproj(h, merged, w, g, layer):
    R, D = h.shape
    tm = min(R, ROW_TILE)
    row_spec = pl.BlockSpec((tm, D), lambda i: (i, 0))
    return pl.pallas_call(
        _outproj_kernel,
        grid=(R // tm,),
        in_specs=[row_spec, row_spec, _layer_spec(w, layer), _const_spec((1, D))],
        out_specs=[row_spec, row_spec],
        out_shape=[jax.ShapeDtypeStruct((R, D), F32), jax.ShapeDtypeStruct((R, D), BF16)],
        compiler_params=_cparams(("parallel",)),
        name="out_proj",
    )(h, merged, w, g.reshape(1, D))


def _ffn_up_kernel(hn_ref, wa_ref, wv_ref, cw_ref, cb_ref, o_ref, carry_ref, *, tn, tiles_per_seq):
    i = pl.program_id(0)
    j = pl.program_id(1)
    tm = hn_ref.shape[0]

    @pl.when(i % tiles_per_seq == 0)
    def _():
        carry_ref[j] = jnp.zeros((SUBLANE, 2 * tn), F32)

    row8 = lax.broadcasted_iota(jnp.int32, (SUBLANE, FF_COL), 0)

    def delayed(u, prev8, k):
        r = pltpu.roll(u, k, 0)
        top = jnp.where(row8 < k, pltpu.roll(prev8, k, 0), r[0:SUBLANE])
        return jnp.concatenate([top, r[SUBLANE:]], axis=0)

    def conv(u, prev8, col0):
        cols = slice(col0, col0 + FF_COL)
        out = cb_ref[:, cols] + u * cw_ref[CONV_W - 1:CONV_W, cols]
        for tap in range(CONV_W - 1):
            out = out + delayed(u, prev8, CONV_W - 1 - tap) * cw_ref[tap:tap + 1, cols]
        return out

    def up(rb, ch):
        hn = hn_ref[rb * FF_SUB:(rb + 1) * FF_SUB, :]
        cols = slice(ch * FF_COL, (ch + 1) * FF_COL)
        return _dot(hn, wa_ref[:, cols]), _dot(hn, wv_ref[:, cols])

    n_col = tn // FF_COL
    units = [(rb, ch) for rb in range(tm // FF_SUB) for ch in range(n_col)]
    prev = carry_ref[j]
    prev_a = [prev[:, ch * FF_COL:(ch + 1) * FF_COL] for ch in range(n_col)]
    prev_v = [prev[:, tn + ch * FF_COL:tn + (ch + 1) * FF_COL] for ch in range(n_col)]
    nxt = up(*units[0])
    for idx, (rb, ch) in enumerate(units):
        ua, uv = nxt
        if idx + 1 < len(units):
            nxt = up(*units[idx + 1])
        a = conv(ua, prev_a[ch], ch * FF_COL)
        v = conv(uv, prev_v[ch], tn + ch * FF_COL)
        prev_a[ch], prev_v[ch] = ua[FF_SUB - SUBLANE:], uv[FF_SUB - SUBLANE:]
        o_ref[rb * FF_SUB:(rb + 1) * FF_SUB, ch * FF_COL:(ch + 1) * FF_COL] = (
            (_silu(a) * v).astype(o_ref.dtype))
    carry_ref[j] = jnp.concatenate(prev_a + prev_v, axis=1)


def _ffn_up(hn, wa, wv, cw, cb, S, layer):
    R, D = hn.shape
    tm = min(S, FF_ROW_TILE)
    tn = FF_TILE
    ffp = wa.shape[2]
    nj = ffp // tn
    return pl.pallas_call(
        functools.partial(_ffn_up_kernel, tn=tn, tiles_per_seq=S // tm),
        grid=(R // tm, nj),
        in_specs=[pl.BlockSpec((tm, D), lambda i, j: (i, 0)),
                  pl.BlockSpec((None, D, tn), lambda i, j: (layer, 0, j)),
                  pl.BlockSpec((None, D, tn), lambda i, j: (layer, 0, j)),
                  pl.BlockSpec((None, None, CONV_W, 2 * tn), lambda i, j: (layer, j, 0, 0)),
                  pl.BlockSpec((None, None, 1, 2 * tn), lambda i, j: (layer, j, 0, 0))],
        out_specs=pl.BlockSpec((tm, tn), lambda i, j: (i, j)),
        out_shape=jax.ShapeDtypeStruct((R, ffp), BF16),
        scratch_shapes=[pltpu.VMEM((nj, SUBLANE, 2 * tn), F32)],
        compiler_params=_cparams(("arbitrary", "arbitrary")),
        name="ffn_up",
    )(hn, wa, wv, cw, cb)


def _ffn_down_kernel(h_ref, act_ref, wd_ref, g_ref, o_ref, *, final):
    for rb in range(h_ref.shape[0] // OUT_SUB):
        rows = slice(rb * OUT_SUB, (rb + 1) * OUT_SUB)
        h = h_ref[rows, :] + _dot(act_ref[rows, :], wd_ref[...])
        o_ref[rows, :] = _rms(h, g_ref[...]) if final else h


def _ffn_down(h, act, wd, g, final, layer):
    R, D = h.shape
    tm = min(R, ROW_TILE)
    ff = wd.shape[1]
    row_spec = pl.BlockSpec((tm, D), lambda i: (i, 0))
    return pl.pallas_call(
        functools.partial(_ffn_down_kernel, final=final),
        grid=(R // tm,),
        in_specs=[row_spec, pl.BlockSpec((tm, ff), lambda i: (i, 0)),
                  _layer_spec(wd, layer), _const_spec((1, D))],
        out_specs=row_spec,
        out_shape=jax.ShapeDtypeStruct((R, D), F32),
        compiler_params=_cparams(("parallel",)),
        name="ffn_down",
    )(h, act, wd, g.reshape(1, D))


def _pad_cols(w, n):
    return jnp.pad(w, ((0, 0), (0, n - w.shape[1])))


def _split_w_in(w_in, D):
    bw = D // 4
    hk = bw // 2
    sizes = [hk, hk, bw, bw, bw, hk, hk, bw, bw, GLA_RANK, bw, N_BRANCH * D]
    offs = np.concatenate([[0], np.cumsum(sizes)])
    seg = lambda a, b: w_in[:, offs[a]:offs[b]]
    w_ret = seg(0, 4)
    w_s5 = seg(4, 5)
    w_gla = jnp.concatenate([seg(5, 9), _pad_cols(seg(9, 10), LANE)], axis=1)
    w_pool = seg(10, 11)
    return [w.astype(BF16) for w in (w_ret, w_s5, w_gla, w_pool)], seg(11, 12).astype(BF16)


def _ffn_weights(w_up, conv_w, conv_b, w_down, tn):
    D, ff2 = w_up.shape
    ff = ff2 // 2
    ffp = -(-ff // tn) * tn
    nj = ffp // tn
    wa = _pad_cols(w_up[:, :ff], ffp).astype(BF16)
    wv = _pad_cols(w_up[:, ff:], ffp).astype(BF16)
    tiles = lambda m: _pad_cols(m, ffp).reshape(m.shape[0], nj, tn)
    cw = jnp.concatenate([tiles(conv_w[:, :ff]), tiles(conv_w[:, ff:])], axis=2).transpose(1, 0, 2)
    cbr = conv_b.reshape(1, ff2)
    cb = jnp.concatenate([tiles(cbr[:, :ff]), tiles(cbr[:, ff:])], axis=2).transpose(1, 0, 2)
    return wa, wv, cw, cb, w_down.astype(BF16)


def kernel(x, positions, norm_mix_g, w_in, s5_a_re, s5_a_im, s5_log_dt, s5_b_re, s5_b_im, s5_c_re, s5_c_im, s5_d, s5_w_glu, gla_w_gate, gla_b_gate, pool_w, pool_scale, w_branch, w_out, norm_ffn_g, w_up, conv_w, conv_b, w_down, final_g):
    B, S, D = x.shape
    depth = w_in.shape[0]
    bw = D // 4
    ret_dk, ret_dv = bw // (2 * RET_HEADS), bw // RET_HEADS
    gla_dk, gla_dv = bw // (2 * GLA_HEADS), bw // GLA_HEADS
    assert S % CHUNK == 0 and S % min(S, ROW_TILE) == 0 and B == SUBLANE
    assert ret_dk // 2 * 4 == LANE and bw == S5_BLOCK * LANE

    inv = ROPE_BASE ** (-jnp.arange(0, ret_dk, 2, dtype=F32) / ret_dk)
    ang = positions.astype(F32)[..., None] * inv
    ang = jnp.tile(ang, (1, 1, LANE // (ret_dk // 2))).reshape(B * S, LANE)
    rope = _rope_tables(ang, ret_dk // 2)

    ws, w_gate = jax.vmap(lambda w: _split_w_in(w, D))(w_in)
    a_row, bblk, cblk = jax.vmap(_s5_params)(s5_a_re, s5_a_im, s5_log_dt, s5_b_re, s5_b_im,
                                             s5_c_re, s5_c_im)
    wglu = s5_w_glu.astype(BF16)
    wg = jnp.pad(gla_w_gate, ((0, 0), (0, LANE - GLA_RANK), (0, 0))).astype(BF16)
    pw = pool_w.astype(BF16)
    wbr = w_branch.astype(BF16)
    wo = w_out.astype(BF16)
    wa, wv, cw, cb, wd = jax.vmap(lambda a, b, c, d: _ffn_weights(a, b, c, d, FF_TILE))(
        w_up, conv_w, conv_b, w_down)

    h = x.reshape(B * S, D)
    for l in range(depth):
        hn, p_ret, p_s5, p_gla, p_pool = _proj(h, norm_mix_g[l], ws, l)
        ya = _retention(p_ret, rope, B, S, RET_HEADS, ret_dk, ret_dv)
        yb = _s5(p_s5.reshape(B, S, bw), a_row[l], bblk, cblk, s5_d[l].reshape(1, bw),
                 wglu, l).reshape(B * S, bw)
        yc = _gla(p_gla, wg[l], gla_b_gate[l].reshape(1, -1), B, S, GLA_HEADS, gla_dk, gla_dv)
        yd = _pool(p_pool, pw[l], pool_scale[l].reshape(1, bw), B, S)
        merged = _merge(hn, [ya, yb, yc, yd], w_gate, wbr, l)
        h, hn = _outproj(h, merged, wo, norm_ffn_g[l], l)
        act = _ffn_up(hn, wa, wv, cw, cb, S, l)
        h = _ffn_down(h, act, wd, final_g, final=(l == depth - 1), layer=l)
    return h.reshape(B, S, D)
```

```python
import functools
import math

import numpy as np
import jax
import jax.numpy as jnp
from jax import lax
from jax.experimental import pallas as pl
from jax.experimental.pallas import tpu as pltpu

F32 = jnp.float32
BF16 = jnp.bfloat16
EPS = 1e-6

N_BRANCH = 4
RET_HEADS = 4
GLA_HEADS = 4
GLA_RANK = 16
GLA_GATE_TEMP = 16.0
S5_GROUP = 16
S5_STATE = 64
POOL_WINDOWS = (2, 4, 8, 16)
POOL_PAD = 16
ROPE_BASE = 10000.0
CONV_W = 3

LANE = 128
SUBLANE = 8
CHUNK = 128
VMEM_LIMIT = 56 * 1024 * 1024
ROW_TILE = 512
MERGE_ROW_TILE = 1024
MERGE_TILE = 512
MERGE_SUB = 256
OUT_SUB = 256
FF_ROW_TILE = 2048
FF_TILE = 512
FF_COL = 512
FF_SUB = 512
SEQ_TILE = 512
S5_TILE = 128
S5_PITCH = S5_TILE + 4
S5_BLOCK = 4


def _cparams(sem):
    return pltpu.CompilerParams(dimension_semantics=sem, vmem_limit_bytes=VMEM_LIMIT)


def _const_spec(shape):
    nd = len(shape)
    return pl.BlockSpec(shape, lambda *_: (0,) * nd, pipeline_mode=pl.Buffered(1))


def _layer_spec(arr, layer):
    nd = arr.ndim - 1
    return pl.BlockSpec((None,) + arr.shape[1:], lambda *_: (layer,) + (0,) * nd,
                        pipeline_mode=pl.Buffered(1))


def _dot(a, b):
    return jnp.dot(a, b, preferred_element_type=F32)


def _dot_nt(a, b):
    return lax.dot_general(a, b, (((1,), (1,)), ((), ())), preferred_element_type=F32)


def _rms(x, g):
    r = lax.rsqrt(jnp.mean(x * x, axis=-1, keepdims=True) + EPS)
    return x * r * g


def _head_norm(o):
    mu = jnp.mean(o, axis=-1, keepdims=True)
    d = o - mu
    var = jnp.mean(d * d, axis=-1, keepdims=True)
    return d * lax.rsqrt(var + EPS)


def _silu(x):
    return x * jax.nn.sigmoid(x)


def _key_blocks(k, heads, dk):
    kt = k.T.astype(BF16)
    zero = jnp.zeros((dk, kt.shape[1]), BF16)
    cols = [jnp.concatenate([kt[h * dk:(h + 1) * dk] if r == h else zero for r in range(heads)], axis=0)
            for h in range(heads)]
    return jnp.concatenate(cols, axis=1)


def _seq_map(nt):
    return lambda b, t: (b * nt + t, 0)


def _proj_kernel(h_ref, g_ref, w_ret, w_s5, w_gla, w_pool, o_hn, o_ret, o_s5, o_gla, o_pool):
    hn = _rms(h_ref[...], g_ref[...]).astype(BF16)
    o_hn[...] = hn
    o_ret[...] = _dot(hn, w_ret[...])
    o_s5[...] = _dot(hn, w_s5[...])
    o_gla[...] = _dot(hn, w_gla[...])
    o_pool[...] = _dot(hn, w_pool[...])


def _proj(h, g, ws, layer):
    R, D = h.shape
    tm = min(R, ROW_TILE)
    widths = [w.shape[2] for w in ws]
    return pl.pallas_call(
        _proj_kernel,
        grid=(R // tm,),
        in_specs=[pl.BlockSpec((tm, D), lambda i: (i, 0)), _const_spec((1, D))]
        + [_layer_spec(w, layer) for w in ws],
        out_specs=[pl.BlockSpec((tm, n), lambda i: (i, 0)) for n in [D] + widths],
        out_shape=[jax.ShapeDtypeStruct((R, D), BF16)]
        + [jax.ShapeDtypeStruct((R, n), F32) for n in widths],
        compiler_params=_cparams(("parallel",)),
        name="mixer_proj",
    )(h, g.reshape(1, D), *ws)


def _ret_consts(heads, dk, dv):
    log_g = np.log(1.0 - 2.0 ** (-5.0 - np.arange(heads, dtype=np.float64)))
    idx = np.arange(CHUNK, dtype=np.float64)
    rel = idx[:, None] - idx[None, :]
    dmask = np.where(rel >= 0, np.exp(np.maximum(rel, 0.0)[None] * log_g[:, None, None]), 0.0)
    dmask = np.concatenate(list(dmask), axis=1)
    lane_g = np.repeat(log_g, dk)[None, :]
    qdec = np.exp((idx[:, None] + 1.0) * lane_g)
    kdec = np.exp((CHUNK - 1.0 - idx)[:, None] * lane_g)
    sdec = np.exp(CHUNK * np.repeat(log_g, dk))[:, None] * np.ones((1, heads * dv))
    bmask = np.kron(np.eye(heads), np.ones((dk, dv)))
    f = lambda a: jnp.asarray(a, F32)
    return f(dmask), f(qdec), f(kdec), f(sdec), f(bmask)


def _rope_kernel(ang_ref, cos_ref, ssin_ref, *, half):
    ang = ang_ref[...]
    lane = lax.broadcasted_iota(jnp.int32, ang.shape, 1)
    cos_ref[...] = jnp.cos(ang)
    s = jnp.sin(ang)
    ssin_ref[...] = jnp.where((lane % (2 * half)) < half, -s, s)


def _rope_tables(ang, half):
    R, W = ang.shape
    tm = min(R, 2048)
    spec = pl.BlockSpec((tm, W), lambda i: (i, 0))
    return pl.pallas_call(
        functools.partial(_rope_kernel, half=half),
        grid=(R // tm,),
        in_specs=[spec],
        out_specs=[spec, spec],
        out_shape=[jax.ShapeDtypeStruct((R, W), F32)] * 2,
        compiler_params=_cparams(("parallel",)),
        name="rope_tables",
    )(ang)


def _ret_body(x_ref, cos_ref, ssin_ref, dmask_ref, qdec_ref, kdec_ref, sdec_ref, bmask_ref,
              o_ref, state_ref, *, heads, dk, dv, n_chunks):
    hk = heads * dk
    hv = heads * dv
    lane = lax.broadcasted_iota(jnp.int32, (CHUNK, hk), 1)
    first_half = (lane % dk) < (dk // 2)

    def rotary(x, cos2, ssin):
        swapped = jnp.where(first_half, pltpu.roll(x, hk - dk // 2, 1), pltpu.roll(x, dk // 2, 1))
        return x * cos2 + swapped * ssin

    def chunk(ci, carry):
        r0 = pl.multiple_of(ci * CHUNK, CHUNK)
        rows = pl.ds(r0, CHUNK)
        reps = hk // LANE
        cos2 = jnp.concatenate([cos_ref[rows, :]] * reps, axis=1)
        ssin = jnp.concatenate([ssin_ref[rows, :]] * reps, axis=1)
        q = rotary(x_ref[rows, 0:hk], cos2, ssin)
        k = rotary(x_ref[rows, hk:2 * hk], cos2, ssin) * (dk ** -0.5)
        v = x_ref[rows, 2 * hk:2 * hk + hv]
        gate = x_ref[rows, 2 * hk + hv:2 * hk + 2 * hv]
        vb = v.astype(BF16)
        state = state_ref[...]
        cross = _dot((q * qdec_ref[...]).astype(BF16), state.astype(BF16))
        sb = (_dot(q.astype(BF16), _key_blocks(k, heads, dk)) * dmask_ref[...]).astype(BF16)
        outs = [_dot(sb[:, h * CHUNK:(h + 1) * CHUNK], vb[:, h * dv:(h + 1) * dv]) for h in range(heads)]
        o = jnp.concatenate(outs, axis=1) + cross
        kd = (k * kdec_ref[...]).astype(BF16)
        upd = _dot(kd.T, vb)
        state_ref[...] = sdec_ref[...] * state + bmask_ref[...] * upd
        y = jnp.concatenate(
            [_head_norm(o[:, h * dv:(h + 1) * dv]) for h in range(heads)], axis=1)
        o_ref[rows, :] = (y * _silu(gate)).astype(o_ref.dtype)
        return carry

    lax.fori_loop(0, n_chunks, chunk, 0, unroll=True)


def _ret_pool_kernel(x_ref, cos_ref, ssin_ref, dmask_ref, qdec_ref, kdec_ref, sdec_ref, bmask_ref,
                     p_ref, pw_ref, psc_ref, o_ref, po_ref, state_ref, ext_ref,
                     *, heads, dk, dv, n_chunks, tt):
    @pl.when(pl.program_id(1) == 0)
    def _():
        state_ref[...] = jnp.zeros_like(state_ref)
        ext_ref[0:POOL_PAD, :] = jnp.zeros((POOL_PAD, ext_ref.shape[1]), F32)

    _pool_body(p_ref, pw_ref, psc_ref, po_ref, ext_ref, tt=tt)
    _ret_body(x_ref, cos_ref, ssin_ref, dmask_ref, qdec_ref, kdec_ref, sdec_ref, bmask_ref,
              o_ref, state_ref, heads=heads, dk=dk, dv=dv, n_chunks=n_chunks)


def _retention_pool(x, rope, p, pw, pscale, B, S, heads, dk, dv):
    tt = min(S, SEQ_TILE)
    nt = S // tt
    win = 2 * heads * dk + 2 * heads * dv
    pwid = pw.shape[0] * LANE
    consts = _ret_consts(heads, dk, dv)
    kern = functools.partial(_ret_pool_kernel, heads=heads, dk=dk, dv=dv, n_chunks=tt // CHUNK, tt=tt)
    seq = lambda width: pl.BlockSpec((tt, width), _seq_map(nt))
    return pl.pallas_call(
        kern,
        grid=(B, nt),
        in_specs=[seq(win), seq(LANE), seq(LANE)] + [_const_spec(c.shape) for c in consts]
        + [seq(pwid), _const_spec(pw.shape), _const_spec(pscale.shape)],
        out_specs=[seq(heads * dv), seq(pwid)],
        out_shape=[jax.ShapeDtypeStruct((B * S, heads * dv), BF16),
                   jax.ShapeDtypeStruct((B * S, pwid), BF16)],
        scratch_shapes=[pltpu.VMEM((heads * dk, heads * dv), F32),
                        pltpu.VMEM((tt + POOL_PAD, pwid), F32)],
        compiler_params=_cparams(("parallel", "arbitrary")),
        name="retention_pool",
    )(x, *rope, *consts, p, pw, pscale)


def _gla_consts(heads, dk, dv):
    C = CHUNK
    n_lvl = int(math.log2(C))
    t = np.arange(C)
    sums = []
    masks = []
    for l in range(n_lvl):
        bit = (t >> l) & 1
        start = (t >> l) << l
        r = t[None, :]
        upper = (bit[:, None] == 1) & (r >= start[:, None]) & (r <= t[:, None])
        lower = (bit[:, None] == 0) & (r > t[:, None]) & (r < (start + (1 << l))[:, None])
        sums.append((upper | lower).astype(np.float64))
        m = ((bit[:, None] == 1) & (bit[None, :] == 0)
             & ((t[:, None] >> (l + 1)) == (t[None, :] >> (l + 1))))
        masks.append(np.tile(m.astype(np.float64), (1, heads)))
    masks.append(np.tile(np.eye(C), (1, heads)))
    sums.append((t[None, :] <= t[:, None]).astype(np.float64))
    sums.append((t[None, :] > t[:, None]).astype(np.float64))
    g = np.concatenate(sums, axis=0)
    gcat = np.concatenate([g, g], axis=1)
    bmask = np.kron(np.eye(heads), np.ones((dk, dv)))
    return jnp.asarray(gcat, BF16), jnp.asarray(np.stack(masks), F32), jnp.asarray(bmask, F32)


def _log_sigmoid(x):
    return jnp.minimum(x, 0.0) - jnp.log(1.0 + jnp.exp(-jnp.abs(x)))


def _gla_kernel(x_ref, wg_ref, bg_ref, gcat_ref, masks_ref, bmask_ref, o_ref, state_ref,
                *, heads, dk, dv, n_chunks):
    hk = heads * dk
    hv = heads * dv
    C = CHUNK
    n_lvl = int(math.log2(C))

    @pl.when(pl.program_id(1) == 0)
    def _():
        state_ref[...] = jnp.zeros_like(state_ref)

    key_blocks = functools.partial(_key_blocks, heads=heads, dk=dk)

    def chunk(ci, carry):
        r0 = pl.multiple_of(ci * C, C)
        rows = pl.ds(r0, C)
        q = x_ref[rows, 0:hk] * (dk ** -0.5)
        k = x_ref[rows, hk:2 * hk]
        v = x_ref[rows, 2 * hk:2 * hk + hv]
        gate = x_ref[rows, 2 * hk + hv:2 * hk + 2 * hv]
        code = x_ref[rows, 2 * hk + 2 * hv:2 * hk + 2 * hv + LANE]
        vb = v.astype(BF16)
        log_a = _log_sigmoid(_dot(code.astype(BF16), wg_ref[...]) + bg_ref[...]) * (1.0 / GLA_GATE_TEMP)
        hi = log_a.astype(BF16)
        lo = (log_a - hi.astype(F32)).astype(BF16)
        sums = _dot(gcat_ref[...], jnp.concatenate([hi, lo], axis=0))

        scores = masks_ref[n_lvl] * _dot(q.astype(BF16), key_blocks(k))
        for l in range(n_lvl):
            z = jnp.exp(sums[l * C:(l + 1) * C])
            scores = scores + masks_ref[l] * _dot((q * z).astype(BF16), key_blocks(k * z))
        sb = scores.astype(BF16)
        intra = jnp.concatenate(
            [_dot(sb[:, h * C:(h + 1) * C], vb[:, h * dv:(h + 1) * dv]) for h in range(heads)], axis=1)

        state = state_ref[...]
        e_pre = sums[n_lvl * C:(n_lvl + 1) * C]
        e_suf = sums[(n_lvl + 1) * C:(n_lvl + 2) * C]
        cross = _dot((q * jnp.exp(e_pre)).astype(BF16), state.astype(BF16))
        kd = (k * jnp.exp(e_suf)).astype(BF16)
        upd = _dot(kd.T, vb)
        total = jnp.broadcast_to(jnp.exp(e_pre[C - 1:C, :]), (LANE, hk))
        dec = jnp.concatenate([total.T] * (hv // LANE), axis=1)
        state_ref[...] = dec * state + bmask_ref[...] * upd

        o = intra + cross
        y = jnp.concatenate(
            [_head_norm(o[:, h * dv:(h + 1) * dv]) for h in range(heads)], axis=1)
        o_ref[rows, :] = (y * _silu(gate)).astype(o_ref.dtype)
        return carry

    lax.fori_loop(0, n_chunks, chunk, 0)


def _gla(x, wg, bg, B, S, heads, dk, dv):
    tt = min(S, SEQ_TILE)
    nt = S // tt
    win = 2 * heads * dk + 2 * heads * dv + LANE
    consts = _gla_consts(heads, dk, dv)
    kern = functools.partial(_gla_kernel, heads=heads, dk=dk, dv=dv, n_chunks=tt // CHUNK)
    return pl.pallas_call(
        kern,
        grid=(B, nt),
        in_specs=[pl.BlockSpec((tt, win), _seq_map(nt)),
                  _const_spec(wg.shape), _const_spec(bg.shape)]
        + [_const_spec(c.shape) for c in consts],
        out_specs=pl.BlockSpec((tt, heads * dv), _seq_map(nt)),
        out_shape=jax.ShapeDtypeStruct((B * S, heads * dv), BF16),
        scratch_shapes=[pltpu.VMEM((heads * dk, heads * dv), F32)],
        compiler_params=_cparams(("parallel", "arbitrary")),
        name="gla",
    )(x, wg, bg, *consts)


def _s5_kernel(u_ref, a_ref, bblk_ref, cblk_ref, d_ref, wglu_ref, o_ref,
               stage_ref, utb_ref, xs_ref, ytb_ref, st_ref, *, tt):
    nb, w = u_ref.shape[0], u_ref.shape[2]
    nblk = w // LANE
    sw = a_ref.shape[1] // (2 * nblk)

    @pl.when(pl.program_id(0) == 0)
    def _():
        st_ref[...] = jnp.zeros_like(st_ref)

    for b in range(nb):
        for m in range(nblk):
            stage_ref[m, b * S5_PITCH:b * S5_PITCH + tt, :] = u_ref[b, :, m * LANE:(m + 1) * LANE]

    def gather(t, carry):
        rows = pl.ds(pl.multiple_of(t * nb, nb), nb)
        for m in range(nblk):
            utb_ref[rows, m * LANE:(m + 1) * LANE] = stage_ref[m, pl.ds(t, nb, stride=S5_PITCH), :]
        return carry

    lax.fori_loop(0, tt, gather, 0)

    u = utb_ref[...]
    ub = u.astype(BF16)
    for m in range(nblk):
        xs_ref[:, 2 * sw * m:2 * sw * (m + 1)] = _dot(ub[:, m * LANE:(m + 1) * LANE], bblk_ref[m])

    def step(t, carry):
        rows = pl.ds(pl.multiple_of(t * nb, nb), nb)
        new = []
        for m in range(nblk):
            xr, xi = carry[2 * m], carry[2 * m + 1]
            re = slice(2 * sw * m, 2 * sw * m + sw)
            im = slice(2 * sw * m + sw, 2 * sw * (m + 1))
            ar, ai = a_ref[:, re], a_ref[:, im]
            nr = ar * xr - ai * xi + xs_ref[rows, re]
            ni = ar * xi + ai * xr + xs_ref[rows, im]
            xs_ref[rows, re] = nr
            xs_ref[rows, im] = ni
            new += [nr, ni]
        return tuple(new)

    init = tuple(st_ref[:, sw * c:sw * (c + 1)] for c in range(2 * nblk))
    last = lax.fori_loop(0, tt, step, init, unroll=2)
    for c in range(2 * nblk):
        st_ref[:, sw * c:sw * (c + 1)] = last[c]

    ys = [_dot(xs_ref[:, 2 * sw * m:2 * sw * (m + 1)].astype(BF16), cblk_ref[m]) for m in range(nblk)]
    y = jnp.concatenate(ys, axis=1) + d_ref[...] * u
    y = jax.nn.gelu(y)
    y = y * jax.nn.sigmoid(_dot(y.astype(BF16), wglu_ref[...]))

    for m in range(nblk):
        ytb_ref[m] = y[:, m * LANE:(m + 1) * LANE]
    for b in range(nb):
        for m in range(nblk):
            o_ref[b, :, m * LANE:(m + 1) * LANE] = (
                ytb_ref[m, pl.ds(b, tt, stride=nb), :].astype(o_ref.dtype))


def _s5(u3, a_row, bblk, cblk, d_row, wglu, layer):
    B, S, W = u3.shape
    tt = min(S, S5_TILE)
    ns2 = a_row.shape[1]
    nblk = W // LANE
    a8 = jnp.broadcast_to(a_row, (B, ns2))
    return pl.pallas_call(
        functools.partial(_s5_kernel, tt=tt),
        grid=(S // tt,),
        in_specs=[pl.BlockSpec((B, tt, W), lambda t: (0, t, 0)),
                  _const_spec(a8.shape), _layer_spec(bblk, layer), _layer_spec(cblk, layer),
                  _const_spec(d_row.shape), _layer_spec(wglu, layer)],
        out_specs=pl.BlockSpec((B, tt, W), lambda t: (0, t, 0)),
        out_shape=jax.ShapeDtypeStruct((B, S, W), BF16),
        scratch_shapes=[pltpu.VMEM((nblk, B * S5_PITCH, LANE), F32),
                        pltpu.VMEM((tt * B, W), F32),
                        pltpu.VMEM((tt * B, ns2), F32),
                        pltpu.VMEM((nblk, tt * B, LANE), F32),
                        pltpu.VMEM((B, ns2), F32)],
        compiler_params=_cparams(("arbitrary",)),
        name="s5",
    )(u3, a8, bblk, cblk, d_row, wglu)


def _s5_params(a_re, a_im, log_dt, b_re, b_im, c_re, c_im):
    G, P = a_re.shape
    n_c = b_re.shape[-1]
    gpb = LANE // n_c
    nblk = G // gpb
    dt = jnp.exp(log_dt)[:, None]
    mag = jnp.exp(a_re * dt)
    abar_re = mag * jnp.cos(a_im * dt)
    abar_im = mag * jnp.sin(a_im * dt)
    den = a_re * a_re + a_im * a_im
    nr, ni = abar_re - 1.0, abar_im
    f_re = (nr * a_re + ni * a_im) / den
    f_im = (ni * a_re - nr * a_im) / den
    bb_re = f_re[..., None] * b_re - f_im[..., None] * b_im
    bb_im = f_re[..., None] * b_im + f_im[..., None] * b_re
    eye = jnp.eye(gpb, dtype=F32)

    def bd_in(m):
        m = m.reshape(nblk, gpb, P, n_c)
        return jnp.einsum('ngpc,gh->ngchp', m, eye).reshape(nblk, gpb * n_c, gpb * P)

    def bd_out(m):
        m = m.reshape(nblk, gpb, n_c, P)
        return jnp.einsum('ngcp,gh->ngphc', m, eye).reshape(nblk, gpb * P, gpb * n_c)

    a_row = jnp.concatenate([abar_re.reshape(nblk, 1, gpb * P), abar_im.reshape(nblk, 1, gpb * P)],
                            axis=2).reshape(1, 2 * G * P)
    bblk = jnp.concatenate([bd_in(bb_re), bd_in(bb_im)], axis=2).astype(BF16)
    cblk = jnp.concatenate([bd_out(c_re), -bd_out(c_im)], axis=1).astype(BF16)
    return a_row, bblk, cblk


def _pool_body(p_ref, w_ref, sc_ref, o_ref, ext_ref, *, tt):
    pad = POOL_PAD
    t_idx = pl.program_id(1)
    ext_ref[pad:pad + tt, :] = p_ref[...]
    pos1 = (lax.broadcasted_iota(jnp.int32, (tt, LANE), 0) + t_idx * tt + 1).astype(F32)
    outs = []
    for g, win in enumerate(POOL_WINDOWS):
        cols = slice(g * LANE, (g + 1) * LANE)
        acc = ext_ref[:, cols]
        k = 1
        while k < win:
            acc = acc + pltpu.roll(acc, k, 0)
            k *= 2
        acc = acc[pad:pad + tt]
        cur = ext_ref[pad:pad + tt, cols]
        mixed = acc / jnp.minimum(pos1, float(win)) - cur
        outs.append(_dot(mixed.astype(BF16), w_ref[g]))
    o_ref[...] = (jnp.concatenate(outs, axis=1) * sc_ref[...]).astype(o_ref.dtype)
    ext_ref[0:pad, :] = ext_ref[tt:tt + pad, :]


def _merge_kernel(hn_ref, *refs):
    ys = refs[0:N_BRANCH]
    wgates = refs[N_BRANCH:2 * N_BRANCH]
    wbrs = refs[2 * N_BRANCH:3 * N_BRANCH]
    o_ref = refs[3 * N_BRANCH]
    tm = o_ref.shape[0]
    for rb in range(tm // MERGE_SUB):
        rows = slice(rb * MERGE_SUB, (rb + 1) * MERGE_SUB)
        hn = hn_ref[rows, :]
        acc = None
        for n in range(N_BRANCH):
            logits = _dot(hn, wgates[n][...])
            term = jax.nn.sigmoid(logits) * _dot(ys[n][rows, :], wbrs[n][...])
            acc = term if acc is None else acc + term
        o_ref[rows, :] = acc.astype(o_ref.dtype)


def _merge(hn, ys, wgate, wbr, layer):
    R, D = hn.shape
    tm = min(R, MERGE_ROW_TILE)
    bw = ys[0].shape[1]
    tn = MERGE_TILE
    nc = D // tn
    gate_spec = lambda n: pl.BlockSpec((None, D, tn), lambda i, c: (layer, 0, n * nc + c))
    br_spec = lambda n: pl.BlockSpec((None, None, bw, tn), lambda i, c: (layer, n, 0, c))
    return pl.pallas_call(
        _merge_kernel,
        grid=(R // tm, nc),
        in_specs=[pl.BlockSpec((tm, D), lambda i, c: (i, 0))]
        + [pl.BlockSpec((tm, bw), lambda i, c: (i, 0)) for _ in ys]
        + [gate_spec(n) for n in range(N_BRANCH)] + [br_spec(n) for n in range(N_BRANCH)],
        out_specs=pl.BlockSpec((tm, tn), lambda i, c: (i, c)),
        out_shape=jax.ShapeDtypeStruct((R, D), BF16),
        compiler_params=_cparams(("parallel", "arbitrary")),
        name="branch_merge",
    )(hn, *ys, *([wgate] * N_BRANCH), *([wbr] * N_BRANCH))


def _outproj_kernel(h_ref, m_ref, w_ref, g_ref, o_ref, hn_ref):
    for rb in range(h_ref.shape[0] // OUT_SUB):
        rows = slice(rb * OUT_SUB, (rb + 1) * OUT_SUB)
        h = h_ref[rows, :] + _dot(m_ref[rows, :], w_ref[...])
        o_ref[rows, :] = h
        hn_ref[rows, :] = _rms(h, g_ref[...]).astype(BF16)


def _outproj(h, merged, w, g, layer):
    R, D = h.shape
    tm = min(R, ROW_TILE)
    row_spec = pl.BlockSpec((tm, D), lambda i: (i, 0))
    return pl.pallas_call(
        _outproj_kernel,
        grid=(R // tm,),
        in_specs=[row_spec, row_spec, _layer_spec(w, layer), _const_spec((1, D))],
        out_specs=[row_spec, row_spec],
        out_shape=[jax.ShapeDtypeStruct((R, D), F32), jax.ShapeDtypeStruct((R, D), BF16)],
        compiler_params=_cparams(("parallel",)),
        name="out_proj",
    )(h, merged, w, g.reshape(1, D))


def _ffn_up_kernel(hn_ref, wa_ref, wv_ref, cw_ref, cb_ref, o_ref, carry_ref, *, tn, tiles_per_seq):
    i = pl.program_id(0)
    j = pl.program_id(1)
    tm = hn_ref.shape[0]

    @pl.when(i % tiles_per_seq == 0)
    def _():
        carry_ref[j] = jnp.zeros((SUBLANE, 2 * tn), F32)

    row8 = lax.broadcasted_iota(jnp.int32, (SUBLANE, FF_COL), 0)

    def delayed(u, prev8, k):
        r = pltpu.roll(u, k, 0)
        top = jnp.where(row8 < k, pltpu.roll(prev8, k, 0), r[0:SUBLANE])
        return jnp.concatenate([top, r[SUBLANE:]], axis=0)

    def conv(u, prev8, col0):
        cols = slice(col0, col0 + FF_COL)
        out = cb_ref[:, cols] + u * cw_ref[CONV_W - 1:CONV_W, cols]
        for tap in range(CONV_W - 1):
            out = out + delayed(u, prev8, CONV_W - 1 - tap) * cw_ref[tap:tap + 1, cols]
        return out

    def up(rb, ch):
        hn = hn_ref[rb * FF_SUB:(rb + 1) * FF_SUB, :]
        cols = slice(ch * FF_COL, (ch + 1) * FF_COL)
        return _dot(hn, wa_ref[:, cols]), _dot(hn, wv_ref[:, cols])

    n_col = tn // FF_COL
    units = [(rb, ch) for rb in range(tm // FF_SUB) for ch in range(n_col)]
    prev = carry_ref[j]
    prev_a = [prev[:, ch * FF_COL:(ch + 1) * FF_COL] for ch in range(n_col)]
    prev_v = [prev[:, tn + ch * FF_COL:tn + (ch + 1) * FF_COL] for ch in range(n_col)]
    nxt = up(*units[0])
    for idx, (rb, ch) in enumerate(units):
        ua, uv = nxt
        if idx + 1 < len(units):
            nxt = up(*units[idx + 1])
        a = conv(ua, prev_a[ch], ch * FF_COL)
        v = conv(uv, prev_v[ch], tn + ch * FF_COL)
        prev_a[ch], prev_v[ch] = ua[FF_SUB - SUBLANE:], uv[FF_SUB - SUBLANE:]
        o_ref[rb * FF_SUB:(rb + 1) * FF_SUB, ch * FF_COL:(ch + 1) * FF_COL] = (
            (_silu(a) * v).astype(o_ref.dtype))
    carry_ref[j] = jnp.concatenate(prev_a + prev_v, axis=1)


def _ffn_up(hn, wa, wv, cw, cb, S, layer):
    R, D = hn.shape
    tm = min(S, FF_ROW_TILE)
    tn = FF_TILE
    ffp = wa.shape[2]
    nj = ffp // tn
    return pl.pallas_call(
        functools.partial(_ffn_up_kernel, tn=tn, tiles_per_seq=S // tm),
        grid=(R // tm, nj),
        in_specs=[pl.BlockSpec((tm, D), lambda i, j: (i, 0)),
                  pl.BlockSpec((None, D, tn), lambda i, j: (layer, 0, j)),
                  pl.BlockSpec((None, D, tn), lambda i, j: (layer, 0, j)),
                  pl.BlockSpec((None, None, CONV_W, 2 * tn), lambda i, j: (layer, j, 0, 0)),
                  pl.BlockSpec((None, None, 1, 2 * tn), lambda i, j: (layer, j, 0, 0))],
        out_specs=pl.BlockSpec((tm, tn), lambda i, j: (i, j)),
        out_shape=jax.ShapeDtypeStruct((R, ffp), BF16),
        scratch_shapes=[pltpu.VMEM((nj, SUBLANE, 2 * tn), F32)],
        compiler_params=_cparams(("arbitrary", "arbitrary")),
        name="ffn_up",
    )(hn, wa, wv, cw, cb)


def _ffn_down_kernel(h_ref, act_ref, wd_ref, g_ref, o_ref, *, final):
    for rb in range(h_ref.shape[0] // OUT_SUB):
        rows = slice(rb * OUT_SUB, (rb + 1) * OUT_SUB)
        h = h_ref[rows, :] + _dot(act_ref[rows, :], wd_ref[...])
        o_ref[rows, :] = _rms(h, g_ref[...]) if final else h


def _ffn_down(h, act, wd, g, final, layer):
    R, D = h.shape
    tm = min(R, ROW_TILE)
    ff = wd.shape[1]
    row_spec = pl.BlockSpec((tm, D), lambda i: (i, 0))
    return pl.pallas_call(
        functools.partial(_ffn_down_kernel, final=final),
        grid=(R // tm,),
        in_specs=[row_spec, pl.BlockSpec((tm, ff), lambda i: (i, 0)),
                  _layer_spec(wd, layer), _const_spec((1, D))],
        out_specs=row_spec,
        out_shape=jax.ShapeDtypeStruct((R, D), F32),
        compiler_params=_cparams(("parallel",)),
        name="ffn_down",
    )(h, act, wd, g.reshape(1, D))


def _pad_cols(w, n):
    return jnp.pad(w, ((0, 0), (0, n - w.shape[1])))


def _split_w_in(w_in, D):
    bw = D // 4
    hk = bw // 2
    sizes = [hk, hk, bw, bw, bw, hk, hk, bw, bw, GLA_RANK, bw, N_BRANCH * D]
    offs = np.concatenate([[0], np.cumsum(sizes)])
    seg = lambda a, b: w_in[:, offs[a]:offs[b]]
    w_ret = seg(0, 4)
    w_s5 = seg(4, 5)
    w_gla = jnp.concatenate([seg(5, 9), _pad_cols(seg(9, 10), LANE)], axis=1)
    w_pool = seg(10, 11)
    return [w.astype(BF16) for w in (w_ret, w_s5, w_gla, w_pool)], seg(11, 12).astype(BF16)


def _ffn_weights(w_up, conv_w, conv_b, w_down, tn):
    D, ff2 = w_up.shape
    ff = ff2 // 2
    ffp = -(-ff // tn) * tn
    nj = ffp // tn
    wa = _pad_cols(w_up[:, :ff], ffp).astype(BF16)
    wv = _pad_cols(w_up[:, ff:], ffp).astype(BF16)
    tiles = lambda m: _pad_cols(m, ffp).reshape(m.shape[0], nj, tn)
    cw = jnp.concatenate([tiles(conv_w[:, :ff]), tiles(conv_w[:, ff:])], axis=2).transpose(1, 0, 2)
    cbr = conv_b.reshape(1, ff2)
    cb = jnp.concatenate([tiles(cbr[:, :ff]), tiles(cbr[:, ff:])], axis=2).transpose(1, 0, 2)
    return wa, wv, cw, cb, w_down.astype(BF16)


def kernel(x, positions, norm_mix_g, w_in, s5_a_re, s5_a_im, s5_log_dt, s5_b_re, s5_b_im, s5_c_re, s5_c_im, s5_d, s5_w_glu, gla_w_gate, gla_b_gate, pool_w, pool_scale, w_branch, w_out, norm_ffn_g, w_up, conv_w, conv_b, w_down, final_g):
    B, S, D = x.shape
    depth = w_in.shape[0]
    bw = D // 4
    ret_dk, ret_dv = bw // (2 * RET_HEADS), bw // RET_HEADS
    gla_dk, gla_dv = bw // (2 * GLA_HEADS), bw // GLA_HEADS
    assert S % CHUNK == 0 and S % min(S, ROW_TILE) == 0 and B == SUBLANE
    assert ret_dk // 2 * 4 == LANE and bw == S5_BLOCK * LANE

    inv = ROPE_BASE ** (-jnp.arange(0, ret_dk, 2, dtype=F32) / ret_dk)
    ang = positions.astype(F32)[..., None] * inv
    ang = jnp.tile(ang, (1, 1, LANE // (ret_dk // 2))).reshape(B * S, LANE)
    rope = _rope_tables(ang, ret_dk // 2)

    ws, w_gate = jax.vmap(lambda w: _split_w_in(w, D))(w_in)
    a_row, bblk, cblk = jax.vmap(_s5_params)(s5_a_re, s5_a_im, s5_log_dt, s5_b_re, s5_b_im,
                                             s5_c_re, s5_c_im)
    wglu = s5_w_glu.astype(BF16)
    wg = jnp.pad(gla_w_gate, ((0, 0), (0, LANE - GLA_RANK), (0, 0))).astype(BF16)
    pw = pool_w.astype(BF16)
    wbr = w_branch.astype(BF16)
    wo = w_out.astype(BF16)
    wa, wv, cw, cb, wd = jax.vmap(lambda a, b, c, d: _ffn_weights(a, b, c, d, FF_TILE))(
        w_up, conv_w, conv_b, w_down)

    h = x.reshape(B * S, D)
    for l in range(depth):
        hn, p_ret, p_s5, p_gla, p_pool = _proj(h, norm_mix_g[l], ws, l)
        ya, yd = _retention_pool(p_ret, rope, p_pool, pw[l], pool_scale[l].reshape(1, bw),
                                 B, S, RET_HEADS, ret_dk, ret_dv)
        yb = _s5(p_s5.reshape(B, S, bw), a_row[l], bblk, cblk, s5_d[l].reshape(1, bw),
                 wglu, l).reshape(B * S, bw)
        yc = _gla(p_gla, wg[l], gla_b_gate[l].reshape(1, -1), B, S, GLA_HEADS, gla_dk, gla_dv)
        merged = _merge(hn, [ya, yb, yc, yd], w_gate, wbr, l)
        h, hn = _outproj(h, merged, wo, norm_ffn_g[l], l)
        act = _ffn_up(hn, wa, wv, cw, cb, S, l)
        h = _ffn_down(h, act, wd, final_g, final=(l == depth - 1), layer=l)
    return h.reshape(B, S, D)
```

```python
import functools
import math

import numpy as np
import jax
import jax.numpy as jnp
from jax import lax
from jax.experimental import pallas as pl
from jax.experimental.pallas import tpu as pltpu

F32 = jnp.float32
BF16 = jnp.bfloat16
EPS = 1e-6

N_BRANCH = 4
RET_HEADS = 4
GLA_HEADS = 4
GLA_RANK = 16
GLA_GATE_TEMP = 16.0
S5_GROUP = 16
S5_STATE = 64
POOL_WINDOWS = (2, 4, 8, 16)
POOL_PAD = 16
ROPE_BASE = 10000.0
CONV_W = 3

LANE = 128
SUBLANE = 8
CHUNK = 128
VMEM_LIMIT = 56 * 1024 * 1024
ROW_TILE = 512
MERGE_ROW_TILE = 1024
MERGE_TILE = 512
MERGE_SUB = 256
OUT_SUB = 256
FF_ROW_TILE = 2048
FF_TILE = 512
FF_STRANDS = (1024, 512, 256, 256)
SEQ_TILE = 512
S5_TILE = 128
S5_PITCH = S5_TILE + 4
S5_BLOCK = 4


def _cparams(sem):
    return pltpu.CompilerParams(dimension_semantics=sem, vmem_limit_bytes=VMEM_LIMIT)


def _const_spec(shape):
    nd = len(shape)
    return pl.BlockSpec(shape, lambda *_: (0,) * nd, pipeline_mode=pl.Buffered(1))


def _layer_spec(arr, layer):
    nd = arr.ndim - 1
    return pl.BlockSpec((None,) + arr.shape[1:], lambda *_: (layer,) + (0,) * nd,
                        pipeline_mode=pl.Buffered(1))


def _dot(a, b):
    return jnp.dot(a, b, preferred_element_type=F32)


def _dot_nt(a, b):
    return lax.dot_general(a, b, (((1,), (1,)), ((), ())), preferred_element_type=F32)


def _rms(x, g):
    r = lax.rsqrt(jnp.mean(x * x, axis=-1, keepdims=True) + EPS)
    return x * r * g


def _head_norm(o):
    mu = jnp.mean(o, axis=-1, keepdims=True)
    d = o - mu
    var = jnp.mean(d * d, axis=-1, keepdims=True)
    return d * lax.rsqrt(var + EPS)


def _silu(x):
    return x * jax.nn.sigmoid(x)


def _key_blocks(k, heads, dk):
    kt = k.T.astype(BF16)
    zero = jnp.zeros((dk, kt.shape[1]), BF16)
    cols = [jnp.concatenate([kt[h * dk:(h + 1) * dk] if r == h else zero for r in range(heads)], axis=0)
            for h in range(heads)]
    return jnp.concatenate(cols, axis=1)


def _seq_map(nt):
    return lambda b, t: (b * nt + t, 0)


def _proj_kernel(h_ref, g_ref, w_ret, w_s5, w_gla, w_pool, o_hn, o_ret, o_s5, o_gla, o_pool):
    hn = _rms(h_ref[...], g_ref[...]).astype(BF16)
    o_hn[...] = hn
    o_ret[...] = _dot(hn, w_ret[...])
    o_s5[...] = _dot(hn, w_s5[...])
    o_gla[...] = _dot(hn, w_gla[...])
    o_pool[...] = _dot(hn, w_pool[...])


def _proj(h, g, ws, layer):
    R, D = h.shape
    tm = min(R, ROW_TILE)
    widths = [w.shape[2] for w in ws]
    return pl.pallas_call(
        _proj_kernel,
        grid=(R // tm,),
        in_specs=[pl.BlockSpec((tm, D), lambda i: (i, 0)), _const_spec((1, D))]
        + [_layer_spec(w, layer) for w in ws],
        out_specs=[pl.BlockSpec((tm, n), lambda i: (i, 0)) for n in [D] + widths],
        out_shape=[jax.ShapeDtypeStruct((R, D), BF16)]
        + [jax.ShapeDtypeStruct((R, n), F32) for n in widths],
        compiler_params=_cparams(("parallel",)),
        name="mixer_proj",
    )(h, g.reshape(1, D), *ws)


def _ret_consts(heads, dk, dv):
    log_g = np.log(1.0 - 2.0 ** (-5.0 - np.arange(heads, dtype=np.float64)))
    idx = np.arange(CHUNK, dtype=np.float64)
    rel = idx[:, None] - idx[None, :]
    dmask = np.where(rel >= 0, np.exp(np.maximum(rel, 0.0)[None] * log_g[:, None, None]), 0.0)
    dmask = np.concatenate(list(dmask), axis=1)
    lane_g = np.repeat(log_g, dk)[None, :]
    qdec = np.exp((idx[:, None] + 1.0) * lane_g)
    kdec = np.exp((CHUNK - 1.0 - idx)[:, None] * lane_g)
    sdec = np.exp(CHUNK * np.repeat(log_g, dk))[:, None] * np.ones((1, heads * dv))
    bmask = np.kron(np.eye(heads), np.ones((dk, dv)))
    f = lambda a: jnp.asarray(a, F32)
    return f(dmask), f(qdec), f(kdec), f(sdec), f(bmask)


def _rope_kernel(ang_ref, cos_ref, ssin_ref, *, half):
    ang = ang_ref[...]
    lane = lax.broadcasted_iota(jnp.int32, ang.shape, 1)
    cos_ref[...] = jnp.cos(ang)
    s = jnp.sin(ang)
    ssin_ref[...] = jnp.where((lane % (2 * half)) < half, -s, s)


def _rope_tables(ang, half):
    R, W = ang.shape
    tm = min(R, 2048)
    spec = pl.BlockSpec((tm, W), lambda i: (i, 0))
    return pl.pallas_call(
        functools.partial(_rope_kernel, half=half),
        grid=(R // tm,),
        in_specs=[spec],
        out_specs=[spec, spec],
        out_shape=[jax.ShapeDtypeStruct((R, W), F32)] * 2,
        compiler_params=_cparams(("parallel",)),
        name="rope_tables",
    )(ang)


def _ret_body(x_ref, cos_ref, ssin_ref, dmask_ref, qdec_ref, kdec_ref, sdec_ref, bmask_ref,
              o_ref, state_ref, *, heads, dk, dv, n_chunks):
    hk = heads * dk
    hv = heads * dv
    lane = lax.broadcasted_iota(jnp.int32, (CHUNK, hk), 1)
    first_half = (lane % dk) < (dk // 2)

    def rotary(x, cos2, ssin):
        swapped = jnp.where(first_half, pltpu.roll(x, hk - dk // 2, 1), pltpu.roll(x, dk // 2, 1))
        return x * cos2 + swapped * ssin

    def chunk(ci, carry):
        r0 = pl.multiple_of(ci * CHUNK, CHUNK)
        rows = pl.ds(r0, CHUNK)
        reps = hk // LANE
        cos2 = jnp.concatenate([cos_ref[rows, :]] * reps, axis=1)
        ssin = jnp.concatenate([ssin_ref[rows, :]] * reps, axis=1)
        q = rotary(x_ref[rows, 0:hk], cos2, ssin)
        k = rotary(x_ref[rows, hk:2 * hk], cos2, ssin) * (dk ** -0.5)
        v = x_ref[rows, 2 * hk:2 * hk + hv]
        gate = x_ref[rows, 2 * hk + hv:2 * hk + 2 * hv]
        vb = v.astype(BF16)
        state = state_ref[...]
        cross = _dot((q * qdec_ref[...]).astype(BF16), state.astype(BF16))
        sb = (_dot(q.astype(BF16), _key_blocks(k, heads, dk)) * dmask_ref[...]).astype(BF16)
        outs = [_dot(sb[:, h * CHUNK:(h + 1) * CHUNK], vb[:, h * dv:(h + 1) * dv]) for h in range(heads)]
        o = jnp.concatenate(outs, axis=1) + cross
        kd = (k * kdec_ref[...]).astype(BF16)
        upd = _dot(kd.T, vb)
        state_ref[...] = sdec_ref[...] * state + bmask_ref[...] * upd
        y = jnp.concatenate(
            [_head_norm(o[:, h * dv:(h + 1) * dv]) for h in range(heads)], axis=1)
        o_ref[rows, :] = (y * _silu(gate)).astype(o_ref.dtype)
        return carry

    lax.fori_loop(0, n_chunks, chunk, 0, unroll=True)


def _ret_pool_kernel(x_ref, cos_ref, ssin_ref, dmask_ref, qdec_ref, kdec_ref, sdec_ref, bmask_ref,
                     p_ref, pw_ref, psc_ref, o_ref, po_ref, state_ref, ext_ref,
                     *, heads, dk, dv, n_chunks, tt):
    @pl.when(pl.program_id(1) == 0)
    def _():
        state_ref[...] = jnp.zeros_like(state_ref)
        ext_ref[0:POOL_PAD, :] = jnp.zeros((POOL_PAD, ext_ref.shape[1]), F32)

    _pool_body(p_ref, pw_ref, psc_ref, po_ref, ext_ref, tt=tt)
    _ret_body(x_ref, cos_ref, ssin_ref, dmask_ref, qdec_ref, kdec_ref, sdec_ref, bmask_ref,
              o_ref, state_ref, heads=heads, dk=dk, dv=dv, n_chunks=n_chunks)


def _retention_pool(x, rope, p, pw, pscale, B, S, heads, dk, dv):
    tt = min(S, SEQ_TILE)
    nt = S // tt
    win = 2 * heads * dk + 2 * heads * dv
    pwid = pw.shape[0] * LANE
    consts = _ret_consts(heads, dk, dv)
    kern = functools.partial(_ret_pool_kernel, heads=heads, dk=dk, dv=dv, n_chunks=tt // CHUNK, tt=tt)
    seq = lambda width: pl.BlockSpec((tt, width), _seq_map(nt))
    return pl.pallas_call(
        kern,
        grid=(B, nt),
        in_specs=[seq(win), seq(LANE), seq(LANE)] + [_const_spec(c.shape) for c in consts]
        + [seq(pwid), _const_spec(pw.shape), _const_spec(pscale.shape)],
        out_specs=[seq(heads * dv), seq(pwid)],
        out_shape=[jax.ShapeDtypeStruct((B * S, heads * dv), BF16),
                   jax.ShapeDtypeStruct((B * S, pwid), BF16)],
        scratch_shapes=[pltpu.VMEM((heads * dk, heads * dv), F32),
                        pltpu.VMEM((tt + POOL_PAD, pwid), F32)],
        compiler_params=_cparams(("parallel", "arbitrary")),
        name="retention_pool",
    )(x, *rope, *consts, p, pw, pscale)


def _gla_consts(heads, dk, dv):
    C = CHUNK
    n_lvl = int(math.log2(C))
    t = np.arange(C)
    sums = []
    masks = []
    for l in range(n_lvl):
        bit = (t >> l) & 1
        start = (t >> l) << l
        r = t[None, :]
        upper = (bit[:, None] == 1) & (r >= start[:, None]) & (r <= t[:, None])
        lower = (bit[:, None] == 0) & (r > t[:, None]) & (r < (start + (1 << l))[:, None])
        sums.append((upper | lower).astype(np.float64))
        m = ((bit[:, None] == 1) & (bit[None, :] == 0)
             & ((t[:, None] >> (l + 1)) == (t[None, :] >> (l + 1))))
        masks.append(np.tile(m.astype(np.float64), (1, heads)))
    masks.append(np.tile(np.eye(C), (1, heads)))
    sums.append((t[None, :] <= t[:, None]).astype(np.float64))
    sums.append((t[None, :] > t[:, None]).astype(np.float64))
    g = np.concatenate(sums, axis=0)
    gcat = np.concatenate([g, g], axis=1)
    bmask = np.kron(np.eye(heads), np.ones((dk, dv)))
    return jnp.asarray(gcat, BF16), jnp.asarray(np.stack(masks), F32), jnp.asarray(bmask, F32)


def _log_sigmoid(x):
    return jnp.minimum(x, 0.0) - jnp.log(1.0 + jnp.exp(-jnp.abs(x)))


def _gla_kernel(x_ref, wg_ref, bg_ref, gcat_ref, masks_ref, bmask_ref, o_ref, state_ref,
                *, heads, dk, dv, n_chunks):
    hk = heads * dk
    hv = heads * dv
    C = CHUNK
    n_lvl = int(math.log2(C))

    @pl.when(pl.program_id(1) == 0)
    def _():
        state_ref[...] = jnp.zeros_like(state_ref)

    key_blocks = functools.partial(_key_blocks, heads=heads, dk=dk)

    def chunk(ci, carry):
        r0 = pl.multiple_of(ci * C, C)
        rows = pl.ds(r0, C)
        q = x_ref[rows, 0:hk] * (dk ** -0.5)
        k = x_ref[rows, hk:2 * hk]
        v = x_ref[rows, 2 * hk:2 * hk + hv]
        gate = x_ref[rows, 2 * hk + hv:2 * hk + 2 * hv]
        code = x_ref[rows, 2 * hk + 2 * hv:2 * hk + 2 * hv + LANE]
        vb = v.astype(BF16)
        log_a = _log_sigmoid(_dot(code.astype(BF16), wg_ref[...]) + bg_ref[...]) * (1.0 / GLA_GATE_TEMP)
        hi = log_a.astype(BF16)
        lo = (log_a - hi.astype(F32)).astype(BF16)
        sums = _dot(gcat_ref[...], jnp.concatenate([hi, lo], axis=0))

        scores = masks_ref[n_lvl] * _dot(q.astype(BF16), key_blocks(k))
        for l in range(n_lvl):
            z = jnp.exp(sums[l * C:(l + 1) * C])
            scores = scores + masks_ref[l] * _dot((q * z).astype(BF16), key_blocks(k * z))
        sb = scores.astype(BF16)
        intra = jnp.concatenate(
            [_dot(sb[:, h * C:(h + 1) * C], vb[:, h * dv:(h + 1) * dv]) for h in range(heads)], axis=1)

        state = state_ref[...]
        e_pre = sums[n_lvl * C:(n_lvl + 1) * C]
        e_suf = sums[(n_lvl + 1) * C:(n_lvl + 2) * C]
        cross = _dot((q * jnp.exp(e_pre)).astype(BF16), state.astype(BF16))
        kd = (k * jnp.exp(e_suf)).astype(BF16)
        upd = _dot(kd.T, vb)
        total = jnp.broadcast_to(jnp.exp(e_pre[C - 1:C, :]), (LANE, hk))
        dec = jnp.concatenate([total.T] * (hv // LANE), axis=1)
        state_ref[...] = dec * state + bmask_ref[...] * upd

        o = intra + cross
        y = jnp.concatenate(
            [_head_norm(o[:, h * dv:(h + 1) * dv]) for h in range(heads)], axis=1)
        o_ref[rows, :] = (y * _silu(gate)).astype(o_ref.dtype)
        return carry

    lax.fori_loop(0, n_chunks, chunk, 0)


def _gla(x, wg, bg, B, S, heads, dk, dv):
    tt = min(S, SEQ_TILE)
    nt = S // tt
    win = 2 * heads * dk + 2 * heads * dv + LANE
    consts = _gla_consts(heads, dk, dv)
    kern = functools.partial(_gla_kernel, heads=heads, dk=dk, dv=dv, n_chunks=tt // CHUNK)
    return pl.pallas_call(
        kern,
        grid=(B, nt),
        in_specs=[pl.BlockSpec((tt, win), _seq_map(nt)),
                  _const_spec(wg.shape), _const_spec(bg.shape)]
        + [_const_spec(c.shape) for c in consts],
        out_specs=pl.BlockSpec((tt, heads * dv), _seq_map(nt)),
        out_shape=jax.ShapeDtypeStruct((B * S, heads * dv), BF16),
        scratch_shapes=[pltpu.VMEM((heads * dk, heads * dv), F32)],
        compiler_params=_cparams(("parallel", "arbitrary")),
        name="gla",
    )(x, wg, bg, *consts)


def _s5_kernel(u_ref, a_ref, bblk_ref, cblk_ref, d_ref, wglu_ref, o_ref,
               stage_ref, utb_ref, xs_ref, ytb_ref, st_ref, *, tt):
    nb, w = u_ref.shape[0], u_ref.shape[2]
    nblk = w // LANE
    sw = a_ref.shape[1] // (2 * nblk)

    @pl.when(pl.program_id(0) == 0)
    def _():
        st_ref[...] = jnp.zeros_like(st_ref)

    for b in range(nb):
        for m in range(nblk):
            stage_ref[m, b * S5_PITCH:b * S5_PITCH + tt, :] = u_ref[b, :, m * LANE:(m + 1) * LANE]

    def gather(t, carry):
        rows = pl.ds(pl.multiple_of(t * nb, nb), nb)
        for m in range(nblk):
            utb_ref[rows, m * LANE:(m + 1) * LANE] = stage_ref[m, pl.ds(t, nb, stride=S5_PITCH), :]
        return carry

    lax.fori_loop(0, tt, gather, 0)

    u = utb_ref[...]
    ub = u.astype(BF16)
    for m in range(nblk):
        xs_ref[:, 2 * sw * m:2 * sw * (m + 1)] = _dot(ub[:, m * LANE:(m + 1) * LANE], bblk_ref[m])

    def step(t, carry):
        rows = pl.ds(pl.multiple_of(t * nb, nb), nb)
        new = []
        for m in range(nblk):
            xr, xi = carry[2 * m], carry[2 * m + 1]
            re = slice(2 * sw * m, 2 * sw * m + sw)
            im = slice(2 * sw * m + sw, 2 * sw * (m + 1))
            ar, ai = a_ref[:, re], a_ref[:, im]
            nr = ar * xr - ai * xi + xs_ref[rows, re]
            ni = ar * xi + ai * xr + xs_ref[rows, im]
            xs_ref[rows, re] = nr
            xs_ref[rows, im] = ni
            new += [nr, ni]
        return tuple(new)

    init = tuple(st_ref[:, sw * c:sw * (c + 1)] for c in range(2 * nblk))
    last = lax.fori_loop(0, tt, step, init, unroll=2)
    for c in range(2 * nblk):
        st_ref[:, sw * c:sw * (c + 1)] = last[c]

    ys = [_dot(xs_ref[:, 2 * sw * m:2 * sw * (m + 1)].astype(BF16), cblk_ref[m]) for m in range(nblk)]
    y = jnp.concatenate(ys, axis=1) + d_ref[...] * u
    y = jax.nn.gelu(y)
    y = y * jax.nn.sigmoid(_dot(y.astype(BF16), wglu_ref[...]))

    for m in range(nblk):
        ytb_ref[m] = y[:, m * LANE:(m + 1) * LANE]
    for b in range(nb):
        for m in range(nblk):
            o_ref[b, :, m * LANE:(m + 1) * LANE] = (
                ytb_ref[m, pl.ds(b, tt, stride=nb), :].astype(o_ref.dtype))


def _s5(u3, a_row, bblk, cblk, d_row, wglu, layer):
    B, S, W = u3.shape
    tt = min(S, S5_TILE)
    ns2 = a_row.shape[1]
    nblk = W // LANE
    a8 = jnp.broadcast_to(a_row, (B, ns2))
    return pl.pallas_call(
        functools.partial(_s5_kernel, tt=tt),
        grid=(S // tt,),
        in_specs=[pl.BlockSpec((B, tt, W), lambda t: (0, t, 0)),
                  _const_spec(a8.shape), _layer_spec(bblk, layer), _layer_spec(cblk, layer),
                  _const_spec(d_row.shape), _layer_spec(wglu, layer)],
        out_specs=pl.BlockSpec((B, tt, W), lambda t: (0, t, 0)),
        out_shape=jax.ShapeDtypeStruct((B, S, W), BF16),
        scratch_shapes=[pltpu.VMEM((nblk, B * S5_PITCH, LANE), F32),
                        pltpu.VMEM((tt * B, W), F32),
                        pltpu.VMEM((tt * B, ns2), F32),
                        pltpu.VMEM((nblk, tt * B, LANE), F32),
                        pltpu.VMEM((B, ns2), F32)],
        compiler_params=_cparams(("arbitrary",)),
        name="s5",
    )(u3, a8, bblk, cblk, d_row, wglu)


def _s5_params(a_re, a_im, log_dt, b_re, b_im, c_re, c_im):
    G, P = a_re.shape
    n_c = b_re.shape[-1]
    gpb = LANE // n_c
    nblk = G // gpb
    dt = jnp.exp(log_dt)[:, None]
    mag = jnp.exp(a_re * dt)
    abar_re = mag * jnp.cos(a_im * dt)
    abar_im = mag * jnp.sin(a_im * dt)
    den = a_re * a_re + a_im * a_im
    nr, ni = abar_re - 1.0, abar_im
    f_re = (nr * a_re + ni * a_im) / den
    f_im = (ni * a_re - nr * a_im) / den
    bb_re = f_re[..., None] * b_re - f_im[..., None] * b_im
    bb_im = f_re[..., None] * b_im + f_im[..., None] * b_re
    eye = jnp.eye(gpb, dtype=F32)

    def bd_in(m):
        m = m.reshape(nblk, gpb, P, n_c)
        return jnp.einsum('ngpc,gh->ngchp', m, eye).reshape(nblk, gpb * n_c, gpb * P)

    def bd_out(m):
        m = m.reshape(nblk, gpb, n_c, P)
        return jnp.einsum('ngcp,gh->ngphc', m, eye).reshape(nblk, gpb * P, gpb * n_c)

    a_row = jnp.concatenate([abar_re.reshape(nblk, 1, gpb * P), abar_im.reshape(nblk, 1, gpb * P)],
                            axis=2).reshape(1, 2 * G * P)
    bblk = jnp.concatenate([bd_in(bb_re), bd_in(bb_im)], axis=2).astype(BF16)
    cblk = jnp.concatenate([bd_out(c_re), -bd_out(c_im)], axis=1).astype(BF16)
    return a_row, bblk, cblk


def _pool_body(p_ref, w_ref, sc_ref, o_ref, ext_ref, *, tt):
    pad = POOL_PAD
    t_idx = pl.program_id(1)
    ext_ref[pad:pad + tt, :] = p_ref[...]
    pos1 = (lax.broadcasted_iota(jnp.int32, (tt, LANE), 0) + t_idx * tt + 1).astype(F32)
    outs = []
    for g, win in enumerate(POOL_WINDOWS):
        cols = slice(g * LANE, (g + 1) * LANE)
        acc = ext_ref[:, cols]
        k = 1
        while k < win:
            acc = acc + pltpu.roll(acc, k, 0)
            k *= 2
        acc = acc[pad:pad + tt]
        cur = ext_ref[pad:pad + tt, cols]
        mixed = acc / jnp.minimum(pos1, float(win)) - cur
        outs.append(_dot(mixed.astype(BF16), w_ref[g]))
    o_ref[...] = (jnp.concatenate(outs, axis=1) * sc_ref[...]).astype(o_ref.dtype)
    ext_ref[0:pad, :] = ext_ref[tt:tt + pad, :]


def _merge_kernel(hn_ref, *refs):
    ys = refs[0:N_BRANCH]
    wgates = refs[N_BRANCH:2 * N_BRANCH]
    wbrs = refs[2 * N_BRANCH:3 * N_BRANCH]
    o_ref = refs[3 * N_BRANCH]
    tm = o_ref.shape[0]
    for rb in range(tm // MERGE_SUB):
        rows = slice(rb * MERGE_SUB, (rb + 1) * MERGE_SUB)
        hn = hn_ref[rows, :]
        acc = None
        for n in range(N_BRANCH):
            logits = _dot(hn, wgates[n][...])
            term = jax.nn.sigmoid(logits) * _dot(ys[n][rows, :], wbrs[n][...])
            acc = term if acc is None else acc + term
        o_ref[rows, :] = acc.astype(o_ref.dtype)


def _merge(hn, ys, wgate, wbr, layer):
    R, D = hn.shape
    tm = min(R, MERGE_ROW_TILE)
    bw = ys[0].shape[1]
    tn = MERGE_TILE
    nc = D // tn
    gate_spec = lambda n: pl.BlockSpec((None, D, tn), lambda i, c: (layer, 0, n * nc + c))
    br_spec = lambda n: pl.BlockSpec((None, None, bw, tn), lambda i, c: (layer, n, 0, c))
    return pl.pallas_call(
        _merge_kernel,
        grid=(R // tm, nc),
        in_specs=[pl.BlockSpec((tm, D), lambda i, c: (i, 0))]
        + [pl.BlockSpec((tm, bw), lambda i, c: (i, 0)) for _ in ys]
        + [gate_spec(n) for n in range(N_BRANCH)] + [br_spec(n) for n in range(N_BRANCH)],
        out_specs=pl.BlockSpec((tm, tn), lambda i, c: (i, c)),
        out_shape=jax.ShapeDtypeStruct((R, D), BF16),
        compiler_params=_cparams(("parallel", "arbitrary")),
        name="branch_merge",
    )(hn, *ys, *([wgate] * N_BRANCH), *([wbr] * N_BRANCH))


def _outproj_kernel(h_ref, m_ref, w_ref, g_ref, o_ref, hn_ref):
    for rb in range(h_ref.shape[0] // OUT_SUB):
        rows = slice(rb * OUT_SUB, (rb + 1) * OUT_SUB)
        h = h_ref[rows, :] + _dot(m_ref[rows, :], w_ref[...])
        o_ref[rows, :] = h
        hn_ref[rows, :] = _rms(h, g_ref[...]).astype(BF16)


def _outproj(h, merged, w, g, layer):
    R, D = h.shape
    tm = min(R, ROW_TILE)
    row_spec = pl.BlockSpec((tm, D), lambda i: (i, 0))
    return pl.pallas_call(
        _outproj_kernel,
        grid=(R // tm,),
        in_specs=[row_spec, row_spec, _layer_spec(w, layer), _const_spec((1, D))],
        out_specs=[row_spec, row_spec],
        out_shape=[jax.ShapeDtypeStruct((R, D), F32), jax.ShapeDtypeStruct((R, D), BF16)],
        compiler_params=_cparams(("parallel",)),
        name="out_proj",
    )(h, merged, w, g.reshape(1, D))


def _ffn_up_kernel(hn_ref, w_ref, cw_ref, cb_ref, o_ref, carry_ref, *, tn, tiles_per_seq):
    i = pl.program_id(0)
    j = pl.program_id(1)
    tm = hn_ref.shape[0]

    @pl.when(i % tiles_per_seq == 0)
    def _():
        carry_ref[j] = jnp.zeros((SUBLANE, 2 * tn), F32)

    row8 = lax.broadcasted_iota(jnp.int32, (SUBLANE, 2 * tn), 0)

    def delayed(u, prev8, k):
        r = pltpu.roll(u, k, 0)
        top = jnp.where(row8 < k, pltpu.roll(prev8, k, 0), r[0:SUBLANE])
        return jnp.concatenate([top, r[SUBLANE:]], axis=0)

    def conv(u, prev8):
        out = cb_ref[...] + u * cw_ref[CONV_W - 1:CONV_W, :]
        for tap in range(CONV_W - 1):
            out = out + delayed(u, prev8, CONV_W - 1 - tap) * cw_ref[tap:tap + 1, :]
        return out

    strands = FF_STRANDS if sum(FF_STRANDS) == tm else (tm,)
    bounds = np.cumsum((0,) + tuple(strands))
    spans = [(int(bounds[k]), int(bounds[k + 1])) for k in range(len(strands))]

    def up(span):
        return _dot(hn_ref[span[0]:span[1], :], w_ref[...])

    prev = carry_ref[j]
    nxt = up(spans[0])
    for idx, (r0, r1) in enumerate(spans):
        u = nxt
        if idx + 1 < len(spans):
            nxt = up(spans[idx + 1])
        c = conv(u, prev)
        prev = u[r1 - r0 - SUBLANE:]
        o_ref[r0:r1, :] = (_silu(c[:, 0:tn]) * c[:, tn:2 * tn]).astype(o_ref.dtype)
    carry_ref[j] = prev


def _ffn_up(hn, w, cw, cb, S, layer):
    R, D = hn.shape
    tm = min(S, FF_ROW_TILE)
    tn = FF_TILE
    nj = w.shape[2] // (2 * tn)
    return pl.pallas_call(
        functools.partial(_ffn_up_kernel, tn=tn, tiles_per_seq=S // tm),
        grid=(R // tm, nj),
        in_specs=[pl.BlockSpec((tm, D), lambda i, j: (i, 0)),
                  pl.BlockSpec((None, D, 2 * tn), lambda i, j: (layer, 0, j)),
                  pl.BlockSpec((None, None, CONV_W, 2 * tn), lambda i, j: (layer, j, 0, 0)),
                  pl.BlockSpec((None, None, 1, 2 * tn), lambda i, j: (layer, j, 0, 0))],
        out_specs=pl.BlockSpec((tm, tn), lambda i, j: (i, j)),
        out_shape=jax.ShapeDtypeStruct((R, nj * tn), BF16),
        scratch_shapes=[pltpu.VMEM((nj, SUBLANE, 2 * tn), F32)],
        compiler_params=_cparams(("arbitrary", "arbitrary")),
        name="ffn_up",
    )(hn, w, cw, cb)


def _ffn_down_kernel(h_ref, act_ref, wd_ref, g_ref, o_ref, *, final):
    for rb in range(h_ref.shape[0] // OUT_SUB):
        rows = slice(rb * OUT_SUB, (rb + 1) * OUT_SUB)
        h = h_ref[rows, :] + _dot(act_ref[rows, :], wd_ref[...])
        o_ref[rows, :] = _rms(h, g_ref[...]) if final else h


def _ffn_down(h, act, wd, g, final, layer):
    R, D = h.shape
    tm = min(R, ROW_TILE)
    ff = wd.shape[1]
    row_spec = pl.BlockSpec((tm, D), lambda i: (i, 0))
    return pl.pallas_call(
        functools.partial(_ffn_down_kernel, final=final),
        grid=(R // tm,),
        in_specs=[row_spec, pl.BlockSpec((tm, ff), lambda i: (i, 0)),
                  _layer_spec(wd, layer), _const_spec((1, D))],
        out_specs=row_spec,
        out_shape=jax.ShapeDtypeStruct((R, D), F32),
        compiler_params=_cparams(("parallel",)),
        name="ffn_down",
    )(h, act, wd, g.reshape(1, D))


def _pad_cols(w, n):
    return jnp.pad(w, ((0, 0), (0, n - w.shape[1])))


def _split_w_in(w_in, D):
    bw = D // 4
    hk = bw // 2
    sizes = [hk, hk, bw, bw, bw, hk, hk, bw, bw, GLA_RANK, bw, N_BRANCH * D]
    offs = np.concatenate([[0], np.cumsum(sizes)])
    w_in = w_in.astype(BF16)
    seg = lambda a, b: w_in[:, offs[a]:offs[b]]
    w_ret = seg(0, 4)
    w_s5 = seg(4, 5)
    w_gla = jnp.concatenate([seg(5, 9), _pad_cols(seg(9, 10), LANE)], axis=1)
    w_pool = seg(10, 11)
    return [w_ret, w_s5, w_gla, w_pool], seg(11, 12)


def _ffn_weights(w_up, conv_w, conv_b, w_down, tn):
    D, ff2 = w_up.shape
    ff = ff2 // 2
    ffp = -(-ff // tn) * tn
    nj = ffp // tn
    tiles = lambda m: _pad_cols(m, ffp).reshape(m.shape[0], nj, tn)
    wb = w_up.astype(BF16)
    w = jnp.concatenate([tiles(wb[:, :ff]), tiles(wb[:, ff:])], axis=2).reshape(D, 2 * ffp)
    cw = jnp.concatenate([tiles(conv_w[:, :ff]), tiles(conv_w[:, ff:])], axis=2).transpose(1, 0, 2)
    cbr = conv_b.reshape(1, ff2)
    cb = jnp.concatenate([tiles(cbr[:, :ff]), tiles(cbr[:, ff:])], axis=2).transpose(1, 0, 2)
    return w, cw, cb, w_down.astype(BF16)


def kernel(x, positions, norm_mix_g, w_in, s5_a_re, s5_a_im, s5_log_dt, s5_b_re, s5_b_im, s5_c_re, s5_c_im, s5_d, s5_w_glu, gla_w_gate, gla_b_gate, pool_w, pool_scale, w_branch, w_out, norm_ffn_g, w_up, conv_w, conv_b, w_down, final_g):
    B, S, D = x.shape
    depth = w_in.shape[0]
    bw = D // 4
    ret_dk, ret_dv = bw // (2 * RET_HEADS), bw // RET_HEADS
    gla_dk, gla_dv = bw // (2 * GLA_HEADS), bw // GLA_HEADS
    assert S % CHUNK == 0 and S % min(S, ROW_TILE) == 0 and B == SUBLANE
    assert ret_dk // 2 * 4 == LANE and bw == S5_BLOCK * LANE

    inv = ROPE_BASE ** (-jnp.arange(0, ret_dk, 2, dtype=F32) / ret_dk)
    ang = positions.astype(F32)[..., None] * inv
    ang = jnp.tile(ang, (1, 1, LANE // (ret_dk // 2))).reshape(B * S, LANE)
    rope = _rope_tables(ang, ret_dk // 2)

    ws, w_gate = jax.vmap(lambda w: _split_w_in(w, D))(w_in)
    a_row, bblk, cblk = jax.vmap(_s5_params)(s5_a_re, s5_a_im, s5_log_dt, s5_b_re, s5_b_im,
                                             s5_c_re, s5_c_im)
    wglu = s5_w_glu.astype(BF16)
    wg = jnp.pad(gla_w_gate, ((0, 0), (0, LANE - GLA_RANK), (0, 0))).astype(BF16)
    pw = pool_w.astype(BF16)
    wbr = w_branch.astype(BF16)
    wo = w_out.astype(BF16)
    wup, cw, cb, wd = jax.vmap(lambda a, b, c, d: _ffn_weights(a, b, c, d, FF_TILE))(
        w_up, conv_w, conv_b, w_down)

    h = x.reshape(B * S, D)
    for l in range(depth):
        hn, p_ret, p_s5, p_gla, p_pool = _proj(h, norm_mix_g[l], ws, l)
        ya, yd = _retention_pool(p_ret, rope, p_pool, pw[l], pool_scale[l].reshape(1, bw),
                                 B, S, RET_HEADS, ret_dk, ret_dv)
        yb = _s5(p_s5.reshape(B, S, bw), a_row[l], bblk, cblk, s5_d[l].reshape(1, bw),
                 wglu, l).reshape(B * S, bw)
        yc = _gla(p_gla, wg[l], gla_b_gate[l].reshape(1, -1), B, S, GLA_HEADS, gla_dk, gla_dv)
        merged = _merge(hn, [ya, yb, yc, yd], w_gate, wbr, l)
        h, hn = _outproj(h, merged, wo, norm_ffn_g[l], l)
        act = _ffn_up(hn, wup, cw, cb, S, l)
        h = _ffn_down(h, act, wd, final_g, final=(l == depth - 1), layer=l)
    return h.reshape(B, S, D)
```

```python
import functools
import math

import numpy as np
import jax
import jax.numpy as jnp
from jax import lax
from jax.experimental import pallas as pl
from jax.experimental.pallas import tpu as pltpu

F32 = jnp.float32
BF16 = jnp.bfloat16
EPS = 1e-6

N_BRANCH = 4
RET_HEADS = 4
GLA_HEADS = 4
GLA_RANK = 16
GLA_GATE_TEMP = 16.0
S5_GROUP = 16
S5_STATE = 64
POOL_WINDOWS = (2, 4, 8, 16)
POOL_PAD = 16
ROPE_BASE = 10000.0
CONV_W = 3

LANE = 128
SUBLANE = 8
CHUNK = 128
VMEM_LIMIT = 56 * 1024 * 1024
ROW_TILE = 512
MERGE_ROW_TILE = 1024
MERGE_TILE = 512
MERGE_SUB = 256
OUT_SUB = 256
FF_ROW_TILE = 2048
FF_TILE = 512
FF_SUB = 512
SEQ_TILE = 512
S5_TILE = 128
S5_PITCH = S5_TILE + 4
S5_BLOCK = 4


def _cparams(sem):
    return pltpu.CompilerParams(dimension_semantics=sem, vmem_limit_bytes=VMEM_LIMIT)


def _const_spec(shape):
    nd = len(shape)
    return pl.BlockSpec(shape, lambda *_: (0,) * nd, pipeline_mode=pl.Buffered(1))


def _layer_spec(arr, layer):
    nd = arr.ndim - 1
    return pl.BlockSpec((None,) + arr.shape[1:], lambda *_: (layer,) + (0,) * nd,
                        pipeline_mode=pl.Buffered(1))


def _dot(a, b):
    return jnp.dot(a, b, preferred_element_type=F32)


def _dot_nt(a, b):
    return lax.dot_general(a, b, (((1,), (1,)), ((), ())), preferred_element_type=F32)


def _rms(x, g):
    r = lax.rsqrt(jnp.mean(x * x, axis=-1, keepdims=True) + EPS)
    return x * r * g


def _head_norm(o):
    mu = jnp.mean(o, axis=-1, keepdims=True)
    d = o - mu
    var = jnp.mean(d * d, axis=-1, keepdims=True)
    return d * lax.rsqrt(var + EPS)


def _silu(x):
    return x * jax.nn.sigmoid(x)


def _key_blocks(k, heads, dk):
    kt = k.T.astype(BF16)
    zero = jnp.zeros((dk, kt.shape[1]), BF16)
    cols = [jnp.concatenate([kt[h * dk:(h + 1) * dk] if r == h else zero for r in range(heads)], axis=0)
            for h in range(heads)]
    return jnp.concatenate(cols, axis=1)


def _seq_map(nt):
    return lambda b, t: (b * nt + t, 0)


def _proj_kernel(h_ref, g_ref, w_ret, w_s5, w_gla, w_pool, o_hn, o_ret, o_s5, o_gla, o_pool):
    hn = _rms(h_ref[...], g_ref[...]).astype(BF16)
    o_hn[...] = hn
    o_ret[...] = _dot(hn, w_ret[...])
    o_s5[...] = _dot(hn, w_s5[...])
    o_gla[...] = _dot(hn, w_gla[...])
    o_pool[...] = _dot(hn, w_pool[...])


def _proj(h, g, ws, layer):
    R, D = h.shape
    tm = min(R, ROW_TILE)
    widths = [w.shape[2] for w in ws]
    return pl.pallas_call(
        _proj_kernel,
        grid=(R // tm,),
        in_specs=[pl.BlockSpec((tm, D), lambda i: (i, 0)), _const_spec((1, D))]
        + [_layer_spec(w, layer) for w in ws],
        out_specs=[pl.BlockSpec((tm, n), lambda i: (i, 0)) for n in [D] + widths],
        out_shape=[jax.ShapeDtypeStruct((R, D), BF16)]
        + [jax.ShapeDtypeStruct((R, n), F32) for n in widths],
        compiler_params=_cparams(("parallel",)),
        name="mixer_proj",
    )(h, g.reshape(1, D), *ws)


def _ret_consts(heads, dk, dv):
    log_g = np.log(1.0 - 2.0 ** (-5.0 - np.arange(heads, dtype=np.float64)))
    idx = np.arange(CHUNK, dtype=np.float64)
    rel = idx[:, None] - idx[None, :]
    dmask = np.where(rel >= 0, np.exp(np.maximum(rel, 0.0)[None] * log_g[:, None, None]), 0.0)
    dmask = np.concatenate(list(dmask), axis=1)
    lane_g = np.repeat(log_g, dk)[None, :]
    qdec = np.exp((idx[:, None] + 1.0) * lane_g)
    kdec = np.exp((CHUNK - 1.0 - idx)[:, None] * lane_g)
    sdec = np.exp(CHUNK * np.repeat(log_g, dk))[:, None] * np.ones((1, heads * dv))
    bmask = np.kron(np.eye(heads), np.ones((dk, dv)))
    f = lambda a: jnp.asarray(a, F32)
    return f(dmask), f(qdec), f(kdec), f(sdec), f(bmask)


def _rope_kernel(ang_ref, cos_ref, ssin_ref, *, half):
    ang = ang_ref[...]
    lane = lax.broadcasted_iota(jnp.int32, ang.shape, 1)
    cos_ref[...] = jnp.cos(ang)
    s = jnp.sin(ang)
    ssin_ref[...] = jnp.where((lane % (2 * half)) < half, -s, s)


def _rope_tables(ang, half):
    R, W = ang.shape
    tm = min(R, 2048)
    spec = pl.BlockSpec((tm, W), lambda i: (i, 0))
    return pl.pallas_call(
        functools.partial(_rope_kernel, half=half),
        grid=(R // tm,),
        in_specs=[spec],
        out_specs=[spec, spec],
        out_shape=[jax.ShapeDtypeStruct((R, W), F32)] * 2,
        compiler_params=_cparams(("parallel",)),
        name="rope_tables",
    )(ang)


def _ret_body(x_ref, cos_ref, ssin_ref, dmask_ref, qdec_ref, kdec_ref, sdec_ref, bmask_ref,
              o_ref, state_ref, *, heads, dk, dv, n_chunks):
    hk = heads * dk
    hv = heads * dv
    lane = lax.broadcasted_iota(jnp.int32, (CHUNK, hk), 1)
    first_half = (lane % dk) < (dk // 2)

    def rotary(x, cos2, ssin):
        swapped = jnp.where(first_half, pltpu.roll(x, hk - dk // 2, 1), pltpu.roll(x, dk // 2, 1))
        return x * cos2 + swapped * ssin

    def chunk(ci, carry):
        r0 = pl.multiple_of(ci * CHUNK, CHUNK)
        rows = pl.ds(r0, CHUNK)
        reps = hk // LANE
        cos2 = jnp.concatenate([cos_ref[rows, :]] * reps, axis=1)
        ssin = jnp.concatenate([ssin_ref[rows, :]] * reps, axis=1)
        q = rotary(x_ref[rows, 0:hk], cos2, ssin)
        k = rotary(x_ref[rows, hk:2 * hk], cos2, ssin) * (dk ** -0.5)
        v = x_ref[rows, 2 * hk:2 * hk + hv]
        gate = x_ref[rows, 2 * hk + hv:2 * hk + 2 * hv]
        vb = v.astype(BF16)
        state = state_ref[...]
        cross = _dot((q * qdec_ref[...]).astype(BF16), state.astype(BF16))
        sb = (_dot(q.astype(BF16), _key_blocks(k, heads, dk)) * dmask_ref[...]).astype(BF16)
        outs = [_dot(sb[:, h * CHUNK:(h + 1) * CHUNK], vb[:, h * dv:(h + 1) * dv]) for h in range(heads)]
        o = jnp.concatenate(outs, axis=1) + cross
        kd = (k * kdec_ref[...]).astype(BF16)
        upd = _dot(kd.T, vb)
        state_ref[...] = sdec_ref[...] * state + bmask_ref[...] * upd
        y = jnp.concatenate(
            [_head_norm(o[:, h * dv:(h + 1) * dv]) for h in range(heads)], axis=1)
        o_ref[rows, :] = (y * _silu(gate)).astype(o_ref.dtype)
        return carry

    lax.fori_loop(0, n_chunks, chunk, 0, unroll=True)


def _ret_pool_kernel(x_ref, cos_ref, ssin_ref, dmask_ref, qdec_ref, kdec_ref, sdec_ref, bmask_ref,
                     p_ref, pw_ref, psc_ref, o_ref, po_ref, state_ref, ext_ref,
                     *, heads, dk, dv, n_chunks, tt):
    @pl.when(pl.program_id(1) == 0)
    def _():
        state_ref[...] = jnp.zeros_like(state_ref)
        ext_ref[0:POOL_PAD, :] = jnp.zeros((POOL_PAD, ext_ref.shape[1]), F32)

    _pool_body(p_ref, pw_ref, psc_ref, po_ref, ext_ref, tt=tt)
    _ret_body(x_ref, cos_ref, ssin_ref, dmask_ref, qdec_ref, kdec_ref, sdec_ref, bmask_ref,
              o_ref, state_ref, heads=heads, dk=dk, dv=dv, n_chunks=n_chunks)


def _retention_pool(x, rope, p, pw, pscale, B, S, heads, dk, dv):
    tt = min(S, SEQ_TILE)
    nt = S // tt
    win = 2 * heads * dk + 2 * heads * dv
    pwid = pw.shape[0] * LANE
    consts = _ret_consts(heads, dk, dv)
    kern = functools.partial(_ret_pool_kernel, heads=heads, dk=dk, dv=dv, n_chunks=tt // CHUNK, tt=tt)
    seq = lambda width: pl.BlockSpec((tt, width), _seq_map(nt))
    return pl.pallas_call(
        kern,
        grid=(B, nt),
        in_specs=[seq(win), seq(LANE), seq(LANE)] + [_const_spec(c.shape) for c in consts]
        + [seq(pwid), _const_spec(pw.shape), _const_spec(pscale.shape)],
        out_specs=[seq(heads * dv), seq(pwid)],
        out_shape=[jax.ShapeDtypeStruct((B * S, heads * dv), BF16),
                   jax.ShapeDtypeStruct((B * S, pwid), BF16)],
        scratch_shapes=[pltpu.VMEM((heads * dk, heads * dv), F32),
                        pltpu.VMEM((tt + POOL_PAD, pwid), F32)],
        compiler_params=_cparams(("parallel", "arbitrary")),
        name="retention_pool",
    )(x, *rope, *consts, p, pw, pscale)


def _gla_consts(heads, dk, dv):
    C = CHUNK
    n_lvl = int(math.log2(C))
    t = np.arange(C)
    sums = []
    masks = []
    for l in range(n_lvl):
        bit = (t >> l) & 1
        start = (t >> l) << l
        r = t[None, :]
        upper = (bit[:, None] == 1) & (r >= start[:, None]) & (r <= t[:, None])
        lower = (bit[:, None] == 0) & (r > t[:, None]) & (r < (start + (1 << l))[:, None])
        sums.append((upper | lower).astype(np.float64))
        m = ((bit[:, None] == 1) & (bit[None, :] == 0)
             & ((t[:, None] >> (l + 1)) == (t[None, :] >> (l + 1))))
        masks.append(np.tile(m.astype(np.float64), (1, heads)))
    masks.append(np.tile(np.eye(C), (1, heads)))
    sums.append((t[None, :] <= t[:, None]).astype(np.float64))
    sums.append((t[None, :] > t[:, None]).astype(np.float64))
    g = np.concatenate(sums, axis=0)
    gcat = np.concatenate([g, g], axis=1)
    bmask = np.kron(np.eye(heads), np.ones((dk, dv)))
    return jnp.asarray(gcat, BF16), jnp.asarray(np.stack(masks), F32), jnp.asarray(bmask, F32)


def _log_sigmoid(x):
    return jnp.minimum(x, 0.0) - jnp.log(1.0 + jnp.exp(-jnp.abs(x)))


def _gla_kernel(x_ref, wg_ref, bg_ref, gcat_ref, masks_ref, bmask_ref, o_ref, state_ref,
                *, heads, dk, dv, n_chunks):
    hk = heads * dk
    hv = heads * dv
    C = CHUNK
    n_lvl = int(math.log2(C))

    @pl.when(pl.program_id(1) == 0)
    def _():
        state_ref[...] = jnp.zeros_like(state_ref)

    key_blocks = functools.partial(_key_blocks, heads=heads, dk=dk)

    def chunk(ci, carry):
        r0 = pl.multiple_of(ci * C, C)
        rows = pl.ds(r0, C)
        q = x_ref[rows, 0:hk] * (dk ** -0.5)
        k = x_ref[rows, hk:2 * hk]
        v = x_ref[rows, 2 * hk:2 * hk + hv]
        gate = x_ref[rows, 2 * hk + hv:2 * hk + 2 * hv]
        code = x_ref[rows, 2 * hk + 2 * hv:2 * hk + 2 * hv + LANE]
        vb = v.astype(BF16)
        log_a = _log_sigmoid(_dot(code.astype(BF16), wg_ref[...]) + bg_ref[...]) * (1.0 / GLA_GATE_TEMP)
        hi = log_a.astype(BF16)
        lo = (log_a - hi.astype(F32)).astype(BF16)
        sums = _dot(gcat_ref[...], jnp.concatenate([hi, lo], axis=0))

        scores = masks_ref[n_lvl] * _dot(q.astype(BF16), key_blocks(k))
        for l in range(n_lvl):
            z = jnp.exp(sums[l * C:(l + 1) * C])
            scores = scores + masks_ref[l] * _dot((q * z).astype(BF16), key_blocks(k * z))
        sb = scores.astype(BF16)
        intra = jnp.concatenate(
            [_dot(sb[:, h * C:(h + 1) * C], vb[:, h * dv:(h + 1) * dv]) for h in range(heads)], axis=1)

        state = state_ref[...]
        e_pre = sums[n_lvl * C:(n_lvl + 1) * C]
        e_suf = sums[(n_lvl + 1) * C:(n_lvl + 2) * C]
        cross = _dot((q * jnp.exp(e_pre)).astype(BF16), state.astype(BF16))
        kd = (k * jnp.exp(e_suf)).astype(BF16)
        upd = _dot(kd.T, vb)
        total = jnp.broadcast_to(jnp.exp(e_pre[C - 1:C, :]), (LANE, hk))
        dec = jnp.concatenate([total.T] * (hv // LANE), axis=1)
        state_ref[...] = dec * state + bmask_ref[...] * upd

        o = intra + cross
        y = jnp.concatenate(
            [_head_norm(o[:, h * dv:(h + 1) * dv]) for h in range(heads)], axis=1)
        o_ref[rows, :] = (y * _silu(gate)).astype(o_ref.dtype)
        return carry

    lax.fori_loop(0, n_chunks, chunk, 0)


def _gla(x, wg, bg, B, S, heads, dk, dv):
    tt = min(S, SEQ_TILE)
    nt = S // tt
    win = 2 * heads * dk + 2 * heads * dv + LANE
    consts = _gla_consts(heads, dk, dv)
    kern = functools.partial(_gla_kernel, heads=heads, dk=dk, dv=dv, n_chunks=tt // CHUNK)
    return pl.pallas_call(
        kern,
        grid=(B, nt),
        in_specs=[pl.BlockSpec((tt, win), _seq_map(nt)),
                  _const_spec(wg.shape), _const_spec(bg.shape)]
        + [_const_spec(c.shape) for c in consts],
        out_specs=pl.BlockSpec((tt, heads * dv), _seq_map(nt)),
        out_shape=jax.ShapeDtypeStruct((B * S, heads * dv), BF16),
        scratch_shapes=[pltpu.VMEM((heads * dk, heads * dv), F32)],
        compiler_params=_cparams(("parallel", "arbitrary")),
        name="gla",
    )(x, wg, bg, *consts)


def _s5_kernel(u_ref, a_ref, bblk_ref, cblk_ref, d_ref, wglu_ref, o_ref,
               stage_ref, utb_ref, xs_ref, ytb_ref, st_ref, *, tt):
    nb, w = u_ref.shape[0], u_ref.shape[2]
    nblk = w // LANE
    sw = a_ref.shape[1] // (2 * nblk)

    @pl.when(pl.program_id(0) == 0)
    def _():
        st_ref[...] = jnp.zeros_like(st_ref)

    for b in range(nb):
        for m in range(nblk):
            stage_ref[m, b * S5_PITCH:b * S5_PITCH + tt, :] = u_ref[b, :, m * LANE:(m + 1) * LANE]

    def gather(t, carry):
        rows = pl.ds(pl.multiple_of(t * nb, nb), nb)
        for m in range(nblk):
            utb_ref[rows, m * LANE:(m + 1) * LANE] = stage_ref[m, pl.ds(t, nb, stride=S5_PITCH), :]
        return carry

    lax.fori_loop(0, tt, gather, 0)

    u = utb_ref[...]
    ub = u.astype(BF16)
    for m in range(nblk):
        xs_ref[:, 2 * sw * m:2 * sw * (m + 1)] = _dot(ub[:, m * LANE:(m + 1) * LANE], bblk_ref[m])

    def step(t, carry):
        rows = pl.ds(pl.multiple_of(t * nb, nb), nb)
        new = []
        for m in range(nblk):
            xr, xi = carry[2 * m], carry[2 * m + 1]
            re = slice(2 * sw * m, 2 * sw * m + sw)
            im = slice(2 * sw * m + sw, 2 * sw * (m + 1))
            ar, ai = a_ref[:, re], a_ref[:, im]
            nr = ar * xr - ai * xi + xs_ref[rows, re]
            ni = ar * xi + ai * xr + xs_ref[rows, im]
            xs_ref[rows, re] = nr
            xs_ref[rows, im] = ni
            new += [nr, ni]
        return tuple(new)

    init = tuple(st_ref[:, sw * c:sw * (c + 1)] for c in range(2 * nblk))
    last = lax.fori_loop(0, tt, step, init, unroll=2)
    for c in range(2 * nblk):
        st_ref[:, sw * c:sw * (c + 1)] = last[c]

    ys = [_dot(xs_ref[:, 2 * sw * m:2 * sw * (m + 1)].astype(BF16), cblk_ref[m]) for m in range(nblk)]
    y = jnp.concatenate(ys, axis=1) + d_ref[...] * u
    y = jax.nn.gelu(y)
    y = y * jax.nn.sigmoid(_dot(y.astype(BF16), wglu_ref[...]))

    for m in range(nblk):
        ytb_ref[m] = y[:, m * LANE:(m + 1) * LANE]
    for b in range(nb):
        for m in range(nblk):
            o_ref[b, :, m * LANE:(m + 1) * LANE] = (
                ytb_ref[m, pl.ds(b, tt, stride=nb), :].astype(o_ref.dtype))


def _s5(u3, a_row, bblk, cblk, d_row, wglu, layer):
    B, S, W = u3.shape
    tt = min(S, S5_TILE)
    ns2 = a_row.shape[1]
    nblk = W // LANE
    a8 = jnp.broadcast_to(a_row, (B, ns2))
    return pl.pallas_call(
        functools.partial(_s5_kernel, tt=tt),
        grid=(S // tt,),
        in_specs=[pl.BlockSpec((B, tt, W), lambda t: (0, t, 0)),
                  _const_spec(a8.shape), _layer_spec(bblk, layer), _layer_spec(cblk, layer),
                  _const_spec(d_row.shape), _layer_spec(wglu, layer)],
        out_specs=pl.BlockSpec((B, tt, W), lambda t: (0, t, 0)),
        out_shape=jax.ShapeDtypeStruct((B, S, W), BF16),
        scratch_shapes=[pltpu.VMEM((nblk, B * S5_PITCH, LANE), F32),
                        pltpu.VMEM((tt * B, W), F32),
                        pltpu.VMEM((tt * B, ns2), F32),
                        pltpu.VMEM((nblk, tt * B, LANE), F32),
                        pltpu.VMEM((B, ns2), F32)],
        compiler_params=_cparams(("arbitrary",)),
        name="s5",
    )(u3, a8, bblk, cblk, d_row, wglu)


def _s5_params(a_re, a_im, log_dt, b_re, b_im, c_re, c_im):
    G, P = a_re.shape
    n_c = b_re.shape[-1]
    gpb = LANE // n_c
    nblk = G // gpb
    dt = jnp.exp(log_dt)[:, None]
    mag = jnp.exp(a_re * dt)
    abar_re = mag * jnp.cos(a_im * dt)
    abar_im = mag * jnp.sin(a_im * dt)
    den = a_re * a_re + a_im * a_im
    nr, ni = abar_re - 1.0, abar_im
    f_re = (nr * a_re + ni * a_im) / den
    f_im = (ni * a_re - nr * a_im) / den
    bb_re = f_re[..., None] * b_re - f_im[..., None] * b_im
    bb_im = f_re[..., None] * b_im + f_im[..., None] * b_re
    eye = jnp.eye(gpb, dtype=F32)

    def bd_in(m):
        m = m.reshape(nblk, gpb, P, n_c)
        return jnp.einsum('ngpc,gh->ngchp', m, eye).reshape(nblk, gpb * n_c, gpb * P)

    def bd_out(m):
        m = m.reshape(nblk, gpb, n_c, P)
        return jnp.einsum('ngcp,gh->ngphc', m, eye).reshape(nblk, gpb * P, gpb * n_c)

    a_row = jnp.concatenate([abar_re.reshape(nblk, 1, gpb * P), abar_im.reshape(nblk, 1, gpb * P)],
                            axis=2).reshape(1, 2 * G * P)
    bblk = jnp.concatenate([bd_in(bb_re), bd_in(bb_im)], axis=2).astype(BF16)
    cblk = jnp.concatenate([bd_out(c_re), -bd_out(c_im)], axis=1).astype(BF16)
    return a_row, bblk, cblk


def _pool_body(p_ref, w_ref, sc_ref, o_ref, ext_ref, *, tt):
    pad = POOL_PAD
    t_idx = pl.program_id(1)
    ext_ref[pad:pad + tt, :] = p_ref[...]
    pos1 = (lax.broadcasted_iota(jnp.int32, (tt, LANE), 0) + t_idx * tt + 1).astype(F32)
    outs = []
    for g, win in enumerate(POOL_WINDOWS):
        cols = slice(g * LANE, (g + 1) * LANE)
        acc = ext_ref[:, cols]
        k = 1
        while k < win:
            acc = acc + pltpu.roll(acc, k, 0)
            k *= 2
        acc = acc[pad:pad + tt]
        cur = ext_ref[pad:pad + tt, cols]
        mixed = acc / jnp.minimum(pos1, float(win)) - cur
        outs.append(_dot(mixed.astype(BF16), w_ref[g]))
    o_ref[...] = (jnp.concatenate(outs, axis=1) * sc_ref[...]).astype(o_ref.dtype)
    ext_ref[0:pad, :] = ext_ref[tt:tt + pad, :]


def _merge_kernel(hn_ref, *refs):
    ys = refs[0:N_BRANCH]
    wgates = refs[N_BRANCH:2 * N_BRANCH]
    wbrs = refs[2 * N_BRANCH:3 * N_BRANCH]
    o_ref = refs[3 * N_BRANCH]
    tm = o_ref.shape[0]
    for rb in range(tm // MERGE_SUB):
        rows = slice(rb * MERGE_SUB, (rb + 1) * MERGE_SUB)
        hn = hn_ref[rows, :]
        acc = None
        for n in range(N_BRANCH):
            logits = _dot(hn, wgates[n][...])
            term = jax.nn.sigmoid(logits) * _dot(ys[n][rows, :], wbrs[n][...])
            acc = term if acc is None else acc + term
        o_ref[rows, :] = acc.astype(o_ref.dtype)


def _merge(hn, ys, wgate, wbr, layer):
    R, D = hn.shape
    tm = min(R, MERGE_ROW_TILE)
    bw = ys[0].shape[1]
    tn = MERGE_TILE
    nc = D // tn
    gate_spec = lambda n: pl.BlockSpec((None, D, tn), lambda i, c: (layer, 0, n * nc + c))
    br_spec = lambda n: pl.BlockSpec((None, None, bw, tn), lambda i, c: (layer, n, 0, c))
    return pl.pallas_call(
        _merge_kernel,
        grid=(R // tm, nc),
        in_specs=[pl.BlockSpec((tm, D), lambda i, c: (i, 0))]
        + [pl.BlockSpec((tm, bw), lambda i, c: (i, 0)) for _ in ys]
        + [gate_spec(n) for n in range(N_BRANCH)] + [br_spec(n) for n in range(N_BRANCH)],
        out_specs=pl.BlockSpec((tm, tn), lambda i, c: (i, c)),
        out_shape=jax.ShapeDtypeStruct((R, D), BF16),
        compiler_params=_cparams(("parallel", "arbitrary")),
        name="branch_merge",
    )(hn, *ys, *([wgate] * N_BRANCH), *([wbr] * N_BRANCH))


def _outproj_kernel(h_ref, m_ref, w_ref, g_ref, o_ref, hn_ref):
    for rb in range(h_ref.shape[0] // OUT_SUB):
        rows = slice(rb * OUT_SUB, (rb + 1) * OUT_SUB)
        h = h_ref[rows, :] + _dot(m_ref[rows, :], w_ref[...])
        o_ref[rows, :] = h
        hn_ref[rows, :] = _rms(h, g_ref[...]).astype(BF16)


def _outproj(h, merged, w, g, layer):
    R, D = h.shape
    tm = min(R, ROW_TILE)
    row_spec = pl.BlockSpec((tm, D), lambda i: (i, 0))
    return pl.pallas_call(
        _outproj_kernel,
        grid=(R // tm,),
        in_specs=[row_spec, row_spec, _layer_spec(w, layer), _const_spec((1, D))],
        out_specs=[row_spec, row_spec],
        out_shape=[jax.ShapeDtypeStruct((R, D), F32), jax.ShapeDtypeStruct((R, D), BF16)],
        compiler_params=_cparams(("parallel",)),
        name="out_proj",
    )(h, merged, w, g.reshape(1, D))


def _ffn_up_kernel(hn_ref, wa_ref, wv_ref, cw_ref, cb_ref, o_ref, carry_ref, *, tn, tiles_per_seq):
    i = pl.program_id(0)
    j = pl.program_id(1)
    tm = hn_ref.shape[0]

    @pl.when(i % tiles_per_seq == 0)
    def _():
        carry_ref[j] = jnp.zeros((SUBLANE, 2 * tn), F32)

    row8 = lax.broadcasted_iota(jnp.int32, (SUBLANE, tn), 0)

    def delayed(u, prev8, k):
        r = pltpu.roll(u, k, 0)
        top = jnp.where(row8 < k, pltpu.roll(prev8, k, 0), r[0:SUBLANE])
        return jnp.concatenate([top, r[SUBLANE:]], axis=0)

    def conv(u, prev8, col0):
        cols = slice(col0, col0 + tn)
        out = cb_ref[:, cols] + u * cw_ref[CONV_W - 1:CONV_W, cols]
        for tap in range(CONV_W - 1):
            out = out + delayed(u, prev8, CONV_W - 1 - tap) * cw_ref[tap:tap + 1, cols]
        return out

    def up(rb):
        hn = hn_ref[rb * FF_SUB:(rb + 1) * FF_SUB, :]
        return _dot(hn, wa_ref[...]), _dot(hn, wv_ref[...])

    n_sub = tm // FF_SUB
    prev = carry_ref[j]
    prev_a, prev_v = prev[:, 0:tn], prev[:, tn:2 * tn]
    nxt = up(0)
    for rb in range(n_sub):
        ua, uv = nxt
        if rb + 1 < n_sub:
            nxt = up(rb + 1)
        a = conv(ua, prev_a, 0)
        v = conv(uv, prev_v, tn)
        prev_a, prev_v = ua[FF_SUB - SUBLANE:], uv[FF_SUB - SUBLANE:]
        o_ref[rb * FF_SUB:(rb + 1) * FF_SUB, :] = (_silu(a) * v).astype(o_ref.dtype)
    carry_ref[j] = jnp.concatenate([prev_a, prev_v], axis=1)


def _ffn_up(hn, wa, wv, cw, cb, S, layer):
    R, D = hn.shape
    tm = min(S, FF_ROW_TILE)
    tn = FF_TILE
    ffp = wa.shape[2]
    nj = ffp // tn
    return pl.pallas_call(
        functools.partial(_ffn_up_kernel, tn=tn, tiles_per_seq=S // tm),
        grid=(R // tm, nj),
        in_specs=[pl.BlockSpec((tm, D), lambda i, j: (i, 0)),
                  pl.BlockSpec((None, D, tn), lambda i, j: (layer, 0, j)),
                  pl.BlockSpec((None, D, tn), lambda i, j: (layer, 0, j)),
                  pl.BlockSpec((None, None, CONV_W, 2 * tn), lambda i, j: (layer, j, 0, 0)),
                  pl.BlockSpec((None, None, 1, 2 * tn), lambda i, j: (layer, j, 0, 0))],
        out_specs=pl.BlockSpec((tm, tn), lambda i, j: (i, j)),
        out_shape=jax.ShapeDtypeStruct((R, ffp), BF16),
        scratch_shapes=[pltpu.VMEM((nj, SUBLANE, 2 * tn), F32)],
        compiler_params=_cparams(("arbitrary", "arbitrary")),
        name="ffn_up",
    )(hn, wa, wv, cw, cb)


def _ffn_down_kernel(h_ref, act_ref, wd_ref, g_ref, o_ref, *, final):
    for rb in range(h_ref.shape[0] // OUT_SUB):
        rows = slice(rb * OUT_SUB, (rb + 1) * OUT_SUB)
        h = h_ref[rows, :] + _dot(act_ref[rows, :], wd_ref[...])
        o_ref[rows, :] = _rms(h, g_ref[...]) if final else h


def _ffn_down(h, act, wd, g, final, layer):
    R, D = h.shape
    tm = min(R, ROW_TILE)
    ff = wd.shape[1]
    row_spec = pl.BlockSpec((tm, D), lambda i: (i, 0))
    return pl.pallas_call(
        functools.partial(_ffn_down_kernel, final=final),
        grid=(R // tm,),
        in_specs=[row_spec, pl.BlockSpec((tm, ff), lambda i: (i, 0)),
                  _layer_spec(wd, layer), _const_spec((1, D))],
        out_specs=row_spec,
        out_shape=jax.ShapeDtypeStruct((R, D), F32),
        compiler_params=_cparams(("parallel",)),
        name="ffn_down",
    )(h, act, wd, g.reshape(1, D))


def _pad_cols(w, n):
    return jnp.pad(w, ((0, 0), (0, n - w.shape[1])))


def _split_w_in(w_in, D):
    bw = D // 4
    hk = bw // 2
    sizes = [hk, hk, bw, bw, bw, hk, hk, bw, bw, GLA_RANK, bw, N_BRANCH * D]
    offs = np.concatenate([[0], np.cumsum(sizes)])
    seg = lambda a, b: w_in[:, offs[a]:offs[b]]
    w_ret = seg(0, 4)
    w_s5 = seg(4, 5)
    w_gla = jnp.concatenate([seg(5, 9), _pad_cols(seg(9, 10), LANE)], axis=1)
    w_pool = seg(10, 11)
    return [w.astype(BF16) for w in (w_ret, w_s5, w_gla, w_pool)], seg(11, 12).astype(BF16)


def _ffn_weights(w_up, conv_w, conv_b, w_down, tn):
    D, ff2 = w_up.shape
    ff = ff2 // 2
    ffp = -(-ff // tn) * tn
    nj = ffp // tn
    wa = _pad_cols(w_up[:, :ff], ffp).astype(BF16)
    wv = _pad_cols(w_up[:, ff:], ffp).astype(BF16)
    tiles = lambda m: _pad_cols(m, ffp).reshape(m.shape[0], nj, tn)
    cw = jnp.concatenate([tiles(conv_w[:, :ff]), tiles(conv_w[:, ff:])], axis=2).transpose(1, 0, 2)
    cbr = conv_b.reshape(1, ff2)
    cb = jnp.concatenate([tiles(cbr[:, :ff]), tiles(cbr[:, ff:])], axis=2).transpose(1, 0, 2)
    return wa, wv, cw, cb, w_down.astype(BF16)


def kernel(x, positions, norm_mix_g, w_in, s5_a_re, s5_a_im, s5_log_dt, s5_b_re, s5_b_im, s5_c_re, s5_c_im, s5_d, s5_w_glu, gla_w_gate, gla_b_gate, pool_w, pool_scale, w_branch, w_out, norm_ffn_g, w_up, conv_w, conv_b, w_down, final_g):
    B, S, D = x.shape
    depth = w_in.shape[0]
    bw = D // 4
    ret_dk, ret_dv = bw // (2 * RET_HEADS), bw // RET_HEADS
    gla_dk, gla_dv = bw // (2 * GLA_HEADS), bw // GLA_HEADS
    assert S % CHUNK == 0 and S % min(S, ROW_TILE) == 0 and B == SUBLANE
    assert ret_dk // 2 * 4 == LANE and bw == S5_BLOCK * LANE

    inv = ROPE_BASE ** (-jnp.arange(0, ret_dk, 2, dtype=F32) / ret_dk)
    ang = positions.astype(F32)[..., None] * inv
    ang = jnp.tile(ang, (1, 1, LANE // (ret_dk // 2))).reshape(B * S, LANE)
    rope = _rope_tables(ang, ret_dk // 2)

    ws, w_gate = jax.vmap(lambda w: _split_w_in(w, D))(w_in)
    a_row, bblk, cblk = jax.vmap(_s5_params)(s5_a_re, s5_a_im, s5_log_dt, s5_b_re, s5_b_im,
                                             s5_c_re, s5_c_im)
    wglu = s5_w_glu.astype(BF16)
    wg = jnp.pad(gla_w_gate, ((0, 0), (0, LANE - GLA_RANK), (0, 0))).astype(BF16)
    pw = pool_w.astype(BF16)
    wbr = w_branch.astype(BF16)
    wo = w_out.astype(BF16)
    wa, wv, cw, cb, wd = jax.vmap(lambda a, b, c, d: _ffn_weights(a, b, c, d, FF_TILE))(
        w_up, conv_w, conv_b, w_down)

    h = x.reshape(B * S, D)
    for l in range(depth):
        hn, p_ret, p_s5, p_gla, p_pool = _proj(h, norm_mix_g[l], ws, l)
        ya, yd = _retention_pool(p_ret, rope, p_pool, pw[l], pool_scale[l].reshape(1, bw),
                                 B, S, RET_HEADS, ret_dk, ret_dv)
        yb = _s5(p_s5.reshape(B, S, bw), a_row[l], bblk, cblk, s5_d[l].reshape(1, bw),
                 wglu, l).reshape(B * S, bw)
        yc = _gla(p_gla, wg[l], gla_b_gate[l].reshape(1, -1), B, S, GLA_HEADS, gla_dk, gla_dv)
        merged = _merge(hn, [ya, yb, yc, yd], w_gate, wbr, l)
        h, hn = _outproj(h, merged, wo, norm_ffn_g[l], l)
        act = _ffn_up(hn, wa, wv, cw, cb, S, l)
        h = _ffn_down(h, act, wd, final_g, final=(l == depth - 1), layer=l)
    return h.reshape(B, S, D)
```

```python
import functools
import math

import numpy as np
import jax
import jax.numpy as jnp
from jax import lax
from jax.experimental import pallas as pl
from jax.experimental.pallas import tpu as pltpu

F32 = jnp.float32
BF16 = jnp.bfloat16
EPS = 1e-6

N_BRANCH = 4
RET_HEADS = 4
GLA_HEADS = 4
GLA_RANK = 16
GLA_GATE_TEMP = 16.0
S5_GROUP = 16
S5_STATE = 64
POOL_WINDOWS = (2, 4, 8, 16)
POOL_PAD = 16
ROPE_BASE = 10000.0
CONV_W = 3

LANE = 128
SUBLANE = 8
CHUNK = 128
VMEM_LIMIT = 56 * 1024 * 1024
ROW_TILE = 512
MERGE_ROW_TILE = 1024
MERGE_TILE = 512
MERGE_SUB = 256
OUT_SUB = 256
FF_ROW_TILE = 2048
FF_TILE = 512
FF_SUB = 1024
SEQ_TILE = 512
S5_TILE = 128
S5_PITCH = S5_TILE + 4
S5_BLOCK = 4


def _cparams(sem):
    return pltpu.CompilerParams(dimension_semantics=sem, vmem_limit_bytes=VMEM_LIMIT)


def _const_spec(shape):
    nd = len(shape)
    return pl.BlockSpec(shape, lambda *_: (0,) * nd, pipeline_mode=pl.Buffered(1))


def _layer_spec(arr, layer):
    nd = arr.ndim - 1
    return pl.BlockSpec((None,) + arr.shape[1:], lambda *_: (layer,) + (0,) * nd,
                        pipeline_mode=pl.Buffered(1))


def _dot(a, b):
    return jnp.dot(a, b, preferred_element_type=F32)


def _dot_nt(a, b):
    return lax.dot_general(a, b, (((1,), (1,)), ((), ())), preferred_element_type=F32)


def _rms(x, g):
    r = lax.rsqrt(jnp.mean(x * x, axis=-1, keepdims=True) + EPS)
    return x * r * g


def _head_norm(o):
    mu = jnp.mean(o, axis=-1, keepdims=True)
    d = o - mu
    var = jnp.mean(d * d, axis=-1, keepdims=True)
    return d * lax.rsqrt(var + EPS)


def _silu(x):
    return x * jax.nn.sigmoid(x)


def _key_blocks(k, heads, dk):
    kt = k.T.astype(BF16)
    zero = jnp.zeros((dk, kt.shape[1]), BF16)
    cols = [jnp.concatenate([kt[h * dk:(h + 1) * dk] if r == h else zero for r in range(heads)], axis=0)
            for h in range(heads)]
    return jnp.concatenate(cols, axis=1)


def _seq_map(nt):
    return lambda b, t: (b * nt + t, 0)


def _proj_kernel(h_ref, g_ref, w_ret, w_s5, w_gla, w_pool, o_hn, o_ret, o_s5, o_gla, o_pool):
    hn = _rms(h_ref[...], g_ref[...]).astype(BF16)
    o_hn[...] = hn
    o_ret[...] = _dot(hn, w_ret[...])
    o_s5[...] = _dot(hn, w_s5[...])
    o_gla[...] = _dot(hn, w_gla[...])
    o_pool[...] = _dot(hn, w_pool[...])


def _proj(h, g, ws, layer):
    R, D = h.shape
    tm = min(R, ROW_TILE)
    widths = [w.shape[2] for w in ws]
    return pl.pallas_call(
        _proj_kernel,
        grid=(R // tm,),
        in_specs=[pl.BlockSpec((tm, D), lambda i: (i, 0)), _const_spec((1, D))]
        + [_layer_spec(w, layer) for w in ws],
        out_specs=[pl.BlockSpec((tm, n), lambda i: (i, 0)) for n in [D] + widths],
        out_shape=[jax.ShapeDtypeStruct((R, D), BF16)]
        + [jax.ShapeDtypeStruct((R, n), F32) for n in widths],
        compiler_params=_cparams(("parallel",)),
        name="mixer_proj",
    )(h, g.reshape(1, D), *ws)


def _ret_consts(heads, dk, dv):
    log_g = np.log(1.0 - 2.0 ** (-5.0 - np.arange(heads, dtype=np.float64)))
    idx = np.arange(CHUNK, dtype=np.float64)
    rel = idx[:, None] - idx[None, :]
    dmask = np.where(rel >= 0, np.exp(np.maximum(rel, 0.0)[None] * log_g[:, None, None]), 0.0)
    dmask = np.concatenate(list(dmask), axis=1)
    lane_g = np.repeat(log_g, dk)[None, :]
    qdec = np.exp((idx[:, None] + 1.0) * lane_g)
    kdec = np.exp((CHUNK - 1.0 - idx)[:, None] * lane_g)
    sdec = np.exp(CHUNK * np.repeat(log_g, dk))[:, None] * np.ones((1, heads * dv))
    bmask = np.kron(np.eye(heads), np.ones((dk, dv)))
    f = lambda a: jnp.asarray(a, F32)
    return f(dmask), f(qdec), f(kdec), f(sdec), f(bmask)


def _rope_kernel(ang_ref, cos_ref, ssin_ref, *, half):
    ang = ang_ref[...]
    lane = lax.broadcasted_iota(jnp.int32, ang.shape, 1)
    cos_ref[...] = jnp.cos(ang)
    s = jnp.sin(ang)
    ssin_ref[...] = jnp.where((lane % (2 * half)) < half, -s, s)


def _rope_tables(ang, half):
    R, W = ang.shape
    tm = min(R, 2048)
    spec = pl.BlockSpec((tm, W), lambda i: (i, 0))
    return pl.pallas_call(
        functools.partial(_rope_kernel, half=half),
        grid=(R // tm,),
        in_specs=[spec],
        out_specs=[spec, spec],
        out_shape=[jax.ShapeDtypeStruct((R, W), F32)] * 2,
        compiler_params=_cparams(("parallel",)),
        name="rope_tables",
    )(ang)


def _ret_body(x_ref, cos_ref, ssin_ref, dmask_ref, qdec_ref, kdec_ref, sdec_ref, bmask_ref,
              o_ref, state_ref, *, heads, dk, dv, n_chunks):
    hk = heads * dk
    hv = heads * dv
    lane = lax.broadcasted_iota(jnp.int32, (CHUNK, hk), 1)
    first_half = (lane % dk) < (dk // 2)

    def rotary(x, cos2, ssin):
        swapped = jnp.where(first_half, pltpu.roll(x, hk - dk // 2, 1), pltpu.roll(x, dk // 2, 1))
        return x * cos2 + swapped * ssin

    def chunk(ci, carry):
        r0 = pl.multiple_of(ci * CHUNK, CHUNK)
        rows = pl.ds(r0, CHUNK)
        reps = hk // LANE
        cos2 = jnp.concatenate([cos_ref[rows, :]] * reps, axis=1)
        ssin = jnp.concatenate([ssin_ref[rows, :]] * reps, axis=1)
        q = rotary(x_ref[rows, 0:hk], cos2, ssin)
        k = rotary(x_ref[rows, hk:2 * hk], cos2, ssin) * (dk ** -0.5)
        v = x_ref[rows, 2 * hk:2 * hk + hv]
        gate = x_ref[rows, 2 * hk + hv:2 * hk + 2 * hv]
        vb = v.astype(BF16)
        state = state_ref[...]
        cross = _dot((q * qdec_ref[...]).astype(BF16), state.astype(BF16))
        sb = (_dot(q.astype(BF16), _key_blocks(k, heads, dk)) * dmask_ref[...]).astype(BF16)
        outs = [_dot(sb[:, h * CHUNK:(h + 1) * CHUNK], vb[:, h * dv:(h + 1) * dv]) for h in range(heads)]
        o = jnp.concatenate(outs, axis=1) + cross
        kd = (k * kdec_ref[...]).astype(BF16)
        upd = _dot(kd.T, vb)
        state_ref[...] = sdec_ref[...] * state + bmask_ref[...] * upd
        y = jnp.concatenate(
            [_head_norm(o[:, h * dv:(h + 1) * dv]) for h in range(heads)], axis=1)
        o_ref[rows, :] = (y * _silu(gate)).astype(o_ref.dtype)
        return carry

    lax.fori_loop(0, n_chunks, chunk, 0, unroll=True)


def _ret_pool_kernel(x_ref, cos_ref, ssin_ref, dmask_ref, qdec_ref, kdec_ref, sdec_ref, bmask_ref,
                     p_ref, pw_ref, psc_ref, o_ref, po_ref, state_ref, ext_ref,
                     *, heads, dk, dv, n_chunks, tt):
    @pl.when(pl.program_id(1) == 0)
    def _():
        state_ref[...] = jnp.zeros_like(state_ref)
        ext_ref[0:POOL_PAD, :] = jnp.zeros((POOL_PAD, ext_ref.shape[1]), F32)

    _pool_body(p_ref, pw_ref, psc_ref, po_ref, ext_ref, tt=tt)
    _ret_body(x_ref, cos_ref, ssin_ref, dmask_ref, qdec_ref, kdec_ref, sdec_ref, bmask_ref,
              o_ref, state_ref, heads=heads, dk=dk, dv=dv, n_chunks=n_chunks)


def _retention_pool(x, rope, p, pw, pscale, B, S, heads, dk, dv):
    tt = min(S, SEQ_TILE)
    nt = S // tt
    win = 2 * heads * dk + 2 * heads * dv
    pwid = pw.shape[0] * LANE
    consts = _ret_consts(heads, dk, dv)
    kern = functools.partial(_ret_pool_kernel, heads=heads, dk=dk, dv=dv, n_chunks=tt // CHUNK, tt=tt)
    seq = lambda width: pl.BlockSpec((tt, width), _seq_map(nt))
    return pl.pallas_call(
        kern,
        grid=(B, nt),
        in_specs=[seq(win), seq(LANE), seq(LANE)] + [_const_spec(c.shape) for c in consts]
        + [seq(pwid), _const_spec(pw.shape), _const_spec(pscale.shape)],
        out_specs=[seq(heads * dv), seq(pwid)],
        out_shape=[jax.ShapeDtypeStruct((B * S, heads * dv), BF16),
                   jax.ShapeDtypeStruct((B * S, pwid), BF16)],
        scratch_shapes=[pltpu.VMEM((heads * dk, heads * dv), F32),
                        pltpu.VMEM((tt + POOL_PAD, pwid), F32)],
        compiler_params=_cparams(("parallel", "arbitrary")),
        name="retention_pool",
    )(x, *rope, *consts, p, pw, pscale)


def _gla_consts(heads, dk, dv):
    C = CHUNK
    n_lvl = int(math.log2(C))
    t = np.arange(C)
    sums = []
    masks = []
    for l in range(n_lvl):
        bit = (t >> l) & 1
        start = (t >> l) << l
        r = t[None, :]
        upper = (bit[:, None] == 1) & (r >= start[:, None]) & (r <= t[:, None])
        lower = (bit[:, None] == 0) & (r > t[:, None]) & (r < (start + (1 << l))[:, None])
        sums.append((upper | lower).astype(np.float64))
        m = ((bit[:, None] == 1) & (bit[None, :] == 0)
             & ((t[:, None] >> (l + 1)) == (t[None, :] >> (l + 1))))
        masks.append(np.tile(m.astype(np.float64), (1, heads)))
    masks.append(np.tile(np.eye(C), (1, heads)))
    sums.append((t[None, :] <= t[:, None]).astype(np.float64))
    sums.append((t[None, :] > t[:, None]).astype(np.float64))
    g = np.concatenate(sums, axis=0)
    gcat = np.concatenate([g, g], axis=1)
    bmask = np.kron(np.eye(heads), np.ones((dk, dv)))
    return jnp.asarray(gcat, BF16), jnp.asarray(np.stack(masks), F32), jnp.asarray(bmask, F32)


def _log_sigmoid(x):
    return jnp.minimum(x, 0.0) - jnp.log(1.0 + jnp.exp(-jnp.abs(x)))


def _gla_kernel(x_ref, wg_ref, bg_ref, gcat_ref, masks_ref, bmask_ref, o_ref, state_ref,
                *, heads, dk, dv, n_chunks):
    hk = heads * dk
    hv = heads * dv
    C = CHUNK
    n_lvl = int(math.log2(C))

    @pl.when(pl.program_id(1) == 0)
    def _():
        state_ref[...] = jnp.zeros_like(state_ref)

    key_blocks = functools.partial(_key_blocks, heads=heads, dk=dk)

    def chunk(ci, carry):
        r0 = pl.multiple_of(ci * C, C)
        rows = pl.ds(r0, C)
        q = x_ref[rows, 0:hk] * (dk ** -0.5)
        k = x_ref[rows, hk:2 * hk]
        v = x_ref[rows, 2 * hk:2 * hk + hv]
        gate = x_ref[rows, 2 * hk + hv:2 * hk + 2 * hv]
        code = x_ref[rows, 2 * hk + 2 * hv:2 * hk + 2 * hv + LANE]
        vb = v.astype(BF16)
        log_a = _log_sigmoid(_dot(code.astype(BF16), wg_ref[...]) + bg_ref[...]) * (1.0 / GLA_GATE_TEMP)
        hi = log_a.astype(BF16)
        lo = (log_a - hi.astype(F32)).astype(BF16)
        sums = _dot(gcat_ref[...], jnp.concatenate([hi, lo], axis=0))

        scores = masks_ref[n_lvl] * _dot(q.astype(BF16), key_blocks(k))
        for l in range(n_lvl):
            z = jnp.exp(sums[l * C:(l + 1) * C])
            scores = scores + masks_ref[l] * _dot((q * z).astype(BF16), key_blocks(k * z))
        sb = scores.astype(BF16)
        intra = jnp.concatenate(
            [_dot(sb[:, h * C:(h + 1) * C], vb[:, h * dv:(h + 1) * dv]) for h in range(heads)], axis=1)

        state = state_ref[...]
        e_pre = sums[n_lvl * C:(n_lvl + 1) * C]
        e_suf = sums[(n_lvl + 1) * C:(n_lvl + 2) * C]
        cross = _dot((q * jnp.exp(e_pre)).astype(BF16), state.astype(BF16))
        kd = (k * jnp.exp(e_suf)).astype(BF16)
        upd = _dot(kd.T, vb)
        total = jnp.broadcast_to(jnp.exp(e_pre[C - 1:C, :]), (LANE, hk))
        dec = jnp.concatenate([total.T] * (hv // LANE), axis=1)
        state_ref[...] = dec * state + bmask_ref[...] * upd

        o = intra + cross
        y = jnp.concatenate(
            [_head_norm(o[:, h * dv:(h + 1) * dv]) for h in range(heads)], axis=1)
        o_ref[rows, :] = (y * _silu(gate)).astype(o_ref.dtype)
        return carry

    lax.fori_loop(0, n_chunks, chunk, 0)


def _gla(x, wg, bg, B, S, heads, dk, dv):
    tt = min(S, SEQ_TILE)
    nt = S // tt
    win = 2 * heads * dk + 2 * heads * dv + LANE
    consts = _gla_consts(heads, dk, dv)
    kern = functools.partial(_gla_kernel, heads=heads, dk=dk, dv=dv, n_chunks=tt // CHUNK)
    return pl.pallas_call(
        kern,
        grid=(B, nt),
        in_specs=[pl.BlockSpec((tt, win), _seq_map(nt)),
                  _const_spec(wg.shape), _const_spec(bg.shape)]
        + [_const_spec(c.shape) for c in consts],
        out_specs=pl.BlockSpec((tt, heads * dv), _seq_map(nt)),
        out_shape=jax.ShapeDtypeStruct((B * S, heads * dv), BF16),
        scratch_shapes=[pltpu.VMEM((heads * dk, heads * dv), F32)],
        compiler_params=_cparams(("parallel", "arbitrary")),
        name="gla",
    )(x, wg, bg, *consts)


def _s5_kernel(u_ref, a_ref, bblk_ref, cblk_ref, d_ref, wglu_ref, o_ref,
               stage_ref, utb_ref, xs_ref, ytb_ref, st_ref, *, tt):
    nb, w = u_ref.shape[0], u_ref.shape[2]
    nblk = w // LANE
    sw = a_ref.shape[1] // (2 * nblk)

    @pl.when(pl.program_id(0) == 0)
    def _():
        st_ref[...] = jnp.zeros_like(st_ref)

    for b in range(nb):
        for m in range(nblk):
            stage_ref[m, b * S5_PITCH:b * S5_PITCH + tt, :] = u_ref[b, :, m * LANE:(m + 1) * LANE]

    def gather(t, carry):
        rows = pl.ds(pl.multiple_of(t * nb, nb), nb)
        for m in range(nblk):
            utb_ref[rows, m * LANE:(m + 1) * LANE] = stage_ref[m, pl.ds(t, nb, stride=S5_PITCH), :]
        return carry

    lax.fori_loop(0, tt, gather, 0)

    u = utb_ref[...]
    ub = u.astype(BF16)
    for m in range(nblk):
        xs_ref[:, 2 * sw * m:2 * sw * (m + 1)] = _dot(ub[:, m * LANE:(m + 1) * LANE], bblk_ref[m])

    def step(t, carry):
        rows = pl.ds(pl.multiple_of(t * nb, nb), nb)
        new = []
        for m in range(nblk):
            xr, xi = carry[2 * m], carry[2 * m + 1]
            re = slice(2 * sw * m, 2 * sw * m + sw)
            im = slice(2 * sw * m + sw, 2 * sw * (m + 1))
            ar, ai = a_ref[:, re], a_ref[:, im]
            nr = ar * xr - ai * xi + xs_ref[rows, re]
            ni = ar * xi + ai * xr + xs_ref[rows, im]
            xs_ref[rows, re] = nr
            xs_ref[rows, im] = ni
            new += [nr, ni]
        return tuple(new)

    init = tuple(st_ref[:, sw * c:sw * (c + 1)] for c in range(2 * nblk))
    last = lax.fori_loop(0, tt, step, init, unroll=2)
    for c in range(2 * nblk):
        st_ref[:, sw * c:sw * (c + 1)] = last[c]

    ys = [_dot(xs_ref[:, 2 * sw * m:2 * sw * (m + 1)].astype(BF16), cblk_ref[m]) for m in range(nblk)]
    y = jnp.concatenate(ys, axis=1) + d_ref[...] * u
    y = jax.nn.gelu(y)
    y = y * jax.nn.sigmoid(_dot(y.astype(BF16), wglu_ref[...]))

    for m in range(nblk):
        ytb_ref[m] = y[:, m * LANE:(m + 1) * LANE]
    for b in range(nb):
        for m in range(nblk):
            o_ref[b, :, m * LANE:(m + 1) * LANE] = (
                ytb_ref[m, pl.ds(b, tt, stride=nb), :].astype(o_ref.dtype))


def _s5(u3, a_row, bblk, cblk, d_row, wglu, layer):
    B, S, W = u3.shape
    tt = min(S, S5_TILE)
    ns2 = a_row.shape[1]
    nblk = W // LANE
    a8 = jnp.broadcast_to(a_row, (B, ns2))
    return pl.pallas_call(
        functools.partial(_s5_kernel, tt=tt),
        grid=(S // tt,),
        in_specs=[pl.BlockSpec((B, tt, W), lambda t: (0, t, 0)),
                  _const_spec(a8.shape), _layer_spec(bblk, layer), _layer_spec(cblk, layer),
                  _const_spec(d_row.shape), _layer_spec(wglu, layer)],
        out_specs=pl.BlockSpec((B, tt, W), lambda t: (0, t, 0)),
        out_shape=jax.ShapeDtypeStruct((B, S, W), BF16),
        scratch_shapes=[pltpu.VMEM((nblk, B * S5_PITCH, LANE), F32),
                        pltpu.VMEM((tt * B, W), F32),
                        pltpu.VMEM((tt * B, ns2), F32),
                        pltpu.VMEM((nblk, tt * B, LANE), F32),
                        pltpu.VMEM((B, ns2), F32)],
        compiler_params=_cparams(("arbitrary",)),
        name="s5",
    )(u3, a8, bblk, cblk, d_row, wglu)


def _s5_params(a_re, a_im, log_dt, b_re, b_im, c_re, c_im):
    G, P = a_re.shape
    n_c = b_re.shape[-1]
    gpb = LANE // n_c
    nblk = G // gpb
    dt = jnp.exp(log_dt)[:, None]
    mag = jnp.exp(a_re * dt)
    abar_re = mag * jnp.cos(a_im * dt)
    abar_im = mag * jnp.sin(a_im * dt)
    den = a_re * a_re + a_im * a_im
    nr, ni = abar_re - 1.0, abar_im
    f_re = (nr * a_re + ni * a_im) / den
    f_im = (ni * a_re - nr * a_im) / den
    bb_re = f_re[..., None] * b_re - f_im[..., None] * b_im
    bb_im = f_re[..., None] * b_im + f_im[..., None] * b_re
    eye = jnp.eye(gpb, dtype=F32)

    def bd_in(m):
        m = m.reshape(nblk, gpb, P, n_c)
        return jnp.einsum('ngpc,gh->ngchp', m, eye).reshape(nblk, gpb * n_c, gpb * P)

    def bd_out(m):
        m = m.reshape(nblk, gpb, n_c, P)
        return jnp.einsum('ngcp,gh->ngphc', m, eye).reshape(nblk, gpb * P, gpb * n_c)

    a_row = jnp.concatenate([abar_re.reshape(nblk, 1, gpb * P), abar_im.reshape(nblk, 1, gpb * P)],
                            axis=2).reshape(1, 2 * G * P)
    bblk = jnp.concatenate([bd_in(bb_re), bd_in(bb_im)], axis=2).astype(BF16)
    cblk = jnp.concatenate([bd_out(c_re), -bd_out(c_im)], axis=1).astype(BF16)
    return a_row, bblk, cblk


def _pool_body(p_ref, w_ref, sc_ref, o_ref, ext_ref, *, tt):
    pad = POOL_PAD
    t_idx = pl.program_id(1)
    ext_ref[pad:pad + tt, :] = p_ref[...]
    pos1 = (lax.broadcasted_iota(jnp.int32, (tt, LANE), 0) + t_idx * tt + 1).astype(F32)
    outs = []
    for g, win in enumerate(POOL_WINDOWS):
        cols = slice(g * LANE, (g + 1) * LANE)
        acc = ext_ref[:, cols]
        k = 1
        while k < win:
            acc = acc + pltpu.roll(acc, k, 0)
            k *= 2
        acc = acc[pad:pad + tt]
        cur = ext_ref[pad:pad + tt, cols]
        mixed = acc / jnp.minimum(pos1, float(win)) - cur
        outs.append(_dot(mixed.astype(BF16), w_ref[g]))
    o_ref[...] = (jnp.concatenate(outs, axis=1) * sc_ref[...]).astype(o_ref.dtype)
    ext_ref[0:pad, :] = ext_ref[tt:tt + pad, :]


def _merge_kernel(hn_ref, *refs):
    ys = refs[0:N_BRANCH]
    wgates = refs[N_BRANCH:2 * N_BRANCH]
    wbrs = refs[2 * N_BRANCH:3 * N_BRANCH]
    o_ref = refs[3 * N_BRANCH]
    tm = o_ref.shape[0]
    for rb in range(tm // MERGE_SUB):
        rows = slice(rb * MERGE_SUB, (rb + 1) * MERGE_SUB)
        hn = hn_ref[rows, :]
        acc = None
        for n in range(N_BRANCH):
            logits = _dot(hn, wgates[n][...])
            term = jax.nn.sigmoid(logits) * _dot(ys[n][rows, :], wbrs[n][...])
            acc = term if acc is None else acc + term
        o_ref[rows, :] = acc.astype(o_ref.dtype)


def _merge(hn, ys, wgate, wbr, layer):
    R, D = hn.shape
    tm = min(R, MERGE_ROW_TILE)
    bw = ys[0].shape[1]
    tn = MERGE_TILE
    nc = D // tn
    gate_spec = lambda n: pl.BlockSpec((None, D, tn), lambda i, c: (layer, 0, n * nc + c))
    br_spec = lambda n: pl.BlockSpec((None, None, bw, tn), lambda i, c: (layer, n, 0, c))
    return pl.pallas_call(
        _merge_kernel,
        grid=(R // tm, nc),
        in_specs=[pl.BlockSpec((tm, D), lambda i, c: (i, 0))]
        + [pl.BlockSpec((tm, bw), lambda i, c: (i, 0)) for _ in ys]
        + [gate_spec(n) for n in range(N_BRANCH)] + [br_spec(n) for n in range(N_BRANCH)],
        out_specs=pl.BlockSpec((tm, tn), lambda i, c: (i, c)),
        out_shape=jax.ShapeDtypeStruct((R, D), BF16),
        compiler_params=_cparams(("parallel", "arbitrary")),
        name="branch_merge",
    )(hn, *ys, *([wgate] * N_BRANCH), *([wbr] * N_BRANCH))


def _outproj_kernel(h_ref, m_ref, w_ref, g_ref, o_ref, hn_ref):
    for rb in range(h_ref.shape[0] // OUT_SUB):
        rows = slice(rb * OUT_SUB, (rb + 1) * OUT_SUB)
        h = h_ref[rows, :] + _dot(m_ref[rows, :], w_ref[...])
        o_ref[rows, :] = h
        hn_ref[rows, :] = _rms(h, g_ref[...]).astype(BF16)


def _outproj(h, merged, w, g, layer):
    R, D = h.shape
    tm = min(R, ROW_TILE)
    row_spec = pl.BlockSpec((tm, D), lambda i: (i, 0))
    return pl.pallas_call(
        _outproj_kernel,
        grid=(R // tm,),
        in_specs=[row_spec, row_spec, _layer_spec(w, layer), _const_spec((1, D))],
        out_specs=[row_spec, row_spec],
        out_shape=[jax.ShapeDtypeStruct((R, D), F32), jax.ShapeDtypeStruct((R, D), BF16)],
        compiler_params=_cparams(("parallel",)),
        name="out_proj",
    )(h, merged, w, g.reshape(1, D))


def _ffn_up_kernel(hn_ref, wa_ref, wv_ref, cw_ref, cb_ref, o_ref, carry_ref, *, tn, tiles_per_seq):
    i = pl.program_id(0)
    j = pl.program_id(1)
    tm = hn_ref.shape[0]

    @pl.when(i % tiles_per_seq == 0)
    def _():
        carry_ref[j] = jnp.zeros((SUBLANE, 2 * tn), F32)

    row8 = lax.broadcasted_iota(jnp.int32, (SUBLANE, tn), 0)

    def delayed(u, prev8, k):
        r = pltpu.roll(u, k, 0)
        top = jnp.where(row8 < k, pltpu.roll(prev8, k, 0), r[0:SUBLANE])
        return jnp.concatenate([top, r[SUBLANE:]], axis=0)

    def conv(u, prev8, col0):
        cols = slice(col0, col0 + tn)
        out = cb_ref[:, cols] + u * cw_ref[CONV_W - 1:CONV_W, cols]
        for tap in range(CONV_W - 1):
            out = out + delayed(u, prev8, CONV_W - 1 - tap) * cw_ref[tap:tap + 1, cols]
        return out

    def up(rb):
        hn = hn_ref[rb * FF_SUB:(rb + 1) * FF_SUB, :]
        return _dot(hn, wa_ref[...]), _dot(hn, wv_ref[...])

    n_sub = tm // FF_SUB
    prev = carry_ref[j]
    prev_a, prev_v = prev[:, 0:tn], prev[:, tn:2 * tn]
    nxt = up(0)
    for rb in range(n_sub):
        ua, uv = nxt
        if rb + 1 < n_sub:
            nxt = up(rb + 1)
        a = conv(ua, prev_a, 0)
        v = conv(uv, prev_v, tn)
        prev_a, prev_v = ua[FF_SUB - SUBLANE:], uv[FF_SUB - SUBLANE:]
        o_ref[rb * FF_SUB:(rb + 1) * FF_SUB, :] = (_silu(a) * v).astype(o_ref.dtype)
    carry_ref[j] = jnp.concatenate([prev_a, prev_v], axis=1)


def _ffn_up(hn, wa, wv, cw, cb, S, layer):
    R, D = hn.shape
    tm = min(S, FF_ROW_TILE)
    tn = FF_TILE
    ffp = wa.shape[2]
    nj = ffp // tn
    return pl.pallas_call(
        functools.partial(_ffn_up_kernel, tn=tn, tiles_per_seq=S // tm),
        grid=(R // tm, nj),
        in_specs=[pl.BlockSpec((tm, D), lambda i, j: (i, 0)),
                  pl.BlockSpec((None, D, tn), lambda i, j: (layer, 0, j)),
                  pl.BlockSpec((None, D, tn), lambda i, j: (layer, 0, j)),
                  pl.BlockSpec((None, None, CONV_W, 2 * tn), lambda i, j: (layer, j, 0, 0)),
                  pl.BlockSpec((None, None, 1, 2 * tn), lambda i, j: (layer, j, 0, 0))],
        out_specs=pl.BlockSpec((tm, tn), lambda i, j: (i, j)),
        out_shape=jax.ShapeDtypeStruct((R, ffp), BF16),
        scratch_shapes=[pltpu.VMEM((nj, SUBLANE, 2 * tn), F32)],
        compiler_params=_cparams(("arbitrary", "arbitrary")),
        name="ffn_up",
    )(hn, wa, wv, cw, cb)


def _ffn_down_kernel(h_ref, act_ref, wd_ref, g_ref, o_ref, *, final):
    for rb in range(h_ref.shape[0] // OUT_SUB):
        rows = slice(rb * OUT_SUB, (rb + 1) * OUT_SUB)
        h = h_ref[rows, :] + _dot(act_ref[rows, :], wd_ref[...])
        o_ref[rows, :] = _rms(h, g_ref[...]) if final else h


def _ffn_down(h, act, wd, g, final, layer):
    R, D = h.shape
    tm = min(R, ROW_TILE)
    ff = wd.shape[1]
    row_spec = pl.BlockSpec((tm, D), lambda i: (i, 0))
    return pl.pallas_call(
        functools.partial(_ffn_down_kernel, final=final),
        grid=(R // tm,),
        in_specs=[row_spec, pl.BlockSpec((tm, ff), lambda i: (i, 0)),
                  _layer_spec(wd, layer), _const_spec((1, D))],
        out_specs=row_spec,
        out_shape=jax.ShapeDtypeStruct((R, D), F32),
        compiler_params=_cparams(("parallel",)),
        name="ffn_down",
    )(h, act, wd, g.reshape(1, D))


def _pad_cols(w, n):
    return jnp.pad(w, ((0, 0), (0, n - w.shape[1])))


def _split_w_in(w_in, D):
    bw = D // 4
    hk = bw // 2
    sizes = [hk, hk, bw, bw, bw, hk, hk, bw, bw, GLA_RANK, bw, N_BRANCH * D]
    offs = np.concatenate([[0], np.cumsum(sizes)])
    seg = lambda a, b: w_in[:, offs[a]:offs[b]]
    w_ret = seg(0, 4)
    w_s5 = seg(4, 5)
    w_gla = jnp.concatenate([seg(5, 9), _pad_cols(seg(9, 10), LANE)], axis=1)
    w_pool = seg(10, 11)
    return [w.astype(BF16) for w in (w_ret, w_s5, w_gla, w_pool)], seg(11, 12).astype(BF16)


def _ffn_weights(w_up, conv_w, conv_b, w_down, tn):
    D, ff2 = w_up.shape
    ff = ff2 // 2
    ffp = -(-ff // tn) * tn
    nj = ffp // tn
    wa = _pad_cols(w_up[:, :ff], ffp).astype(BF16)
    wv = _pad_cols(w_up[:, ff:], ffp).astype(BF16)
    tiles = lambda m: _pad_cols(m, ffp).reshape(m.shape[0], nj, tn)
    cw = jnp.concatenate([tiles(conv_w[:, :ff]), tiles(conv_w[:, ff:])], axis=2).transpose(1, 0, 2)
    cbr = conv_b.reshape(1, ff2)
    cb = jnp.concatenate([tiles(cbr[:, :ff]), tiles(cbr[:, ff:])], axis=2).transpose(1, 0, 2)
    return wa, wv, cw, cb, w_down.astype(BF16)


def kernel(x, positions, norm_mix_g, w_in, s5_a_re, s5_a_im, s5_log_dt, s5_b_re, s5_b_im, s5_c_re, s5_c_im, s5_d, s5_w_glu, gla_w_gate, gla_b_gate, pool_w, pool_scale, w_branch, w_out, norm_ffn_g, w_up, conv_w, conv_b, w_down, final_g):
    B, S, D = x.shape
    depth = w_in.shape[0]
    bw = D // 4
    ret_dk, ret_dv = bw // (2 * RET_HEADS), bw // RET_HEADS
    gla_dk, gla_dv = bw // (2 * GLA_HEADS), bw // GLA_HEADS
    assert S % CHUNK == 0 and S % min(S, ROW_TILE) == 0 and B == SUBLANE
    assert ret_dk // 2 * 4 == LANE and bw == S5_BLOCK * LANE

    inv = ROPE_BASE ** (-jnp.arange(0, ret_dk, 2, dtype=F32) / ret_dk)
    ang = positions.astype(F32)[..., None] * inv
    ang = jnp.tile(ang, (1, 1, LANE // (ret_dk // 2))).reshape(B * S, LANE)
    rope = _rope_tables(ang, ret_dk // 2)

    ws, w_gate = jax.vmap(lambda w: _split_w_in(w, D))(w_in)
    a_row, bblk, cblk = jax.vmap(_s5_params)(s5_a_re, s5_a_im, s5_log_dt, s5_b_re, s5_b_im,
                                             s5_c_re, s5_c_im)
    wglu = s5_w_glu.astype(BF16)
    wg = jnp.pad(gla_w_gate, ((0, 0), (0, LANE - GLA_RANK), (0, 0))).astype(BF16)
    pw = pool_w.astype(BF16)
    wbr = w_branch.astype(BF16)
    wo = w_out.astype(BF16)
    wa, wv, cw, cb, wd = jax.vmap(lambda a, b, c, d: _ffn_weights(a, b, c, d, FF_TILE))(
        w_up, conv_w, conv_b, w_down)

    h = x.reshape(B * S, D)
    for l in range(depth):
        hn, p_ret, p_s5, p_gla, p_pool = _proj(h, norm_mix_g[l], ws, l)
        ya, yd = _retention_pool(p_ret, rope, p_pool, pw[l], pool_scale[l].reshape(1, bw),
                                 B, S, RET_HEADS, ret_dk, ret_dv)
        yb = _s5(p_s5.reshape(B, S, bw), a_row[l], bblk, cblk, s5_d[l].reshape(1, bw),
                 wglu, l).reshape(B * S, bw)
        yc = _gla(p_gla, wg[l], gla_b_gate[l].reshape(1, -1), B, S, GLA_HEADS, gla_dk, gla_dv)
        merged = _merge(hn, [ya, yb, yc, yd], w_gate, wbr, l)
        h, hn = _outproj(h, merged, wo, norm_ffn_g[l], l)
        act = _ffn_up(hn, wa, wv, cw, cb, S, l)
        h = _ffn_down(h, act, wd, final_g, final=(l == depth - 1), layer=l)
    return h.reshape(B, S, D)
```

```python
import functools
import math

import numpy as np
import jax
import jax.numpy as jnp
from jax import lax
from jax.experimental import pallas as pl
from jax.experimental.pallas import tpu as pltpu

F32 = jnp.float32
BF16 = jnp.bfloat16
EPS = 1e-6

N_BRANCH = 4
RET_HEADS = 4
GLA_HEADS = 4
GLA_RANK = 16
GLA_GATE_TEMP = 16.0
S5_GROUP = 16
S5_STATE = 64
POOL_WINDOWS = (2, 4, 8, 16)
POOL_PAD = 16
ROPE_BASE = 10000.0
CONV_W = 3

LANE = 128
SUBLANE = 8
CHUNK = 128
VMEM_LIMIT = 56 * 1024 * 1024
ROW_TILE = 512
MERGE_ROW_TILE = 1024
MERGE_TILE = 512
MERGE_SUB = 256
OUT_SUB = 256
FF_ROW_TILE = 2048
FF_TILE = 512
FF_SUB = 1024
SEQ_TILE = 512
S5_TILE = 128
S5_PITCH = S5_TILE + 4
S5_BLOCK = 4


def _cparams(sem):
    return pltpu.CompilerParams(dimension_semantics=sem, vmem_limit_bytes=VMEM_LIMIT)


def _const_spec(shape):
    nd = len(shape)
    return pl.BlockSpec(shape, lambda *_: (0,) * nd, pipeline_mode=pl.Buffered(1))


def _layer_spec(arr, layer):
    nd = arr.ndim - 1
    return pl.BlockSpec((None,) + arr.shape[1:], lambda *_: (layer,) + (0,) * nd,
                        pipeline_mode=pl.Buffered(1))


def _dot(a, b):
    return jnp.dot(a, b, preferred_element_type=F32)


def _dot_nt(a, b):
    return lax.dot_general(a, b, (((1,), (1,)), ((), ())), preferred_element_type=F32)


def _rms(x, g):
    r = lax.rsqrt(jnp.mean(x * x, axis=-1, keepdims=True) + EPS)
    return x * r * g


def _head_norm(o):
    mu = jnp.mean(o, axis=-1, keepdims=True)
    d = o - mu
    var = jnp.mean(d * d, axis=-1, keepdims=True)
    return d * lax.rsqrt(var + EPS)


def _silu(x):
    return x * jax.nn.sigmoid(x)


def _key_blocks(k, heads, dk):
    kt = k.T.astype(BF16)
    zero = jnp.zeros((dk, kt.shape[1]), BF16)
    cols = [jnp.concatenate([kt[h * dk:(h + 1) * dk] if r == h else zero for r in range(heads)], axis=0)
            for h in range(heads)]
    return jnp.concatenate(cols, axis=1)


def _seq_map(nt):
    return lambda b, t: (b * nt + t, 0)


def _proj_kernel(h_ref, g_ref, w_ret, w_s5, w_gla, w_pool, o_hn, o_ret, o_s5, o_gla, o_pool):
    hn = _rms(h_ref[...], g_ref[...]).astype(BF16)
    o_hn[...] = hn
    o_ret[...] = _dot(hn, w_ret[...])
    o_s5[...] = _dot(hn, w_s5[...])
    o_gla[...] = _dot(hn, w_gla[...])
    o_pool[...] = _dot(hn, w_pool[...])


def _proj(h, g, ws, layer):
    R, D = h.shape
    tm = min(R, ROW_TILE)
    widths = [w.shape[2] for w in ws]
    return pl.pallas_call(
        _proj_kernel,
        grid=(R // tm,),
        in_specs=[pl.BlockSpec((tm, D), lambda i: (i, 0)), _const_spec((1, D))]
        + [_layer_spec(w, layer) for w in ws],
        out_specs=[pl.BlockSpec((tm, n), lambda i: (i, 0)) for n in [D] + widths],
        out_shape=[jax.ShapeDtypeStruct((R, D), BF16)]
        + [jax.ShapeDtypeStruct((R, n), F32) for n in widths],
        compiler_params=_cparams(("parallel",)),
        name="mixer_proj",
    )(h, g.reshape(1, D), *ws)


def _ret_consts(heads, dk, dv):
    log_g = np.log(1.0 - 2.0 ** (-5.0 - np.arange(heads, dtype=np.float64)))
    idx = np.arange(CHUNK, dtype=np.float64)
    rel = idx[:, None] - idx[None, :]
    dmask = np.where(rel >= 0, np.exp(np.maximum(rel, 0.0)[None] * log_g[:, None, None]), 0.0)
    dmask = np.concatenate(list(dmask), axis=1)
    lane_g = np.repeat(log_g, dk)[None, :]
    qdec = np.exp((idx[:, None] + 1.0) * lane_g)
    kdec = np.exp((CHUNK - 1.0 - idx)[:, None] * lane_g)
    sdec = np.exp(CHUNK * np.repeat(log_g, dk))[:, None] * np.ones((1, heads * dv))
    bmask = np.kron(np.eye(heads), np.ones((dk, dv)))
    f = lambda a: jnp.asarray(a, F32)
    return f(dmask), f(qdec), f(kdec), f(sdec), f(bmask)


def _rope_kernel(ang_ref, cos_ref, ssin_ref, *, half):
    ang = ang_ref[...]
    lane = lax.broadcasted_iota(jnp.int32, ang.shape, 1)
    cos_ref[...] = jnp.cos(ang)
    s = jnp.sin(ang)
    ssin_ref[...] = jnp.where((lane % (2 * half)) < half, -s, s)


def _rope_tables(ang, half):
    R, W = ang.shape
    tm = min(R, 2048)
    spec = pl.BlockSpec((tm, W), lambda i: (i, 0))
    return pl.pallas_call(
        functools.partial(_rope_kernel, half=half),
        grid=(R // tm,),
        in_specs=[spec],
        out_specs=[spec, spec],
        out_shape=[jax.ShapeDtypeStruct((R, W), F32)] * 2,
        compiler_params=_cparams(("parallel",)),
        name="rope_tables",
    )(ang)


def _ret_body(x_ref, cos_ref, ssin_ref, dmask_ref, qdec_ref, kdec_ref, sdec_ref, bmask_ref,
              o_ref, state_ref, *, heads, dk, dv, n_chunks):
    hk = heads * dk
    hv = heads * dv
    lane = lax.broadcasted_iota(jnp.int32, (CHUNK, hk), 1)
    first_half = (lane % dk) < (dk // 2)

    def rotary(x, cos2, ssin):
        swapped = jnp.where(first_half, pltpu.roll(x, hk - dk // 2, 1), pltpu.roll(x, dk // 2, 1))
        return x * cos2 + swapped * ssin

    def chunk(ci, carry):
        r0 = pl.multiple_of(ci * CHUNK, CHUNK)
        rows = pl.ds(r0, CHUNK)
        reps = hk // LANE
        cos2 = jnp.concatenate([cos_ref[rows, :]] * reps, axis=1)
        ssin = jnp.concatenate([ssin_ref[rows, :]] * reps, axis=1)
        q = rotary(x_ref[rows, 0:hk], cos2, ssin)
        k = rotary(x_ref[rows, hk:2 * hk], cos2, ssin) * (dk ** -0.5)
        v = x_ref[rows, 2 * hk:2 * hk + hv]
        gate = x_ref[rows, 2 * hk + hv:2 * hk + 2 * hv]
        vb = v.astype(BF16)
        state = state_ref[...]
        cross = _dot((q * qdec_ref[...]).astype(BF16), state.astype(BF16))
        sb = (_dot(q.astype(BF16), _key_blocks(k, heads, dk)) * dmask_ref[...]).astype(BF16)
        outs = [_dot(sb[:, h * CHUNK:(h + 1) * CHUNK], vb[:, h * dv:(h + 1) * dv]) for h in range(heads)]
        o = jnp.concatenate(outs, axis=1) + cross
        kd = (k * kdec_ref[...]).astype(BF16)
        upd = _dot(kd.T, vb)
        state_ref[...] = sdec_ref[...] * state + bmask_ref[...] * upd
        y = jnp.concatenate(
            [_head_norm(o[:, h * dv:(h + 1) * dv]) for h in range(heads)], axis=1)
        o_ref[rows, :] = (y * _silu(gate)).astype(o_ref.dtype)
        return carry

    lax.fori_loop(0, n_chunks, chunk, 0, unroll=True)


def _ret_pool_kernel(x_ref, cos_ref, ssin_ref, dmask_ref, qdec_ref, kdec_ref, sdec_ref, bmask_ref,
                     p_ref, pw_ref, psc_ref, o_ref, po_ref, state_ref, ext_ref,
                     *, heads, dk, dv, n_chunks, tt):
    @pl.when(pl.program_id(1) == 0)
    def _():
        state_ref[...] = jnp.zeros_like(state_ref)
        ext_ref[0:POOL_PAD, :] = jnp.zeros((POOL_PAD, ext_ref.shape[1]), F32)

    _pool_body(p_ref, pw_ref, psc_ref, po_ref, ext_ref, tt=tt)
    _ret_body(x_ref, cos_ref, ssin_ref, dmask_ref, qdec_ref, kdec_ref, sdec_ref, bmask_ref,
              o_ref, state_ref, heads=heads, dk=dk, dv=dv, n_chunks=n_chunks)


def _retention_pool(x, rope, p, pw, pscale, B, S, heads, dk, dv):
    tt = min(S, SEQ_TILE)
    nt = S // tt
    win = 2 * heads * dk + 2 * heads * dv
    pwid = pw.shape[0] * LANE
    consts = _ret_consts(heads, dk, dv)
    kern = functools.partial(_ret_pool_kernel, heads=heads, dk=dk, dv=dv, n_chunks=tt // CHUNK, tt=tt)
    seq = lambda width: pl.BlockSpec((tt, width), _seq_map(nt))
    return pl.pallas_call(
        kern,
        grid=(B, nt),
        in_specs=[seq(win), seq(LANE), seq(LANE)] + [_const_spec(c.shape) for c in consts]
        + [seq(pwid), _const_spec(pw.shape), _const_spec(pscale.shape)],
        out_specs=[seq(heads * dv), seq(pwid)],
        out_shape=[jax.ShapeDtypeStruct((B * S, heads * dv), BF16),
                   jax.ShapeDtypeStruct((B * S, pwid), BF16)],
        scratch_shapes=[pltpu.VMEM((heads * dk, heads * dv), F32),
                        pltpu.VMEM((tt + POOL_PAD, pwid), F32)],
        compiler_params=_cparams(("parallel", "arbitrary")),
        name="retention_pool",
    )(x, *rope, *consts, p, pw, pscale)


def _gla_consts(heads, dk, dv):
    C = CHUNK
    n_lvl = int(math.log2(C))
    t = np.arange(C)
    sums = []
    masks = []
    for l in range(n_lvl):
        bit = (t >> l) & 1
        start = (t >> l) << l
        r = t[None, :]
        upper = (bit[:, None] == 1) & (r >= start[:, None]) & (r <= t[:, None])
        lower = (bit[:, None] == 0) & (r > t[:, None]) & (r < (start + (1 << l))[:, None])
        sums.append((upper | lower).astype(np.float64))
        m = ((bit[:, None] == 1) & (bit[None, :] == 0)
             & ((t[:, None] >> (l + 1)) == (t[None, :] >> (l + 1))))
        masks.append(np.tile(m.astype(np.float64), (1, heads)))
    masks.append(np.tile(np.eye(C), (1, heads)))
    sums.append((t[None, :] <= t[:, None]).astype(np.float64))
    sums.append((t[None, :] > t[:, None]).astype(np.float64))
    g = np.concatenate(sums, axis=0)
    gcat = np.concatenate([g, g], axis=1)
    bmask = np.kron(np.eye(heads), np.ones((dk, dv)))
    return jnp.asarray(gcat, BF16), jnp.asarray(np.stack(masks), F32), jnp.asarray(bmask, F32)


def _log_sigmoid(x):
    return jnp.minimum(x, 0.0) - jnp.log(1.0 + jnp.exp(-jnp.abs(x)))


def _gla_kernel(x_ref, wg_ref, bg_ref, gcat_ref, masks_ref, bmask_ref, o_ref, state_ref,
                *, heads, dk, dv, n_chunks):
    hk = heads * dk
    hv = heads * dv
    C = CHUNK
    n_lvl = int(math.log2(C))

    @pl.when(pl.program_id(1) == 0)
    def _():
        state_ref[...] = jnp.zeros_like(state_ref)

    key_blocks = functools.partial(_key_blocks, heads=heads, dk=dk)

    def chunk(ci, carry):
        r0 = pl.multiple_of(ci * C, C)
        rows = pl.ds(r0, C)
        q = x_ref[rows, 0:hk] * (dk ** -0.5)
        k = x_ref[rows, hk:2 * hk]
        v = x_ref[rows, 2 * hk:2 * hk + hv]
        gate = x_ref[rows, 2 * hk + hv:2 * hk + 2 * hv]
        code = x_ref[rows, 2 * hk + 2 * hv:2 * hk + 2 * hv + LANE]
        vb = v.astype(BF16)
        log_a = _log_sigmoid(_dot(code.astype(BF16), wg_ref[...]) + bg_ref[...]) * (1.0 / GLA_GATE_TEMP)
        hi = log_a.astype(BF16)
        lo = (log_a - hi.astype(F32)).astype(BF16)
        sums = _dot(gcat_ref[...], jnp.concatenate([hi, lo], axis=0))

        scores = masks_ref[n_lvl] * _dot(q.astype(BF16), key_blocks(k))
        for l in range(n_lvl):
            z = jnp.exp(sums[l * C:(l + 1) * C])
            scores = scores + masks_ref[l] * _dot((q * z).astype(BF16), key_blocks(k * z))
        sb = scores.astype(BF16)
        intra = jnp.concatenate(
            [_dot(sb[:, h * C:(h + 1) * C], vb[:, h * dv:(h + 1) * dv]) for h in range(heads)], axis=1)

        state = state_ref[...]
        e_pre = sums[n_lvl * C:(n_lvl + 1) * C]
        e_suf = sums[(n_lvl + 1) * C:(n_lvl + 2) * C]
        cross = _dot((q * jnp.exp(e_pre)).astype(BF16), state.astype(BF16))
        kd = (k * jnp.exp(e_suf)).astype(BF16)
        upd = _dot(kd.T, vb)
        total = jnp.broadcast_to(jnp.exp(e_pre[C - 1:C, :]), (LANE, hk))
        dec = jnp.concatenate([total.T] * (hv // LANE), axis=1)
        state_ref[...] = dec * state + bmask_ref[...] * upd

        o = intra + cross
        y = jnp.concatenate(
            [_head_norm(o[:, h * dv:(h + 1) * dv]) for h in range(heads)], axis=1)
        o_ref[rows, :] = (y * _silu(gate)).astype(o_ref.dtype)
        return carry

    lax.fori_loop(0, n_chunks, chunk, 0)


def _gla(x, wg, bg, B, S, heads, dk, dv):
    tt = min(S, SEQ_TILE)
    nt = S // tt
    win = 2 * heads * dk + 2 * heads * dv + LANE
    consts = _gla_consts(heads, dk, dv)
    kern = functools.partial(_gla_kernel, heads=heads, dk=dk, dv=dv, n_chunks=tt // CHUNK)
    return pl.pallas_call(
        kern,
        grid=(B, nt),
        in_specs=[pl.BlockSpec((tt, win), _seq_map(nt)),
                  _const_spec(wg.shape), _const_spec(bg.shape)]
        + [_const_spec(c.shape) for c in consts],
        out_specs=pl.BlockSpec((tt, heads * dv), _seq_map(nt)),
        out_shape=jax.ShapeDtypeStruct((B * S, heads * dv), BF16),
        scratch_shapes=[pltpu.VMEM((heads * dk, heads * dv), F32)],
        compiler_params=_cparams(("parallel", "arbitrary")),
        name="gla",
    )(x, wg, bg, *consts)


def _s5_kernel(u_ref, a_ref, bblk_ref, cblk_ref, d_ref, wglu_ref, o_ref,
               stage_ref, utb_ref, xs_ref, ytb_ref, st_ref, *, tt):
    nb, w = u_ref.shape[0], u_ref.shape[2]
    nblk = w // LANE
    sw = a_ref.shape[1] // (2 * nblk)

    @pl.when(pl.program_id(0) == 0)
    def _():
        st_ref[...] = jnp.zeros_like(st_ref)

    for b in range(nb):
        for m in range(nblk):
            stage_ref[m, b * S5_PITCH:b * S5_PITCH + tt, :] = u_ref[b, :, m * LANE:(m + 1) * LANE]

    def gather(t, carry):
        rows = pl.ds(pl.multiple_of(t * nb, nb), nb)
        for m in range(nblk):
            utb_ref[rows, m * LANE:(m + 1) * LANE] = stage_ref[m, pl.ds(t, nb, stride=S5_PITCH), :]
        return carry

    lax.fori_loop(0, tt, gather, 0)

    u = utb_ref[...]
    ub = u.astype(BF16)
    for m in range(nblk):
        xs_ref[:, 2 * sw * m:2 * sw * (m + 1)] = _dot(ub[:, m * LANE:(m + 1) * LANE], bblk_ref[m])

    def step(t, carry):
        rows = pl.ds(pl.multiple_of(t * nb, nb), nb)
        new = []
        for m in range(nblk):
            xr, xi = carry[2 * m], carry[2 * m + 1]
            re = slice(2 * sw * m, 2 * sw * m + sw)
            im = slice(2 * sw * m + sw, 2 * sw * (m + 1))
            ar, ai = a_ref[:, re], a_ref[:, im]
            nr = ar * xr - ai * xi + xs_ref[rows, re]
            ni = ar * xi + ai * xr + xs_ref[rows, im]
            xs_ref[rows, re] = nr
            xs_ref[rows, im] = ni
            new += [nr, ni]
        return tuple(new)

    init = tuple(st_ref[:, sw * c:sw * (c + 1)] for c in range(2 * nblk))
    last = lax.fori_loop(0, tt, step, init, unroll=2)
    for c in range(2 * nblk):
        st_ref[:, sw * c:sw * (c + 1)] = last[c]

    ys = [_dot(xs_ref[:, 2 * sw * m:2 * sw * (m + 1)].astype(BF16), cblk_ref[m]) for m in range(nblk)]
    y = jnp.concatenate(ys, axis=1) + d_ref[...] * u
    y = jax.nn.gelu(y)
    y = y * jax.nn.sigmoid(_dot(y.astype(BF16), wglu_ref[...]))

    for m in range(nblk):
        ytb_ref[m] = y[:, m * LANE:(m + 1) * LANE]
    for b in range(nb):
        for m in range(nblk):
            o_ref[b, :, m * LANE:(m + 1) * LANE] = (
                ytb_ref[m, pl.ds(b, tt, stride=nb), :].astype(o_ref.dtype))


def _s5(u3, a_row, bblk, cblk, d_row, wglu, layer):
    B, S, W = u3.shape
    tt = min(S, S5_TILE)
    ns2 = a_row.shape[1]
    nblk = W // LANE
    a8 = jnp.broadcast_to(a_row, (B, ns2))
    return pl.pallas_call(
        functools.partial(_s5_kernel, tt=tt),
        grid=(S // tt,),
        in_specs=[pl.BlockSpec((B, tt, W), lambda t: (0, t, 0)),
                  _const_spec(a8.shape), _layer_spec(bblk, layer), _layer_spec(cblk, layer),
                  _const_spec(d_row.shape), _layer_spec(wglu, layer)],
        out_specs=pl.BlockSpec((B, tt, W), lambda t: (0, t, 0)),
        out_shape=jax.ShapeDtypeStruct((B, S, W), BF16),
        scratch_shapes=[pltpu.VMEM((nblk, B * S5_PITCH, LANE), F32),
                        pltpu.VMEM((tt * B, W), F32),
                        pltpu.VMEM((tt * B, ns2), F32),
                        pltpu.VMEM((nblk, tt * B, LANE), F32),
                        pltpu.VMEM((B, ns2), F32)],
        compiler_params=_cparams(("arbitrary",)),
        name="s5",
    )(u3, a8, bblk, cblk, d_row, wglu)


def _s5_params(a_re, a_im, log_dt, b_re, b_im, c_re, c_im):
    G, P = a_re.shape
    n_c = b_re.shape[-1]
    gpb = LANE // n_c
    nblk = G // gpb
    dt = jnp.exp(log_dt)[:, None]
    mag = jnp.exp(a_re * dt)
    abar_re = mag * jnp.cos(a_im * dt)
    abar_im = mag * jnp.sin(a_im * dt)
    den = a_re * a_re + a_im * a_im
    nr, ni = abar_re - 1.0, abar_im
    f_re = (nr * a_re + ni * a_im) / den
    f_im = (ni * a_re - nr * a_im) / den
    bb_re = f_re[..., None] * b_re - f_im[..., None] * b_im
    bb_im = f_re[..., None] * b_im + f_im[..., None] * b_re
    eye = jnp.eye(gpb, dtype=F32)

    def bd_in(m):
        m = m.reshape(nblk, gpb, P, n_c)
        return jnp.einsum('ngpc,gh->ngchp', m, eye).reshape(nblk, gpb * n_c, gpb * P)

    def bd_out(m):
        m = m.reshape(nblk, gpb, n_c, P)
        return jnp.einsum('ngcp,gh->ngphc', m, eye).reshape(nblk, gpb * P, gpb * n_c)

    a_row = jnp.concatenate([abar_re.reshape(nblk, 1, gpb * P), abar_im.reshape(nblk, 1, gpb * P)],
                            axis=2).reshape(1, 2 * G * P)
    bblk = jnp.concatenate([bd_in(bb_re), bd_in(bb_im)], axis=2).astype(BF16)
    cblk = jnp.concatenate([bd_out(c_re), -bd_out(c_im)], axis=1).astype(BF16)
    return a_row, bblk, cblk


def _pool_body(p_ref, w_ref, sc_ref, o_ref, ext_ref, *, tt):
    pad = POOL_PAD
    t_idx = pl.program_id(1)
    ext_ref[pad:pad + tt, :] = p_ref[...]
    pos1 = (lax.broadcasted_iota(jnp.int32, (tt, LANE), 0) + t_idx * tt + 1).astype(F32)
    outs = []
    for g, win in enumerate(POOL_WINDOWS):
        cols = slice(g * LANE, (g + 1) * LANE)
        acc = ext_ref[:, cols]
        k = 1
        while k < win:
            acc = acc + pltpu.roll(acc, k, 0)
            k *= 2
        acc = acc[pad:pad + tt]
        cur = ext_ref[pad:pad + tt, cols]
        mixed = acc / jnp.minimum(pos1, float(win)) - cur
        outs.append(_dot(mixed.astype(BF16), w_ref[g]))
    o_ref[...] = (jnp.concatenate(outs, axis=1) * sc_ref[...]).astype(o_ref.dtype)
    ext_ref[0:pad, :] = ext_ref[tt:tt + pad, :]


def _merge_kernel(hn_ref, *refs):
    ys = refs[0:N_BRANCH]
    wgates = refs[N_BRANCH:2 * N_BRANCH]
    wbrs = refs[2 * N_BRANCH:3 * N_BRANCH]
    o_ref = refs[3 * N_BRANCH]
    tm = o_ref.shape[0]
    for rb in range(tm // MERGE_SUB):
        rows = slice(rb * MERGE_SUB, (rb + 1) * MERGE_SUB)
        hn = hn_ref[rows, :]
        acc = None
        for n in range(N_BRANCH):
            logits = _dot(hn, wgates[n][...])
            term = jax.nn.sigmoid(logits) * _dot(ys[n][rows, :], wbrs[n][...])
            acc = term if acc is None else acc + term
        o_ref[rows, :] = acc.astype(o_ref.dtype)


def _merge(hn, ys, wgate, wbr, layer):
    R, D = hn.shape
    tm = min(R, MERGE_ROW_TILE)
    bw = ys[0].shape[1]
    tn = MERGE_TILE
    nc = D // tn
    gate_spec = lambda n: pl.BlockSpec((None, D, tn), lambda i, c: (layer, 0, n * nc + c))
    br_spec = lambda n: pl.BlockSpec((None, None, bw, tn), lambda i, c: (layer, n, 0, c))
    return pl.pallas_call(
        _merge_kernel,
        grid=(R // tm, nc),
        in_specs=[pl.BlockSpec((tm, D), lambda i, c: (i, 0))]
        + [pl.BlockSpec((tm, bw), lambda i, c: (i, 0)) for _ in ys]
        + [gate_spec(n) for n in range(N_BRANCH)] + [br_spec(n) for n in range(N_BRANCH)],
        out_specs=pl.BlockSpec((tm, tn), lambda i, c: (i, c)),
        out_shape=jax.ShapeDtypeStruct((R, D), BF16),
        compiler_params=_cparams(("parallel", "arbitrary")),
        name="branch_merge",
    )(hn, *ys, *([wgate] * N_BRANCH), *([wbr] * N_BRANCH))


def _outproj_kernel(h_ref, m_ref, w_ref, g_ref, o_ref, hn_ref):
    for rb in range(h_ref.shape[0] // OUT_SUB):
        rows = slice(rb * OUT_SUB, (rb + 1) * OUT_SUB)
        h = h_ref[rows, :] + _dot(m_ref[rows, :], w_ref[...])
        o_ref[rows, :] = h
        hn_ref[rows, :] = _rms(h, g_ref[...]).astype(BF16)


def _outproj(h, merged, w, g, layer):
    R, D = h.shape
    tm = min(R, ROW_TILE)
    row_spec = pl.BlockSpec((tm, D), lambda i: (i, 0))
    return pl.pallas_call(
        _outproj_kernel,
        grid=(R // tm,),
        in_specs=[row_spec, row_spec, _layer_spec(w, layer), _const_spec((1, D))],
        out_specs=[row_spec, row_spec],
        out_shape=[jax.ShapeDtypeStruct((R, D), F32), jax.ShapeDtypeStruct((R, D), BF16)],
        compiler_params=_cparams(("parallel",)),
        name="out_proj",
    )(h, merged, w, g.reshape(1, D))


def _ffn_up_kernel(hn_ref, wa_ref, wv_ref, cw_ref, cb_ref, o_ref, carry_ref, *, tn, tiles_per_seq):
    i = pl.program_id(0)
    j = pl.program_id(1)
    tm = hn_ref.shape[0]

    @pl.when(i % tiles_per_seq == 0)
    def _():
        carry_ref[j] = jnp.zeros((SUBLANE, 2 * tn), F32)

    row8 = lax.broadcasted_iota(jnp.int32, (SUBLANE, tn), 0)

    def delayed(u, prev8, k):
        r = pltpu.roll(u, k, 0)
        top = jnp.where(row8 < k, pltpu.roll(prev8, k, 0), r[0:SUBLANE])
        return jnp.concatenate([top, r[SUBLANE:]], axis=0)

    def conv(u, prev8, col0):
        cols = slice(col0, col0 + tn)
        out = cb_ref[:, cols] + u * cw_ref[CONV_W - 1:CONV_W, cols]
        for tap in range(CONV_W - 1):
            out = out + delayed(u, prev8, CONV_W - 1 - tap) * cw_ref[tap:tap + 1, cols]
        return out

    sub = min(FF_SUB, tm)

    def up(rb):
        hn = hn_ref[rb * sub:(rb + 1) * sub, :]
        return _dot(hn, wa_ref[...]), _dot(hn, wv_ref[...])

    n_sub = tm // sub
    prev = carry_ref[j]
    prev_a, prev_v = prev[:, 0:tn], prev[:, tn:2 * tn]
    nxt = up(0)
    for rb in range(n_sub):
        ua, uv = nxt
        if rb + 1 < n_sub:
            nxt = up(rb + 1)
        a = conv(ua, prev_a, 0)
        v = conv(uv, prev_v, tn)
        prev_a, prev_v = ua[sub - SUBLANE:], uv[sub - SUBLANE:]
        o_ref[rb * sub:(rb + 1) * sub, :] = (_silu(a) * v).astype(o_ref.dtype)
    carry_ref[j] = jnp.concatenate([prev_a, prev_v], axis=1)


def _ffn_up(hn, wa, wv, cw, cb, S, layer):
    R, D = hn.shape
    tm = min(S, FF_ROW_TILE)
    tn = FF_TILE
    ffp = wa.shape[2]
    nj = ffp // tn
    assert tm % min(FF_SUB, tm) == 0 and S % tm == 0 and ffp % tn == 0
    return pl.pallas_call(
        functools.partial(_ffn_up_kernel, tn=tn, tiles_per_seq=S // tm),
        grid=(R // tm, nj),
        in_specs=[pl.BlockSpec((tm, D), lambda i, j: (i, 0)),
                  pl.BlockSpec((None, D, tn), lambda i, j: (layer, 0, j)),
                  pl.BlockSpec((None, D, tn), lambda i, j: (layer, 0, j)),
                  pl.BlockSpec((None, None, CONV_W, 2 * tn), lambda i, j: (layer, j, 0, 0)),
                  pl.BlockSpec((None, None, 1, 2 * tn), lambda i, j: (layer, j, 0, 0))],
        out_specs=pl.BlockSpec((tm, tn), lambda i, j: (i, j)),
        out_shape=jax.ShapeDtypeStruct((R, ffp), BF16),
        scratch_shapes=[pltpu.VMEM((nj, SUBLANE, 2 * tn), F32)],
        compiler_params=_cparams(("arbitrary", "arbitrary")),
        name="ffn_up",
    )(hn, wa, wv, cw, cb)


def _ffn_down_kernel(h_ref, act_ref, wd_ref, g_ref, o_ref, *, final):
    for rb in range(h_ref.shape[0] // OUT_SUB):
        rows = slice(rb * OUT_SUB, (rb + 1) * OUT_SUB)
        h = h_ref[rows, :] + _dot(act_ref[rows, :], wd_ref[...])
        o_ref[rows, :] = _rms(h, g_ref[...]) if final else h


def _ffn_down(h, act, wd, g, final, layer):
    R, D = h.shape
    tm = min(R, ROW_TILE)
    ff = wd.shape[1]
    row_spec = pl.BlockSpec((tm, D), lambda i: (i, 0))
    return pl.pallas_call(
        functools.partial(_ffn_down_kernel, final=final),
        grid=(R // tm,),
        in_specs=[row_spec, pl.BlockSpec((tm, ff), lambda i: (i, 0)),
                  _layer_spec(wd, layer), _const_spec((1, D))],
        out_specs=row_spec,
        out_shape=jax.ShapeDtypeStruct((R, D), F32),
        compiler_params=_cparams(("parallel",)),
        name="ffn_down",
    )(h, act, wd, g.reshape(1, D))


def _pad_cols(w, n):
    return jnp.pad(w, ((0, 0), (0, n - w.shape[1])))


def _split_w_in(w_in, D):
    bw = D // 4
    hk = bw // 2
    sizes = [hk, hk, bw, bw, bw, hk, hk, bw, bw, GLA_RANK, bw, N_BRANCH * D]
    offs = np.concatenate([[0], np.cumsum(sizes)])
    seg = lambda a, b: w_in[:, offs[a]:offs[b]]
    w_ret = seg(0, 4)
    w_s5 = seg(4, 5)
    w_gla = jnp.concatenate([seg(5, 9), _pad_cols(seg(9, 10), LANE)], axis=1)
    w_pool = seg(10, 11)
    return [w.astype(BF16) for w in (w_ret, w_s5, w_gla, w_pool)], seg(11, 12).astype(BF16)


def _ffn_weights(w_up, conv_w, conv_b, w_down, tn):
    D, ff2 = w_up.shape
    ff = ff2 // 2
    ffp = -(-ff // tn) * tn
    nj = ffp // tn
    wa = _pad_cols(w_up[:, :ff], ffp).astype(BF16)
    wv = _pad_cols(w_up[:, ff:], ffp).astype(BF16)
    tiles = lambda m: _pad_cols(m, ffp).reshape(m.shape[0], nj, tn)
    cw = jnp.concatenate([tiles(conv_w[:, :ff]), tiles(conv_w[:, ff:])], axis=2).transpose(1, 0, 2)
    cbr = conv_b.reshape(1, ff2)
    cb = jnp.concatenate([tiles(cbr[:, :ff]), tiles(cbr[:, ff:])], axis=2).transpose(1, 0, 2)
    return wa, wv, cw, cb, w_down.astype(BF16)


def kernel(x, positions, norm_mix_g, w_in, s5_a_re, s5_a_im, s5_log_dt, s5_b_re, s5_b_im, s5_c_re, s5_c_im, s5_d, s5_w_glu, gla_w_gate, gla_b_gate, pool_w, pool_scale, w_branch, w_out, norm_ffn_g, w_up, conv_w, conv_b, w_down, final_g):
    B, S, D = x.shape
    depth = w_in.shape[0]
    bw = D // 4
    ret_dk, ret_dv = bw // (2 * RET_HEADS), bw // RET_HEADS
    gla_dk, gla_dv = bw // (2 * GLA_HEADS), bw // GLA_HEADS
    assert S % CHUNK == 0 and S % min(S, ROW_TILE) == 0 and B == SUBLANE
    assert ret_dk // 2 * 4 == LANE and bw == S5_BLOCK * LANE

    inv = ROPE_BASE ** (-jnp.arange(0, ret_dk, 2, dtype=F32) / ret_dk)
    ang = positions.astype(F32)[..., None] * inv
    ang = jnp.tile(ang, (1, 1, LANE // (ret_dk // 2))).reshape(B * S, LANE)
    rope = _rope_tables(ang, ret_dk // 2)

    ws, w_gate = jax.vmap(lambda w: _split_w_in(w, D))(w_in)
    a_row, bblk, cblk = jax.vmap(_s5_params)(s5_a_re, s5_a_im, s5_log_dt, s5_b_re, s5_b_im,
                                             s5_c_re, s5_c_im)
    wglu = s5_w_glu.astype(BF16)
    wg = jnp.pad(gla_w_gate, ((0, 0), (0, LANE - GLA_RANK), (0, 0))).astype(BF16)
    pw = pool_w.astype(BF16)
    wbr = w_branch.astype(BF16)
    wo = w_out.astype(BF16)
    wa, wv, cw, cb, wd = jax.vmap(lambda a, b, c, d: _ffn_weights(a, b, c, d, FF_TILE))(
        w_up, conv_w, conv_b, w_down)

    h = x.reshape(B * S, D)
    for l in range(depth):
        hn, p_ret, p_s5, p_gla, p_pool = _proj(h, norm_mix_g[l], ws, l)
        ya, yd = _retention_pool(p_ret, rope, p_pool, pw[l], pool_scale[l].reshape(1, bw),
                                 B, S, RET_HEADS, ret_dk, ret_dv)
        yb = _s5(p_s5.reshape(B, S, bw), a_row[l], bblk, cblk, s5_d[l].reshape(1, bw),
                 wglu, l).reshape(B * S, bw)
        yc = _gla(p_gla, wg[l], gla_b_gate[l].reshape(1, -1), B, S, GLA_HEADS, gla_dk, gla_dv)
        merged = _merge(hn, [ya, yb, yc, yd], w_gate, wbr, l)
        h, hn = _outproj(h, merged, wo, norm_ffn_g[l], l)
        act = _ffn_up(hn, wa, wv, cw, cb, S, l)
        h = _ffn_down(h, act, wd, final_g, final=(l == depth - 1), layer=l)
    return h.reshape(B, S, D)
```

```python
import functools
import math

import numpy as np
import jax
import jax.numpy as jnp
from jax import lax
from jax.experimental import pallas as pl
from jax.experimental.pallas import tpu as pltpu

F32 = jnp.float32
BF16 = jnp.bfloat16
EPS = 1e-6

N_BRANCH = 4
RET_HEADS = 4
GLA_HEADS = 4
GLA_RANK = 16
GLA_GATE_TEMP = 16.0
S5_GROUP = 16
S5_STATE = 64
POOL_WINDOWS = (2, 4, 8, 16)
POOL_PAD = 16
ROPE_BASE = 10000.0
CONV_W = 3

LANE = 128
SUBLANE = 8
CHUNK = 128
VMEM_LIMIT = 56 * 1024 * 1024
ROW_TILE = 512
MERGE_ROW_TILE = 1024
MERGE_TILE = 512
MERGE_SUB = 256
OUT_SUB = 256
FF_ROW_TILE = 2048
FF_TILE = 512
FF_SUB = 1024
SEQ_TILE = 512
S5_TILE = 128
S5_PITCH = S5_TILE + 4
S5_BLOCK = 4


def _cparams(sem):
    return pltpu.CompilerParams(dimension_semantics=sem, vmem_limit_bytes=VMEM_LIMIT)


def _const_spec(shape):
    nd = len(shape)
    return pl.BlockSpec(shape, lambda *_: (0,) * nd, pipeline_mode=pl.Buffered(1))


def _layer_spec(arr, layer):
    nd = arr.ndim - 1
    return pl.BlockSpec((None,) + arr.shape[1:], lambda *_: (layer,) + (0,) * nd,
                        pipeline_mode=pl.Buffered(1))


def _dot(a, b):
    return jnp.dot(a, b, preferred_element_type=F32)


def _dot_nt(a, b):
    return lax.dot_general(a, b, (((1,), (1,)), ((), ())), preferred_element_type=F32)


def _rms(x, g):
    r = lax.rsqrt(jnp.mean(x * x, axis=-1, keepdims=True) + EPS)
    return x * r * g


def _head_norm(o):
    mu = jnp.mean(o, axis=-1, keepdims=True)
    d = o - mu
    var = jnp.mean(d * d, axis=-1, keepdims=True)
    return d * lax.rsqrt(var + EPS)


def _silu(x):
    return x * jax.nn.sigmoid(x)


def _key_blocks(k, heads, dk):
    return _key_blocks_t(k.T, heads, dk)


def _key_blocks_t(kt, heads, dk):
    kt = kt.astype(BF16)
    zero = jnp.zeros((dk, kt.shape[1]), BF16)
    cols = [jnp.concatenate([kt[h * dk:(h + 1) * dk] if r == h else zero for r in range(heads)], axis=0)
            for h in range(heads)]
    return jnp.concatenate(cols, axis=1)


def _seq_map(nt):
    return lambda b, t: (b * nt + t, 0)


def _proj_kernel(h_ref, g_ref, w_ret, w_s5, w_gla, w_pool, o_hn, o_ret, o_s5, o_gla, o_pool):
    hn = _rms(h_ref[...], g_ref[...]).astype(BF16)
    o_hn[...] = hn
    o_ret[...] = _dot(hn, w_ret[...])
    o_s5[...] = _dot(hn, w_s5[...])
    o_gla[...] = _dot(hn, w_gla[...])
    o_pool[...] = _dot(hn, w_pool[...])


def _proj(h, g, ws, layer):
    R, D = h.shape
    tm = min(R, ROW_TILE)
    widths = [w.shape[2] for w in ws]
    return pl.pallas_call(
        _proj_kernel,
        grid=(R // tm,),
        in_specs=[pl.BlockSpec((tm, D), lambda i: (i, 0)), _const_spec((1, D))]
        + [_layer_spec(w, layer) for w in ws],
        out_specs=[pl.BlockSpec((tm, n), lambda i: (i, 0)) for n in [D] + widths],
        out_shape=[jax.ShapeDtypeStruct((R, D), BF16)]
        + [jax.ShapeDtypeStruct((R, n), F32) for n in widths],
        compiler_params=_cparams(("parallel",)),
        name="mixer_proj",
    )(h, g.reshape(1, D), *ws)


def _ret_consts(heads, dk, dv):
    log_g = np.log(1.0 - 2.0 ** (-5.0 - np.arange(heads, dtype=np.float64)))
    idx = np.arange(CHUNK, dtype=np.float64)
    rel = idx[:, None] - idx[None, :]
    dmask = np.where(rel >= 0, np.exp(np.maximum(rel, 0.0)[None] * log_g[:, None, None]), 0.0)
    dmask = np.concatenate(list(dmask), axis=1)
    lane_g = np.repeat(log_g, dk)[None, :]
    qdec = np.exp((idx[:, None] + 1.0) * lane_g)
    kdec = np.exp((CHUNK - 1.0 - idx)[:, None] * lane_g)
    sdec = np.exp(CHUNK * np.repeat(log_g, dk))[:, None] * np.ones((1, heads * dv))
    bmask = np.kron(np.eye(heads), np.ones((dk, dv)))
    f = lambda a: jnp.asarray(a, F32)
    return f(dmask), f(qdec), f(kdec), f(sdec), f(bmask)


def _rope_kernel(ang_ref, cos_ref, ssin_ref, *, half):
    ang = ang_ref[...]
    lane = lax.broadcasted_iota(jnp.int32, ang.shape, 1)
    cos_ref[...] = jnp.cos(ang)
    s = jnp.sin(ang)
    ssin_ref[...] = jnp.where((lane % (2 * half)) < half, -s, s)


def _rope_tables(ang, half):
    R, W = ang.shape
    tm = min(R, 2048)
    spec = pl.BlockSpec((tm, W), lambda i: (i, 0))
    return pl.pallas_call(
        functools.partial(_rope_kernel, half=half),
        grid=(R // tm,),
        in_specs=[spec],
        out_specs=[spec, spec],
        out_shape=[jax.ShapeDtypeStruct((R, W), F32)] * 2,
        compiler_params=_cparams(("parallel",)),
        name="rope_tables",
    )(ang)


def _ret_body(x_ref, cos_ref, ssin_ref, dmask_ref, qdec_ref, kdec_ref, sdec_ref, bmask_ref,
              o_ref, state_ref, *, heads, dk, dv, n_chunks):
    hk = heads * dk
    hv = heads * dv
    lane = lax.broadcasted_iota(jnp.int32, (CHUNK, hk), 1)
    first_half = (lane % dk) < (dk // 2)

    def rotary(x, cos2, ssin):
        swapped = jnp.where(first_half, pltpu.roll(x, hk - dk // 2, 1), pltpu.roll(x, dk // 2, 1))
        return x * cos2 + swapped * ssin

    def chunk(ci, carry):
        r0 = pl.multiple_of(ci * CHUNK, CHUNK)
        rows = pl.ds(r0, CHUNK)
        reps = hk // LANE
        cos2 = jnp.concatenate([cos_ref[rows, :]] * reps, axis=1)
        ssin = jnp.concatenate([ssin_ref[rows, :]] * reps, axis=1)
        q = rotary(x_ref[rows, 0:hk], cos2, ssin)
        k = rotary(x_ref[rows, hk:2 * hk], cos2, ssin) * (dk ** -0.5)
        v = x_ref[rows, 2 * hk:2 * hk + hv]
        gate = x_ref[rows, 2 * hk + hv:2 * hk + 2 * hv]
        vb = v.astype(BF16)
        state = state_ref[...]
        cross = _dot((q * qdec_ref[...]).astype(BF16), state.astype(BF16))
        sb = (_dot(q.astype(BF16), _key_blocks(k, heads, dk)) * dmask_ref[...]).astype(BF16)
        outs = [_dot(sb[:, h * CHUNK:(h + 1) * CHUNK], vb[:, h * dv:(h + 1) * dv]) for h in range(heads)]
        o = jnp.concatenate(outs, axis=1) + cross
        kd = (k * kdec_ref[...]).astype(BF16)
        upd = _dot(kd.T, vb)
        state_ref[...] = sdec_ref[...] * state + bmask_ref[...] * upd
        y = jnp.concatenate(
            [_head_norm(o[:, h * dv:(h + 1) * dv]) for h in range(heads)], axis=1)
        o_ref[rows, :] = (y * _silu(gate)).astype(o_ref.dtype)
        return carry

    lax.fori_loop(0, n_chunks, chunk, 0, unroll=True)


def _ret_pool_kernel(x_ref, cos_ref, ssin_ref, dmask_ref, qdec_ref, kdec_ref, sdec_ref, bmask_ref,
                     p_ref, pw_ref, psc_ref, o_ref, po_ref, state_ref, ext_ref,
                     *, heads, dk, dv, n_chunks, tt):
    @pl.when(pl.program_id(1) == 0)
    def _():
        state_ref[...] = jnp.zeros_like(state_ref)
        ext_ref[0:POOL_PAD, :] = jnp.zeros((POOL_PAD, ext_ref.shape[1]), F32)

    _pool_body(p_ref, pw_ref, psc_ref, po_ref, ext_ref, tt=tt)
    _ret_body(x_ref, cos_ref, ssin_ref, dmask_ref, qdec_ref, kdec_ref, sdec_ref, bmask_ref,
              o_ref, state_ref, heads=heads, dk=dk, dv=dv, n_chunks=n_chunks)


def _retention_pool(x, rope, p, pw, pscale, B, S, heads, dk, dv):
    tt = min(S, SEQ_TILE)
    nt = S // tt
    win = 2 * heads * dk + 2 * heads * dv
    pwid = pw.shape[0] * LANE
    consts = _ret_consts(heads, dk, dv)
    kern = functools.partial(_ret_pool_kernel, heads=heads, dk=dk, dv=dv, n_chunks=tt // CHUNK, tt=tt)
    seq = lambda width: pl.BlockSpec((tt, width), _seq_map(nt))
    return pl.pallas_call(
        kern,
        grid=(B, nt),
        in_specs=[seq(win), seq(LANE), seq(LANE)] + [_const_spec(c.shape) for c in consts]
        + [seq(pwid), _const_spec(pw.shape), _const_spec(pscale.shape)],
        out_specs=[seq(heads * dv), seq(pwid)],
        out_shape=[jax.ShapeDtypeStruct((B * S, heads * dv), BF16),
                   jax.ShapeDtypeStruct((B * S, pwid), BF16)],
        scratch_shapes=[pltpu.VMEM((heads * dk, heads * dv), F32),
                        pltpu.VMEM((tt + POOL_PAD, pwid), F32)],
        compiler_params=_cparams(("parallel", "arbitrary")),
        name="retention_pool",
    )(x, *rope, *consts, p, pw, pscale)


def _gla_consts(heads, dk, dv):
    C = CHUNK
    n_lvl = int(math.log2(C))
    t = np.arange(C)
    sums = []
    masks = []
    for l in range(n_lvl):
        bit = (t >> l) & 1
        start = (t >> l) << l
        r = t[None, :]
        upper = (bit[:, None] == 1) & (r >= start[:, None]) & (r <= t[:, None])
        lower = (bit[:, None] == 0) & (r > t[:, None]) & (r < (start + (1 << l))[:, None])
        sums.append((upper | lower).astype(np.float64))
        m = ((bit[:, None] == 1) & (bit[None, :] == 0)
             & ((t[:, None] >> (l + 1)) == (t[None, :] >> (l + 1))))
        masks.append(np.tile(m.astype(np.float64), (1, heads)))
    masks.append(np.tile(np.eye(C), (1, heads)))
    sums.append((t[None, :] <= t[:, None]).astype(np.float64))
    sums.append((t[None, :] > t[:, None]).astype(np.float64))
    g = np.concatenate(sums, axis=0)
    gcat = np.concatenate([g, g], axis=1)
    bmask = np.kron(np.eye(heads), np.ones((dk, dv)))
    return jnp.asarray(gcat, BF16), jnp.asarray(np.stack(masks), F32), jnp.asarray(bmask, F32)


def _log_sigmoid(x):
    return jnp.minimum(x, 0.0) - jnp.log(1.0 + jnp.exp(-jnp.abs(x)))


def _gla_kernel(x_ref, wg_ref, bg_ref, gcat_ref, masks_ref, bmask_ref, o_ref, state_ref,
                *, heads, dk, dv, n_chunks):
    hk = heads * dk
    hv = heads * dv
    C = CHUNK
    n_lvl = int(math.log2(C))

    @pl.when(pl.program_id(1) == 0)
    def _():
        state_ref[...] = jnp.zeros_like(state_ref)

    def chunk(ci, carry):
        r0 = pl.multiple_of(ci * C, C)
        rows = pl.ds(r0, C)
        q = x_ref[rows, 0:hk] * (dk ** -0.5)
        k = x_ref[rows, hk:2 * hk]
        v = x_ref[rows, 2 * hk:2 * hk + hv]
        gate = x_ref[rows, 2 * hk + hv:2 * hk + 2 * hv]
        code = x_ref[rows, 2 * hk + 2 * hv:2 * hk + 2 * hv + LANE]
        vb = v.astype(BF16)
        log_a = _log_sigmoid(_dot(code.astype(BF16), wg_ref[...]) + bg_ref[...]) * (1.0 / GLA_GATE_TEMP)
        hi = log_a.astype(BF16)
        lo = (log_a - hi.astype(F32)).astype(BF16)
        sums = _dot(gcat_ref[...], jnp.concatenate([hi, lo], axis=0))

        log_at = log_a.T
        hit = log_at.astype(BF16)
        lot = (log_at - hit.astype(F32)).astype(BF16)
        sums_t = _dot_nt(jnp.concatenate([hit, lot], axis=1), gcat_ref[0:n_lvl * C, :])
        kt = k.T
        scores = masks_ref[n_lvl] * _dot(q.astype(BF16), _key_blocks_t(kt, heads, dk))
        for l in range(n_lvl):
            z = jnp.exp(sums[l * C:(l + 1) * C])
            zt = jnp.exp(sums_t[:, l * C:(l + 1) * C])
            scores = scores + masks_ref[l] * _dot((q * z).astype(BF16), _key_blocks_t(kt * zt, heads, dk))
        sb = scores.astype(BF16)
        intra = jnp.concatenate(
            [_dot(sb[:, h * C:(h + 1) * C], vb[:, h * dv:(h + 1) * dv]) for h in range(heads)], axis=1)

        state = state_ref[...]
        e_pre = sums[n_lvl * C:(n_lvl + 1) * C]
        e_suf = sums[(n_lvl + 1) * C:(n_lvl + 2) * C]
        cross = _dot((q * jnp.exp(e_pre)).astype(BF16), state.astype(BF16))
        kd = (k * jnp.exp(e_suf)).astype(BF16)
        upd = _dot(kd.T, vb)
        total = jnp.broadcast_to(jnp.exp(e_pre[C - 1:C, :]), (LANE, hk))
        dec = jnp.concatenate([total.T] * (hv // LANE), axis=1)
        state_ref[...] = dec * state + bmask_ref[...] * upd

        o = intra + cross
        y = jnp.concatenate(
            [_head_norm(o[:, h * dv:(h + 1) * dv]) for h in range(heads)], axis=1)
        o_ref[rows, :] = (y * _silu(gate)).astype(o_ref.dtype)
        return carry

    lax.fori_loop(0, n_chunks, chunk, 0)


def _gla(x, wg, bg, B, S, heads, dk, dv):
    tt = min(S, SEQ_TILE)
    nt = S // tt
    win = 2 * heads * dk + 2 * heads * dv + LANE
    consts = _gla_consts(heads, dk, dv)
    kern = functools.partial(_gla_kernel, heads=heads, dk=dk, dv=dv, n_chunks=tt // CHUNK)
    return pl.pallas_call(
        kern,
        grid=(B, nt),
        in_specs=[pl.BlockSpec((tt, win), _seq_map(nt)),
                  _const_spec(wg.shape), _const_spec(bg.shape)]
        + [_const_spec(c.shape) for c in consts],
        out_specs=pl.BlockSpec((tt, heads * dv), _seq_map(nt)),
        out_shape=jax.ShapeDtypeStruct((B * S, heads * dv), BF16),
        scratch_shapes=[pltpu.VMEM((heads * dk, heads * dv), F32)],
        compiler_params=_cparams(("parallel", "arbitrary")),
        name="gla",
    )(x, wg, bg, *consts)


def _s5_kernel(u_ref, a_ref, bblk_ref, cblk_ref, d_ref, wglu_ref, o_ref,
               stage_ref, utb_ref, xs_ref, ytb_ref, st_ref, *, tt):
    nb, w = u_ref.shape[0], u_ref.shape[2]
    nblk = w // LANE
    sw = a_ref.shape[1] // (2 * nblk)

    @pl.when(pl.program_id(0) == 0)
    def _():
        st_ref[...] = jnp.zeros_like(st_ref)

    for b in range(nb):
        for m in range(nblk):
            stage_ref[m, b * S5_PITCH:b * S5_PITCH + tt, :] = u_ref[b, :, m * LANE:(m + 1) * LANE]

    def gather(t, carry):
        rows = pl.ds(pl.multiple_of(t * nb, nb), nb)
        for m in range(nblk):
            utb_ref[rows, m * LANE:(m + 1) * LANE] = stage_ref[m, pl.ds(t, nb, stride=S5_PITCH), :]
        return carry

    lax.fori_loop(0, tt, gather, 0)

    u = utb_ref[...]
    ub = u.astype(BF16)
    for m in range(nblk):
        xs_ref[:, 2 * sw * m:2 * sw * (m + 1)] = _dot(ub[:, m * LANE:(m + 1) * LANE], bblk_ref[m])

    def step(t, carry):
        rows = pl.ds(pl.multiple_of(t * nb, nb), nb)
        new = []
        for m in range(nblk):
            xr, xi = carry[2 * m], carry[2 * m + 1]
            re = slice(2 * sw * m, 2 * sw * m + sw)
            im = slice(2 * sw * m + sw, 2 * sw * (m + 1))
            ar, ai = a_ref[:, re], a_ref[:, im]
            nr = ar * xr - ai * xi + xs_ref[rows, re]
            ni = ar * xi + ai * xr + xs_ref[rows, im]
            xs_ref[rows, re] = nr
            xs_ref[rows, im] = ni
            new += [nr, ni]
        return tuple(new)

    init = tuple(st_ref[:, sw * c:sw * (c + 1)] for c in range(2 * nblk))
    last = lax.fori_loop(0, tt, step, init, unroll=2)
    for c in range(2 * nblk):
        st_ref[:, sw * c:sw * (c + 1)] = last[c]

    ys = [_dot(xs_ref[:, 2 * sw * m:2 * sw * (m + 1)].astype(BF16), cblk_ref[m]) for m in range(nblk)]
    y = jnp.concatenate(ys, axis=1) + d_ref[...] * u
    y = jax.nn.gelu(y)
    y = y * jax.nn.sigmoid(_dot(y.astype(BF16), wglu_ref[...]))

    for m in range(nblk):
        ytb_ref[m] = y[:, m * LANE:(m + 1) * LANE]
    for b in range(nb):
        for m in range(nblk):
            o_ref[b, :, m * LANE:(m + 1) * LANE] = (
                ytb_ref[m, pl.ds(b, tt, stride=nb), :].astype(o_ref.dtype))


def _s5(u3, a_row, bblk, cblk, d_row, wglu, layer):
    B, S, W = u3.shape
    tt = min(S, S5_TILE)
    ns2 = a_row.shape[1]
    nblk = W // LANE
    a8 = jnp.broadcast_to(a_row, (B, ns2))
    return pl.pallas_call(
        functools.partial(_s5_kernel, tt=tt),
        grid=(S // tt,),
        in_specs=[pl.BlockSpec((B, tt, W), lambda t: (0, t, 0)),
                  _const_spec(a8.shape), _layer_spec(bblk, layer), _layer_spec(cblk, layer),
                  _const_spec(d_row.shape), _layer_spec(wglu, layer)],
        out_specs=pl.BlockSpec((B, tt, W), lambda t: (0, t, 0)),
        out_shape=jax.ShapeDtypeStruct((B, S, W), BF16),
        scratch_shapes=[pltpu.VMEM((nblk, B * S5_PITCH, LANE), F32),
                        pltpu.VMEM((tt * B, W), F32),
                        pltpu.VMEM((tt * B, ns2), F32),
                        pltpu.VMEM((nblk, tt * B, LANE), F32),
                        pltpu.VMEM((B, ns2), F32)],
        compiler_params=_cparams(("arbitrary",)),
        name="s5",
    )(u3, a8, bblk, cblk, d_row, wglu)


def _s5_params(a_re, a_im, log_dt, b_re, b_im, c_re, c_im):
    G, P = a_re.shape
    n_c = b_re.shape[-1]
    gpb = LANE // n_c
    nblk = G // gpb
    dt = jnp.exp(log_dt)[:, None]
    mag = jnp.exp(a_re * dt)
    abar_re = mag * jnp.cos(a_im * dt)
    abar_im = mag * jnp.sin(a_im * dt)
    den = a_re * a_re + a_im * a_im
    nr, ni = abar_re - 1.0, abar_im
    f_re = (nr * a_re + ni * a_im) / den
    f_im = (ni * a_re - nr * a_im) / den
    bb_re = f_re[..., None] * b_re - f_im[..., None] * b_im
    bb_im = f_re[..., None] * b_im + f_im[..., None] * b_re
    eye = jnp.eye(gpb, dtype=F32)

    def bd_in(m):
        m = m.reshape(nblk, gpb, P, n_c)
        return jnp.einsum('ngpc,gh->ngchp', m, eye).reshape(nblk, gpb * n_c, gpb * P)

    def bd_out(m):
        m = m.reshape(nblk, gpb, n_c, P)
        return jnp.einsum('ngcp,gh->ngphc', m, eye).reshape(nblk, gpb * P, gpb * n_c)

    a_row = jnp.concatenate([abar_re.reshape(nblk, 1, gpb * P), abar_im.reshape(nblk, 1, gpb * P)],
                            axis=2).reshape(1, 2 * G * P)
    bblk = jnp.concatenate([bd_in(bb_re), bd_in(bb_im)], axis=2).astype(BF16)
    cblk = jnp.concatenate([bd_out(c_re), -bd_out(c_im)], axis=1).astype(BF16)
    return a_row, bblk, cblk


def _pool_body(p_ref, w_ref, sc_ref, o_ref, ext_ref, *, tt):
    pad = POOL_PAD
    t_idx = pl.program_id(1)
    ext_ref[pad:pad + tt, :] = p_ref[...]
    pos1 = (lax.broadcasted_iota(jnp.int32, (tt, LANE), 0) + t_idx * tt + 1).astype(F32)
    outs = []
    for g, win in enumerate(POOL_WINDOWS):
        cols = slice(g * LANE, (g + 1) * LANE)
        acc = ext_ref[:, cols]
        k = 1
        while k < win:
            acc = acc + pltpu.roll(acc, k, 0)
            k *= 2
        acc = acc[pad:pad + tt]
        cur = ext_ref[pad:pad + tt, cols]
        mixed = acc / jnp.minimum(pos1, float(win)) - cur
        outs.append(_dot(mixed.astype(BF16), w_ref[g]))
    o_ref[...] = (jnp.concatenate(outs, axis=1) * sc_ref[...]).astype(o_ref.dtype)
    ext_ref[0:pad, :] = ext_ref[tt:tt + pad, :]


def _merge_kernel(hn_ref, *refs):
    ys = refs[0:N_BRANCH]
    wgates = refs[N_BRANCH:2 * N_BRANCH]
    wbrs = refs[2 * N_BRANCH:3 * N_BRANCH]
    o_ref = refs[3 * N_BRANCH]
    tm = o_ref.shape[0]
    for rb in range(tm // MERGE_SUB):
        rows = slice(rb * MERGE_SUB, (rb + 1) * MERGE_SUB)
        hn = hn_ref[rows, :]
        acc = None
        for n in range(N_BRANCH):
            logits = _dot(hn, wgates[n][...])
            term = jax.nn.sigmoid(logits) * _dot(ys[n][rows, :], wbrs[n][...])
            acc = term if acc is None else acc + term
        o_ref[rows, :] = acc.astype(o_ref.dtype)


def _merge(hn, ys, wgate, wbr, layer):
    R, D = hn.shape
    tm = min(R, MERGE_ROW_TILE)
    bw = ys[0].shape[1]
    tn = MERGE_TILE
    nc = D // tn
    gate_spec = lambda n: pl.BlockSpec((None, D, tn), lambda i, c: (layer, 0, n * nc + c))
    br_spec = lambda n: pl.BlockSpec((None, None, bw, tn), lambda i, c: (layer, n, 0, c))
    return pl.pallas_call(
        _merge_kernel,
        grid=(R // tm, nc),
        in_specs=[pl.BlockSpec((tm, D), lambda i, c: (i, 0))]
        + [pl.BlockSpec((tm, bw), lambda i, c: (i, 0)) for _ in ys]
        + [gate_spec(n) for n in range(N_BRANCH)] + [br_spec(n) for n in range(N_BRANCH)],
        out_specs=pl.BlockSpec((tm, tn), lambda i, c: (i, c)),
        out_shape=jax.ShapeDtypeStruct((R, D), BF16),
        compiler_params=_cparams(("parallel", "arbitrary")),
        name="branch_merge",
    )(hn, *ys, *([wgate] * N_BRANCH), *([wbr] * N_BRANCH))


def _outproj_kernel(h_ref, m_ref, w_ref, g_ref, o_ref, hn_ref):
    for rb in range(h_ref.shape[0] // OUT_SUB):
        rows = slice(rb * OUT_SUB, (rb + 1) * OUT_SUB)
        h = h_ref[rows, :] + _dot(m_ref[rows, :], w_ref[...])
        o_ref[rows, :] = h
        hn_ref[rows, :] = _rms(h, g_ref[...]).astype(BF16)


def _outproj(h, merged, w, g, layer):
    R, D = h.shape
    tm = min(R, ROW_TILE)
    row_spec = pl.BlockSpec((tm, D), lambda i: (i, 0))
    return pl.pallas_call(
        _outproj_kernel,
        grid=(R // tm,),
        in_specs=[row_spec, row_spec, _layer_spec(w, layer), _const_spec((1, D))],
        out_specs=[row_spec, row_spec],
        out_shape=[jax.ShapeDtypeStruct((R, D), F32), jax.ShapeDtypeStruct((R, D), BF16)],
        compiler_params=_cparams(("parallel",)),
        name="out_proj",
    )(h, merged, w, g.reshape(1, D))


def _ffn_up_kernel(hn_ref, wa_ref, wv_ref, cw_ref, cb_ref, o_ref, carry_ref, *, tn, tiles_per_seq):
    i = pl.program_id(0)
    j = pl.program_id(1)
    tm = hn_ref.shape[0]

    @pl.when(i % tiles_per_seq == 0)
    def _():
        carry_ref[j] = jnp.zeros((SUBLANE, 2 * tn), F32)

    row8 = lax.broadcasted_iota(jnp.int32, (SUBLANE, tn), 0)

    def delayed(u, prev8, k):
        r = pltpu.roll(u, k, 0)
        top = jnp.where(row8 < k, pltpu.roll(prev8, k, 0), r[0:SUBLANE])
        return jnp.concatenate([top, r[SUBLANE:]], axis=0)

    def conv(u, prev8, col0):
        cols = slice(col0, col0 + tn)
        out = cb_ref[:, cols] + u * cw_ref[CONV_W - 1:CONV_W, cols]
        for tap in range(CONV_W - 1):
            out = out + delayed(u, prev8, CONV_W - 1 - tap) * cw_ref[tap:tap + 1, cols]
        return out

    sub = min(FF_SUB, tm)

    def up(rb):
        hn = hn_ref[rb * sub:(rb + 1) * sub, :]
        return _dot(hn, wa_ref[...]), _dot(hn, wv_ref[...])

    n_sub = tm // sub
    prev = carry_ref[j]
    prev_a, prev_v = prev[:, 0:tn], prev[:, tn:2 * tn]
    nxt = up(0)
    for rb in range(n_sub):
        ua, uv = nxt
        if rb + 1 < n_sub:
            nxt = up(rb + 1)
        a = conv(ua, prev_a, 0)
        v = conv(uv, prev_v, tn)
        prev_a, prev_v = ua[sub - SUBLANE:], uv[sub - SUBLANE:]
        o_ref[rb * sub:(rb + 1) * sub, :] = (_silu(a) * v).astype(o_ref.dtype)
    carry_ref[j] = jnp.concatenate([prev_a, prev_v], axis=1)


def _ffn_up(hn, wa, wv, cw, cb, S, layer):
    R, D = hn.shape
    tm = min(S, FF_ROW_TILE)
    tn = FF_TILE
    ffp = wa.shape[2]
    nj = ffp // tn
    assert tm % min(FF_SUB, tm) == 0 and S % tm == 0 and ffp % tn == 0
    return pl.pallas_call(
        functools.partial(_ffn_up_kernel, tn=tn, tiles_per_seq=S // tm),
        grid=(R // tm, nj),
        in_specs=[pl.BlockSpec((tm, D), lambda i, j: (i, 0)),
                  pl.BlockSpec((None, D, tn), lambda i, j: (layer, 0, j)),
                  pl.BlockSpec((None, D, tn), lambda i, j: (layer, 0, j)),
                  pl.BlockSpec((None, None, CONV_W, 2 * tn), lambda i, j: (layer, j, 0, 0)),
                  pl.BlockSpec((None, None, 1, 2 * tn), lambda i, j: (layer, j, 0, 0))],
        out_specs=pl.BlockSpec((tm, tn), lambda i, j: (i, j)),
        out_shape=jax.ShapeDtypeStruct((R, ffp), BF16),
        scratch_shapes=[pltpu.VMEM((nj, SUBLANE, 2 * tn), F32)],
        compiler_params=_cparams(("arbitrary", "arbitrary")),
        name="ffn_up",
    )(hn, wa, wv, cw, cb)


def _ffn_down_kernel(h_ref, act_ref, wd_ref, g_ref, o_ref, *, final):
    for rb in range(h_ref.shape[0] // OUT_SUB):
        rows = slice(rb * OUT_SUB, (rb + 1) * OUT_SUB)
        h = h_ref[rows, :] + _dot(act_ref[rows, :], wd_ref[...])
        o_ref[rows, :] = _rms(h, g_ref[...]) if final else h


def _ffn_down(h, act, wd, g, final, layer):
    R, D = h.shape
    tm = min(R, ROW_TILE)
    ff = wd.shape[1]
    row_spec = pl.BlockSpec((tm, D), lambda i: (i, 0))
    return pl.pallas_call(
        functools.partial(_ffn_down_kernel, final=final),
        grid=(R // tm,),
        in_specs=[row_spec, pl.BlockSpec((tm, ff), lambda i: (i, 0)),
                  _layer_spec(wd, layer), _const_spec((1, D))],
        out_specs=row_spec,
        out_shape=jax.ShapeDtypeStruct((R, D), F32),
        compiler_params=_cparams(("parallel",)),
        name="ffn_down",
    )(h, act, wd, g.reshape(1, D))


def _pad_cols(w, n):
    return jnp.pad(w, ((0, 0), (0, n - w.shape[1])))


def _split_w_in(w_in, D):
    bw = D // 4
    hk = bw // 2
    sizes = [hk, hk, bw, bw, bw, hk, hk, bw, bw, GLA_RANK, bw, N_BRANCH * D]
    offs = np.concatenate([[0], np.cumsum(sizes)])
    seg = lambda a, b: w_in[:, offs[a]:offs[b]]
    w_ret = seg(0, 4)
    w_s5 = seg(4, 5)
    w_gla = jnp.concatenate([seg(5, 9), _pad_cols(seg(9, 10), LANE)], axis=1)
    w_pool = seg(10, 11)
    return [w.astype(BF16) for w in (w_ret, w_s5, w_gla, w_pool)], seg(11, 12).astype(BF16)


def _ffn_weights(w_up, conv_w, conv_b, w_down, tn):
    D, ff2 = w_up.shape
    ff = ff2 // 2
    ffp = -(-ff // tn) * tn
    nj = ffp // tn
    wa = _pad_cols(w_up[:, :ff], ffp).astype(BF16)
    wv = _pad_cols(w_up[:, ff:], ffp).astype(BF16)
    tiles = lambda m: _pad_cols(m, ffp).reshape(m.shape[0], nj, tn)
    cw = jnp.concatenate([tiles(conv_w[:, :ff]), tiles(conv_w[:, ff:])], axis=2).transpose(1, 0, 2)
    cbr = conv_b.reshape(1, ff2)
    cb = jnp.concatenate([tiles(cbr[:, :ff]), tiles(cbr[:, ff:])], axis=2).transpose(1, 0, 2)
    return wa, wv, cw, cb, w_down.astype(BF16)


def kernel(x, positions, norm_mix_g, w_in, s5_a_re, s5_a_im, s5_log_dt, s5_b_re, s5_b_im, s5_c_re, s5_c_im, s5_d, s5_w_glu, gla_w_gate, gla_b_gate, pool_w, pool_scale, w_branch, w_out, norm_ffn_g, w_up, conv_w, conv_b, w_down, final_g):
    B, S, D = x.shape
    depth = w_in.shape[0]
    bw = D // 4
    ret_dk, ret_dv = bw // (2 * RET_HEADS), bw // RET_HEADS
    gla_dk, gla_dv = bw // (2 * GLA_HEADS), bw // GLA_HEADS
    assert S % CHUNK == 0 and S % min(S, ROW_TILE) == 0 and B == SUBLANE
    assert ret_dk // 2 * 4 == LANE and bw == S5_BLOCK * LANE

    inv = ROPE_BASE ** (-jnp.arange(0, ret_dk, 2, dtype=F32) / ret_dk)
    ang = positions.astype(F32)[..., None] * inv
    ang = jnp.tile(ang, (1, 1, LANE // (ret_dk // 2))).reshape(B * S, LANE)
    rope = _rope_tables(ang, ret_dk // 2)

    ws, w_gate = jax.vmap(lambda w: _split_w_in(w, D))(w_in)
    a_row, bblk, cblk = jax.vmap(_s5_params)(s5_a_re, s5_a_im, s5_log_dt, s5_b_re, s5_b_im,
                                             s5_c_re, s5_c_im)
    wglu = s5_w_glu.astype(BF16)
    wg = jnp.pad(gla_w_gate, ((0, 0), (0, LANE - GLA_RANK), (0, 0))).astype(BF16)
    pw = pool_w.astype(BF16)
    wbr = w_branch.astype(BF16)
    wo = w_out.astype(BF16)
    wa, wv, cw, cb, wd = jax.vmap(lambda a, b, c, d: _ffn_weights(a, b, c, d, FF_TILE))(
        w_up, conv_w, conv_b, w_down)

    h = x.reshape(B * S, D)
    for l in range(depth):
        hn, p_ret, p_s5, p_gla, p_pool = _proj(h, norm_mix_g[l], ws, l)
        ya, yd = _retention_pool(p_ret, rope, p_pool, pw[l], pool_scale[l].reshape(1, bw),
                                 B, S, RET_HEADS, ret_dk, ret_dv)
        yb = _s5(p_s5.reshape(B, S, bw), a_row[l], bblk, cblk, s5_d[l].reshape(1, bw),
                 wglu, l).reshape(B * S, bw)
        yc = _gla(p_gla, wg[l], gla_b_gate[l].reshape(1, -1), B, S, GLA_HEADS, gla_dk, gla_dv)
        merged = _merge(hn, [ya, yb, yc, yd], w_gate, wbr, l)
        h, hn = _outproj(h, merged, wo, norm_ffn_g[l], l)
        act = _ffn_up(hn, wa, wv, cw, cb, S, l)
        h = _ffn_down(h, act, wd, final_g, final=(l == depth - 1), layer=l)
    return h.reshape(B, S, D)
```
